```python
import math
import jax
import jax.numpy as jnp
from jax import lax
import numpy as np

D_MODEL = 1024
BATCH = 8
SEQ = 4096
DEPTH = 2

GRID_W = 64
CTX_LEN = 256
DA_HEADS = 4
DA_HEAD_DIM = 64
DA_WIDTH = DA_HEADS * 2 * DA_HEAD_DIM
HY_WIDTH = D_MODEL - DA_WIDTH
HY_ORDER = 2
HY_SHORT = 3
HY_EMB = 33
HY_BANDS = (HY_EMB - 1) // 2
HY_FFN = 64
HY_FILTERS = HY_ORDER * 2 * HY_WIDTH
HY_TARGET = 1e-2
HY_MIN_DECAY = math.log(HY_TARGET) / 0.3
HY_MAX_DECAY = math.log(HY_TARGET) / 1.5
IN0_WIDTH = 3 * DA_WIDTH + 3 * HY_WIDTH
CV_WIDTH = D_MODEL
CV_KERNEL = 31
N_GROUPS = 4
EXPERTS_PER_GROUP = 8
N_EXPERTS = N_GROUPS * EXPERTS_PER_GROUP
TOP_K = 2
D_EXPERT = 512
ROPE_THETA = 10000.0
EPS = 1e-6
Q_BLOCK = 128
N_EVEN = (DEPTH + 1) // 2
N_ODD = DEPTH // 2

kernel_name = 'hybrid_diffattn_hyena_conformer_hmoe'

F32 = jnp.float32


def rmsnorm(x, g):
    x32 = x.astype(F32)
    y = x32 * lax.rsqrt(jnp.mean(x32 * x32, axis=-1, keepdims=True) + EPS)
    return (y * g).astype(x.dtype)


def layernorm(x, g, b):
    x32 = x.astype(F32)
    mu = jnp.mean(x32, axis=-1, keepdims=True)
    var = jnp.mean(jnp.square(x32 - mu), axis=-1, keepdims=True)
    return ((x32 - mu) * lax.rsqrt(var + EPS) * g + b).astype(x.dtype)


def adaln(cvec, w, b):
    m = (jax.nn.silu(cvec) @ w + b).reshape((-1, 1, 6 * D_MODEL))
    return jnp.split(m, 6, axis=-1)


def modulate(h, shift, scale):
    return h * (1.0 + scale) + shift


def _heads(t, n_heads):
    bsz, n, _ = t.shape
    return t.reshape(bsz, n, n_heads, -1).transpose(0, 2, 1, 3)


def rope_2d(x):
    n, hd = x.shape[2], x.shape[3]
    rows = n // GRID_W
    row = jnp.repeat(jnp.arange(rows, dtype=F32), GRID_W)
    col = jnp.tile(jnp.arange(GRID_W, dtype=F32), rows)
    half = hd // 2
    inv = ROPE_THETA ** (-jnp.arange(0, half, 2, dtype=F32) / half)
    x32 = x.astype(F32)

    def rot(xp, pos):
        ang = pos[:, None] * inv
        cos, sin = jnp.cos(ang), jnp.sin(ang)
        a, b = jnp.split(xp, 2, axis=-1)
        return jnp.concatenate([a * cos - b * sin, a * sin + b * cos], axis=-1)

    out = jnp.concatenate([rot(x32[..., :half], row), rot(x32[..., half:], col)], axis=-1)
    return out.astype(x.dtype)


def diff_attention(q, k, v, lam, subln_g, lam_init):
    bsz, h2, lq, d = q.shape
    nh = h2 // 2
    nb = lq // Q_BLOCK
    qb = q.reshape(bsz, h2, nb, Q_BLOCK, d).transpose(2, 0, 1, 3, 4)
    scale = d ** -0.5

    def block(qi):
        s = jnp.einsum('bhqd,bhkd->bhqk', qi, k).astype(F32) * scale
        p = jax.nn.softmax(s, axis=-1).reshape(bsz, nh, 2, Q_BLOCK, -1)
        a = p[:, :, 0] - lam * p[:, :, 1]
        o = jnp.einsum('bhqk,bhkv->bhqv', a.astype(v.dtype), v)
        return rmsnorm(o, subln_g) * (1.0 - lam_init)

    o = lax.map(block, qb)
    return o.transpose(1, 0, 3, 2, 4).reshape(bsz, lq, nh * 2 * d)


def depthwise_conv(x, w, b):
    kw, ch = w.shape
    y = lax.conv_general_dilated(x, w[:, None, :].astype(x.dtype), window_strides=(1,),
                                 padding=[((kw - 1) // 2, kw // 2)],
                                 dimension_numbers=('NWC', 'WIO', 'NWC'),
                                 feature_group_count=ch)
    return y + b


def hyena_kernels(n, w1, b1, fr1, w2, b2, fr2, w3, b3):
    t = jnp.linspace(0.0, 1.0, n, dtype=F32)[:, None]
    ang = (2.0 * math.pi / n) * jnp.arange(n, dtype=F32)[:, None] * jnp.linspace(1e-4, HY_BANDS - 1, HY_BANDS, dtype=F32)[None, :]
    z = jnp.concatenate([t, jnp.cos(ang), -jnp.sin(ang)], axis=-1).astype(w1.dtype)
    h = jnp.sin(fr1 * (z @ w1 + b1))
    h = jnp.sin(fr2 * (h @ w2 + b2))
    h = (h @ w3 + b3).astype(F32).reshape(n, HY_ORDER, 2, HY_WIDTH)
    deltas = jnp.abs(jnp.linspace(HY_MIN_DECAY, HY_MAX_DECAY, HY_WIDTH, dtype=F32))
    h = h * jnp.exp(-t[:, :, None, None] * deltas)
    fwd, bwd = h[:, :, 0], h[:, :, 1]
    kc = jnp.concatenate([fwd, jnp.zeros_like(fwd[:1]), bwd[:0:-1]], axis=0)
    return kc / jnp.sum(jnp.abs(kc), axis=0, keepdims=True)


def fft_long_conv(z, kf, bias):
    n = z.shape[1]
    z32 = z.astype(F32)
    y = jnp.fft.irfft(jnp.fft.rfft(z32, n=2 * n, axis=1) * kf, n=2 * n, axis=1)[:, :n]
    return (y + z32 * bias.astype(F32)).astype(z.dtype)


def hyena_mixer(u, short_w, short_b, w1, b1, fr1, w2, b2, fr2, w3, b3, bias):
    n = u.shape[1]
    u = depthwise_conv(u, short_w, short_b)
    v, x1, x2 = jnp.split(u, 3, axis=-1)
    kf = jnp.fft.rfft(hyena_kernels(n, w1, b1, fr1, w2, b2, fr2, w3, b3), axis=0)
    z = x1 * fft_long_conv(v, kf[:, 0], bias[0])
    return x2 * fft_long_conv(z, kf[:, 1], bias[1])


def conformer_conv(h, w1, b1, dw_w, dw_b, ln_g, ln_b, w2, b2):
    a = h @ w1 + b1
    a, g = jnp.split(a, 2, axis=-1)
    a = a * jax.nn.sigmoid(g)
    a = depthwise_conv(a, dw_w, dw_b)
    a = jax.nn.silu(layernorm(a, ln_g, ln_b))
    return a @ w2 + b2


def hier_moe(h, wg, bg, we, be, w_gate, w_up, w_down):
    bsz, n, d = h.shape
    t = h.reshape(-1, d)
    ntok = t.shape[0]
    g_prob = jax.nn.softmax((t @ wg + bg).astype(F32), axis=-1)
    g_w, g_idx = lax.top_k(g_prob, 1)
    e_logits = (t @ we + be).astype(F32).reshape(ntok, N_GROUPS, EXPERTS_PER_GROUP)
    e_logits = e_logits[jnp.arange(ntok), g_idx[:, 0]]
    e_w, e_idx = lax.top_k(jax.nn.softmax(e_logits, axis=-1), TOP_K)
    e_w = e_w / jnp.sum(e_w, axis=-1, keepdims=True)
    wts = g_w * e_w
    gid = g_idx * EXPERTS_PER_GROUP + e_idx
    comb = jnp.sum(jax.nn.one_hot(gid, N_EXPERTS, dtype=F32) * wts[..., None], axis=1)
    y = jnp.zeros((ntok, d), F32)
    for e in range(N_EXPERTS):
        he = jax.nn.silu(t @ w_gate[e]) * (t @ w_up[e])
        y = y + comb[:, e:e + 1] * (he @ w_down[e]).astype(F32)
    return y.astype(h.dtype).reshape(bsz, n, d)


def setup_inputs(seed: int = 0) -> dict:
    key = jax.random.key(seed)
    ks = iter(jax.random.split(key, 48))

    def nrm(shape, scale):
        return jax.random.normal(next(ks), shape, jnp.float32) * scale

    def gain(shape, s=0.02):
        return 1.0 + nrm(shape, s)

    D = D_MODEL
    C3 = 3 * HY_WIDTH
    DO = DA_WIDTH + HY_WIDTH
    return dict(
        x=nrm((BATCH, SEQ, D), 1.0),
        c=nrm((BATCH, D), 1.0),
        ctx=nrm((BATCH, CTX_LEN, D), 1.0),
        c_ctx=nrm((D,), 1.0),
        ada_w=nrm((DEPTH, D, 6 * D), 0.5 * D ** -0.5),
        ada_b=nrm((DEPTH, 6 * D), 0.01),
        norm1_g=gain((DEPTH, D)),
        norm2_g=gain((DEPTH, D)),
        final_g=gain((D,)),
        w_in0=nrm((N_EVEN, D, IN0_WIDTH), D ** -0.5),
        w_out0=nrm((N_EVEN, DO, D), DO ** -0.5),
        lam_q1=nrm((N_EVEN, DA_HEAD_DIM), 0.1),
        lam_k1=nrm((N_EVEN, DA_HEAD_DIM), 0.1),
        lam_q2=nrm((N_EVEN, DA_HEAD_DIM), 0.1),
        lam_k2=nrm((N_EVEN, DA_HEAD_DIM), 0.1),
        subln_g=gain((N_EVEN, 2 * DA_HEAD_DIM)),
        hy_short_w=nrm((N_EVEN, HY_SHORT, C3), HY_SHORT ** -0.5),
        hy_short_b=nrm((N_EVEN, C3), 0.01),
        hy_w1=nrm((N_EVEN, HY_EMB, HY_FFN), HY_EMB ** -0.5),
        hy_b1=nrm((N_EVEN, HY_FFN), 0.1),
        hy_fr1=gain((N_EVEN, HY_FFN), 0.1),
        hy_w2=nrm((N_EVEN, HY_FFN, HY_FFN), HY_FFN ** -0.5),
        hy_b2=nrm((N_EVEN, HY_FFN), 0.1),
        hy_fr2=gain((N_EVEN, HY_FFN), 0.1),
        hy_w3=nrm((N_EVEN, HY_FFN, HY_FILTERS), HY_FFN ** -0.5),
        hy_b3=nrm((N_EVEN, HY_FILTERS), 0.01),
        hy_bias=nrm((N_EVEN, HY_ORDER, HY_WIDTH), 0.5),
        cv_w1=nrm((N_ODD, D, 2 * CV_WIDTH), D ** -0.5),
        cv_b1=nrm((N_ODD, 2 * CV_WIDTH), 0.01),
        cv_dw_w=nrm((N_ODD, CV_KERNEL, CV_WIDTH), CV_KERNEL ** -0.5),
        cv_dw_b=nrm((N_ODD, CV_WIDTH), 0.01),
        cv_ln_g=gain((N_ODD, CV_WIDTH)),
        cv_ln_b=nrm((N_ODD, CV_WIDTH), 0.01),
        cv_w2=nrm((N_ODD, CV_WIDTH, D), CV_WIDTH ** -0.5),
        cv_b2=nrm((N_ODD, D), 0.01),
        moe_wg=nrm((DEPTH, D, N_GROUPS), D ** -0.5),
        moe_bg=nrm((DEPTH, N_GROUPS), 0.01),
        moe_we=nrm((DEPTH, D, N_EXPERTS), D ** -0.5),
        moe_be=nrm((DEPTH, N_EXPERTS), 0.01),
        moe_w_gate=nrm((DEPTH, N_EXPERTS, D, D_EXPERT), D ** -0.5),
        moe_w_up=nrm((DEPTH, N_EXPERTS, D, D_EXPERT), D ** -0.5),
        moe_w_down=nrm((DEPTH, N_EXPERTS, D_EXPERT, D), D_EXPERT ** -0.5),
    )


def reference(x, c, ctx, c_ctx, ada_w, ada_b, norm1_g, norm2_g, final_g,
              w_in0, w_out0, lam_q1, lam_k1, lam_q2, lam_k2, subln_g,
              hy_short_w, hy_short_b, hy_w1, hy_b1, hy_fr1, hy_w2, hy_b2, hy_fr2, hy_w3, hy_b3, hy_bias,
              cv_w1, cv_b1, cv_dw_w, cv_dw_b, cv_ln_g, cv_ln_b, cv_w2, cv_b2,
              moe_wg, moe_bg, moe_we, moe_be, moe_w_gate, moe_w_up, moe_w_down):
    ctx_s = ctx
    H2 = 2 * DA_HEADS
    for i in range(DEPTH):
        j = i // 2
        ctx_live = any(m % 2 == 0 for m in range(i + 1, DEPTH))
        sh1, sc1, g1, sh2, sc2, g2 = adaln(c, ada_w[i], ada_b[i])
        if i % 2 == 0 or ctx_live:
            csh1, csc1, cg1, csh2, csc2, cg2 = adaln(c_ctx, ada_w[i], ada_b[i])
        hx = modulate(rmsnorm(x, norm1_g[i]), sh1, sc1)
        if i % 2 == 0:
            w_in = w_in0[j]
            hy = (hy_short_w[j], hy_short_b[j], hy_w1[j], hy_b1[j], hy_fr1[j], hy_w2[j], hy_b2[j],
                  hy_fr2[j], hy_w3[j], hy_b3[j], hy_bias[j])
            lam_init = 0.8 - 0.6 * math.exp(-0.3 * i)
            lam = (jnp.exp(jnp.sum(lam_q1[j] * lam_k1[j]).astype(F32))
                   - jnp.exp(jnp.sum(lam_q2[j] * lam_k2[j]).astype(F32)) + lam_init)
            hc = modulate(rmsnorm(ctx_s, norm1_g[i]), csh1, csc1)
            if ctx_live:
                pc = hc @ w_in
                pc_kv = pc[..., DA_WIDTH:3 * DA_WIDTH]
            else:
                pc_kv = hc @ w_in[:, DA_WIDTH:3 * DA_WIDTH]
            kc = _heads(pc_kv[..., :DA_WIDTH], H2)
            vc = _heads(pc_kv[..., DA_WIDTH:], DA_HEADS)
            px = hx @ w_in
            q = rope_2d(_heads(px[..., :DA_WIDTH], H2))
            k = rope_2d(_heads(px[..., DA_WIDTH:2 * DA_WIDTH], H2))
            v = _heads(px[..., 2 * DA_WIDTH:3 * DA_WIDTH], DA_HEADS)
            o_a = diff_attention(q, jnp.concatenate([kc, k], axis=2), jnp.concatenate([vc, v], axis=2),
                                 lam, subln_g[j], lam_init)
            o_b = hyena_mixer(px[..., 3 * DA_WIDTH:], *hy)
            x_new = x + g1 * (jnp.concatenate([o_a, o_b], axis=-1) @ w_out0[j])
            if ctx_live:
                qc = _heads(pc[..., :DA_WIDTH], H2)
                oc_a = diff_attention(qc, kc, vc, lam, subln_g[j], lam_init)
                oc_b = hyena_mixer(pc[..., 3 * DA_WIDTH:], *hy)
                ctx_s = ctx_s + cg1 * (jnp.concatenate([oc_a, oc_b], axis=-1) @ w_out0[j])
            x = x_new
        else:
            cv = (cv_w1[j], cv_b1[j], cv_dw_w[j], cv_dw_b[j], cv_ln_g[j], cv_ln_b[j], cv_w2[j], cv_b2[j])
            x = x + g1 * conformer_conv(hx, *cv)
            if ctx_live:
                hc = modulate(rmsnorm(ctx_s, norm1_g[i]), csh1, csc1)
                ctx_s = ctx_s + cg1 * conformer_conv(hc, *cv)
        moe_p = (moe_wg[i], moe_bg[i], moe_we[i], moe_be[i], moe_w_gate[i], moe_w_up[i], moe_w_down[i])
        x = x + g2 * hier_moe(modulate(rmsnorm(x, norm2_g[i]), sh2, sc2), *moe_p)
        if ctx_live:
            ctx_s = ctx_s + cg2 * hier_moe(modulate(rmsnorm(ctx_s, norm2_g[i]), csh2, csc2), *moe_p)
    return rmsnorm(x, final_g)
```

```python
import functools
import math

import jax
import jax.numpy as jnp
from jax import lax
from jax.experimental import pallas as pl
from jax.experimental.pallas import tpu as pltpu

F32 = jnp.float32
BF16 = jnp.bfloat16

GRID_W = 64
DA_HEADS = 4
DA_HEAD_DIM = 64
DA_WIDTH = DA_HEADS * 2 * DA_HEAD_DIM
HY_ORDER = 2
HY_TARGET = 1e-2
HY_MIN_DECAY = math.log(HY_TARGET) / 0.3
HY_MAX_DECAY = math.log(HY_TARGET) / 1.5
N_GROUPS = 4
EXPERTS_PER_GROUP = 8
N_EXPERTS = N_GROUPS * EXPERTS_PER_GROUP
ROPE_THETA = 10000.0
EPS = 1e-6

LANES = 128
SUBLANES = 8
VMEM_LIMIT = 52 * 1024 * 1024
ROUTE_LANE0 = N_GROUPS
MOE_TILE = 256
LC_P = 128
LC_W = 1024


def _params(sem):
    return pltpu.CompilerParams(dimension_semantics=sem, vmem_limit_bytes=VMEM_LIMIT)


def _split(a):
    hi = a.astype(BF16)
    lo = (a - hi.astype(F32)).astype(BF16)
    return hi, lo


def _dot(a, b):
    return jnp.dot(a, b, preferred_element_type=F32)


def _dot3(a, b):
    ah, al = _split(a)
    bh, bl = _split(b)
    return _dot(ah, bh) + _dot(al, bh) + _dot(ah, bl)


def _norm_mod(x, g, sh, sc):
    y = x * lax.rsqrt(jnp.mean(x * x, axis=-1, keepdims=True) + EPS)
    return (y * g) * (1.0 + sc) + sh


def _ada_kernel(c_ref, w_ref, b_ref, o_ref):
    c = c_ref[...]
    s = c * jax.nn.sigmoid(c)
    o_ref[0] = _dot3(s, w_ref[0]) + b_ref[0]


def _ada(cs, ada_w, ada_b):
    depth, d, n6 = ada_w.shape
    rows = cs.shape[0]
    tn = 1536
    return pl.pallas_call(
        _ada_kernel,
        out_shape=jax.ShapeDtypeStruct((depth, rows, n6), F32),
        grid=(depth, n6 // tn),
        in_specs=[pl.BlockSpec((rows, d), lambda l, j: (0, 0)),
                  pl.BlockSpec((1, d, tn), lambda l, j: (l, 0, j)),
                  pl.BlockSpec((1, 1, tn), lambda l, j: (l, 0, j))],
        out_specs=pl.BlockSpec((1, rows, tn), lambda l, j: (l, 0, j)),
        compiler_params=_params(("parallel", "parallel")),
        name="adaln",
    )(cs, ada_w, ada_b.reshape(depth, 1, n6))


def _inproj_kernel(x_ref, g_ref, sh_ref, sc_ref, cos_ref, sin_ref, w_ref,
                   q_ref, k_ref, v_ref, u_ref, *, d_attn, d_hy):
    h = _norm_mod(x_ref[...], g_ref[...], sh_ref[0], sc_ref[0]).astype(BF16)
    cos = cos_ref[...]
    sin = sin_ref[...]
    rot0 = 3 * d_attn + d_hy
    cw = cos.shape[1]

    def mm(c0, c1):
        return _dot(h, w_ref[:, c0:c1])

    for j in range(d_attn // cw):
        qa = mm(j * cw, (j + 1) * cw)
        qr = mm(rot0 + j * cw, rot0 + (j + 1) * cw)
        q_ref[:, j * cw:(j + 1) * cw] = ((qa * cos + qr * sin) * (DA_HEAD_DIM ** -0.5)).astype(BF16)
        ka = mm(d_attn + j * cw, d_attn + (j + 1) * cw)
        kr = mm(rot0 + d_attn + j * cw, rot0 + d_attn + (j + 1) * cw)
        k_ref[:, j * cw:(j + 1) * cw] = (ka * cos + kr * sin).astype(BF16)
    v_ref[...] = mm(2 * d_attn, 3 * d_attn).astype(BF16)
    for j in range(d_hy // 512):
        u_ref[:, j * 512:(j + 1) * 512] = mm(3 * d_attn + j * 512, 3 * d_attn + (j + 1) * 512)


def _inproj(x2, g, sh, sc, cos, sin, w_ext, seq, tm=512):
    t, d = x2.shape
    tpb = seq // tm
    d_attn = DA_WIDTH
    d_hy = w_ext.shape[1] - 5 * d_attn
    cw = cos.shape[1]
    kern = functools.partial(_inproj_kernel, d_attn=d_attn, d_hy=d_hy)
    return pl.pallas_call(
        kern,
        out_shape=(jax.ShapeDtypeStruct((t, d_attn), BF16), jax.ShapeDtypeStruct((t, d_attn), BF16),
                   jax.ShapeDtypeStruct((t, d_attn), BF16), jax.ShapeDtypeStruct((t, d_hy), F32)),
        grid=(t // tm,),
        in_specs=[pl.BlockSpec((tm, d), lambda i: (i, 0)),
                  pl.BlockSpec((1, d), lambda i: (0, 0)),
                  pl.BlockSpec((1, 1, d), lambda i: (i // tpb, 0, 0)),
                  pl.BlockSpec((1, 1, d), lambda i: (i // tpb, 0, 0)),
                  pl.BlockSpec((tm, cw), lambda i: (i % tpb, 0)),
                  pl.BlockSpec((tm, cw), lambda i: (i % tpb, 0)),
                  pl.BlockSpec(w_ext.shape, lambda i: (0, 0))],
        out_specs=(pl.BlockSpec((tm, d_attn), lambda i: (i, 0)), pl.BlockSpec((tm, d_attn), lambda i: (i, 0)),
                   pl.BlockSpec((tm, d_attn), lambda i: (i, 0)), pl.BlockSpec((tm, d_hy), lambda i: (i, 0))),
        compiler_params=_params(("parallel",)),
        name="inproj0",
    )(x2, g, sh, sc, cos, sin, w_ext)


def _ctxproj_kernel(x_ref, g_ref, sh_ref, sc_ref, w_ref, k_ref, v_ref):
    h = _norm_mod(x_ref[...], g_ref[...], sh_ref[0], sc_ref[0]).astype(BF16)
    n = k_ref.shape[1]
    k_ref[...] = _dot(h, w_ref[:, :n]).astype(BF16)
    v_ref[...] = _dot(h, w_ref[:, n:]).astype(BF16)


def _ctxproj(c2, g, sh, sc, w_kv, tm):
    t, d = c2.shape
    n = w_kv.shape[1] // 2
    return pl.pallas_call(
        _ctxproj_kernel,
        out_shape=(jax.ShapeDtypeStruct((t, n), BF16), jax.ShapeDtypeStruct((t, n), BF16)),
        grid=(t // tm,),
        in_specs=[pl.BlockSpec((tm, d), lambda i: (i, 0)),
                  pl.BlockSpec((1, d), lambda i: (0, 0)),
                  pl.BlockSpec((1, 1, d), lambda i: (0, 0, 0)),
                  pl.BlockSpec((1, 1, d), lambda i: (0, 0, 0)),
                  pl.BlockSpec(w_kv.shape, lambda i: (0, 0))],
        out_specs=(pl.BlockSpec((tm, n), lambda i: (i, 0)), pl.BlockSpec((tm, n), lambda i: (i, 0))),
        compiler_params=_params(("parallel",)),
        name="ctxproj",
    )(c2, g, sh, sc, w_kv)


def _attn_kernel(lam_ref, g_ref, q_ref, kt_ref, v_ref, o_ref, *, lam_init):
    lp = lam_ref[...]
    lam = (jnp.exp(jnp.sum(lp[0:1] * lp[1:2], axis=-1, keepdims=True))
           - jnp.exp(jnp.sum(lp[2:3] * lp[3:4], axis=-1, keepdims=True)) + lam_init)
    q = q_ref[...]
    lane = lax.broadcasted_iota(jnp.int32, q.shape, 1)
    zero = jnp.zeros_like(q)
    kt = kt_ref[0]
    s1 = _dot(jnp.where(lane < DA_HEAD_DIM, q, zero), kt)
    s2 = _dot(jnp.where(lane >= DA_HEAD_DIM, q, zero), kt)
    e1 = jnp.exp(s1 - jnp.max(s1, axis=-1, keepdims=True))
    e2 = jnp.exp(s2 - jnp.max(s2, axis=-1, keepdims=True))
    r1 = 1.0 / jnp.sum(e1, axis=-1, keepdims=True)
    r2 = lam / jnp.sum(e2, axis=-1, keepdims=True)
    a = (e1 * r1 - e2 * r2).astype(BF16)
    o = _dot(a, v_ref[0])
    y = o * lax.rsqrt(jnp.mean(o * o, axis=-1, keepdims=True) + EPS)
    o_ref[...] = (y * g_ref[...]) * (1.0 - lam_init)


def _attention(q, kt_all, v_all, lam_p, subln_g, seq, lam_init, tq=256):
    t = q.shape[0]
    b, _, lk = kt_all.shape
    hw = 2 * DA_HEAD_DIM
    nq = seq // tq
    kern = functools.partial(_attn_kernel, lam_init=lam_init)
    return pl.pallas_call(
        kern,
        out_shape=jax.ShapeDtypeStruct((t, DA_WIDTH), F32),
        grid=(b, DA_HEADS, nq),
        in_specs=[pl.BlockSpec(lam_p.shape, lambda bi, h, i: (0, 0)),
                  pl.BlockSpec((1, hw), lambda bi, h, i: (0, 0)),
                  pl.BlockSpec((tq, hw), lambda bi, h, i: (bi * nq + i, h)),
                  pl.BlockSpec((1, hw, lk), lambda bi, h, i: (bi, h, 0)),
                  pl.BlockSpec((1, lk, hw), lambda bi, h, i: (bi, 0, h))],
        out_specs=pl.BlockSpec((tq, hw), lambda bi, h, i: (bi * nq + i, h)),
        compiler_params=_params(("parallel", "parallel", "parallel")),
        name="diff_attn",
    )(lam_p, subln_g, q, kt_all, v_all)


def _dwconv_kernel(x_ref, w_ref, b_ref, o_ref, pad_ref, *, kw, halo):
    s = x_ref.shape[1]
    pl_ = (kw - 1) // 2
    zeros = jnp.zeros((halo, x_ref.shape[2]), F32)
    pad_ref[0:halo, :] = zeros
    pad_ref[halo + s:halo + s + halo, :] = zeros
    pad_ref[halo:halo + s, :] = x_ref[0]
    w = w_ref[...]
    acc = jnp.zeros((s, x_ref.shape[2]), F32) + b_ref[...]
    for j in range(kw):
        off = halo - pl_ + j
        acc = acc + pad_ref[off:off + s, :] * w[j:j + 1, :]
    o_ref[0] = acc


def _dwconv(x3, w, bias):
    b, s, c = x3.shape
    kw = w.shape[0]
    halo = 2 * SUBLANES
    kern = functools.partial(_dwconv_kernel, kw=kw, halo=halo)
    return pl.pallas_call(
        kern,
        out_shape=jax.ShapeDtypeStruct((b, s, c), F32),
        grid=(b, c // LANES),
        in_specs=[pl.BlockSpec((1, s, LANES), lambda bi, ci: (bi, 0, ci)),
                  pl.BlockSpec((kw, LANES), lambda bi, ci: (0, ci)),
                  pl.BlockSpec((1, LANES), lambda bi, ci: (0, ci))],
        out_specs=pl.BlockSpec((1, s, LANES), lambda bi, ci: (bi, 0, ci)),
        scratch_shapes=[pltpu.VMEM((s + 2 * halo, LANES), F32)],
        compiler_params=_params(("parallel", "parallel")),
        name="dwconv",
    )(x3, w, bias.reshape(1, c))


def _filter_kernel(zt_ref, t_ref, w1_ref, b1_ref, f1_ref, w2_ref, b2_ref, f2_ref, w3_ref, b3_ref, dl_ref,
                   o_ref, h_ref, *, n):
    @pl.when(pl.program_id(0) == 0)
    def _():
        h1 = jnp.sin(f1_ref[...] * (_dot3(w1_ref[...], zt_ref[...]) + b1_ref[...]))
        h_ref[...] = jnp.sin(f2_ref[...] * (_dot3(w2_ref[...], h1) + b2_ref[...]))

    h2 = h_ref[...]
    rows = o_ref.shape[0]
    fwd = _dot3(w3_ref[0], h2[:, :n]) + b3_ref[0]
    bwd = _dot3(w3_ref[1], h2[:, n:]) + b3_ref[1]
    decay = jnp.exp(-t_ref[...] * dl_ref[...])
    lane = lax.broadcasted_iota(jnp.int32, (rows, n), 1)
    kf = fwd * decay[:, :n]
    kb = jnp.where(lane == 0, 0.0, bwd * decay[:, n:])
    inv = 1.0 / (jnp.sum(jnp.abs(kf), axis=-1, keepdims=True) + jnp.sum(jnp.abs(kb), axis=-1, keepdims=True))
    kf = kf * inv
    kb = kb * inv
    o_ref[:, 0:LANES] = kb[:, n - LANES:]
    o_ref[:, LANES:LANES + n] = kf
    o_ref[:, LANES + n:] = kb


def _hyena_filters(n, w1, b1, fr1, w2, b2, fr2, w3, b3):
    emb, ffn = w1.shape
    c = w3.shape[1] // (2 * HY_ORDER)
    bands = (emb - 1) // 2
    t = jnp.linspace(0.0, 1.0, n, dtype=F32)[:, None]
    ang = (2.0 * math.pi / n) * jnp.arange(n, dtype=F32)[:, None] * jnp.linspace(1e-4, bands - 1, bands, dtype=F32)[None, :]
    z = jnp.concatenate([t, jnp.cos(ang), -jnp.sin(ang)], axis=-1)
    rev = (n - jnp.arange(n)) % n
    z2 = jnp.concatenate([z, z[rev]], axis=0)
    t2 = jnp.concatenate([t, t[rev]], axis=0).reshape(1, 2 * n)
    emb_p = ((emb + SUBLANES - 1) // SUBLANES) * SUBLANES
    zt = jnp.zeros((emb_p, 2 * n), F32).at[:emb].set(z2.T)
    w1t = jnp.zeros((ffn, emb_p), F32).at[:, :emb].set(w1.T)
    deltas = jnp.abs(jnp.linspace(HY_MIN_DECAY, HY_MAX_DECAY, c, dtype=F32))
    w3t = w3.T.reshape(HY_ORDER, 2, c, ffn).transpose(1, 0, 2, 3).reshape(2, HY_ORDER * c, ffn)
    b3t = b3.reshape(HY_ORDER, 2, c).transpose(1, 0, 2).reshape(2, HY_ORDER * c, 1)
    dl = jnp.tile(deltas, HY_ORDER).reshape(HY_ORDER * c, 1)
    rows = LANES
    kern = functools.partial(_filter_kernel, n=n)
    col = lambda v: v.reshape(ffn, 1)
    return pl.pallas_call(
        kern,
        out_shape=jax.ShapeDtypeStruct((HY_ORDER * c, LANES + 2 * n), F32),
        grid=(HY_ORDER * c // rows,),
        in_specs=[pl.BlockSpec(zt.shape, lambda i: (0, 0)),
                  pl.BlockSpec(t2.shape, lambda i: (0, 0)),
                  pl.BlockSpec(w1t.shape, lambda i: (0, 0)),
                  pl.BlockSpec((ffn, 1), lambda i: (0, 0)),
                  pl.BlockSpec((ffn, 1), lambda i: (0, 0)),
                  pl.BlockSpec((ffn, ffn), lambda i: (0, 0)),
                  pl.BlockSpec((ffn, 1), lambda i: (0, 0)),
                  pl.BlockSpec((ffn, 1), lambda i: (0, 0)),
                  pl.BlockSpec((2, rows, ffn), lambda i: (0, i, 0)),
                  pl.BlockSpec((2, rows, 1), lambda i: (0, i, 0)),
                  pl.BlockSpec((rows, 1), lambda i: (i, 0))],
        out_specs=pl.BlockSpec((rows, LANES + 2 * n), lambda i: (i, 0)),
        scratch_shapes=[pltpu.VMEM((ffn, 2 * n), F32)],
        compiler_params=_params(("arbitrary",)),
        name="hyena_filters",
    )(zt, t2, w1t, col(b1), col(fr1), w2.T, col(b2), col(fr2), w3t, b3t, dl)


def _longconv_kernel(z_ref, gate_ref, kc_ref, bias_ref, o_ref, r_ref, acc_ref, *, nb, bsz, cb):
    p = LC_P
    n2 = 2 * nb * p
    rows = nb * bsz

    def channel(c, carry):
        w = min(LC_W, n2)
        for ci in range(n2 // w):
            win = kc_ref[pl.ds(c, 1), ci * w:ci * w + w + LANES]
            rolled = pltpu.roll(jnp.broadcast_to(win, (p, w + LANES)), 0, 1, stride=1, stride_axis=0)
            r_ref[:, ci * w:(ci + 1) * w] = rolled[:, LANES:].astype(BF16)
        acc_ref[...] = jnp.zeros_like(acc_ref)
        for pi in range(nb):
            d = -nb + 2 * pi
            off = (d % (2 * nb)) * p
            lo = max(0, -d - 1)
            hi = min(nb, nb - d)
            lhs = z_ref[c, lo * bsz:hi * bsz, :].astype(BF16)
            out = _dot(lhs, r_ref[:, off:off + 2 * p])
            for k in range(2):
                dk = d + k
                a0, a1 = max(0, -dk), min(nb, nb - dk)
                if a1 <= a0:
                    continue
                acc_ref[(a0 + dk) * bsz:(a1 + dk) * bsz, :] += out[(a0 - lo) * bsz:(a1 - lo) * bsz, k * p:(k + 1) * p]
        zf = z_ref[c]
        o_ref[c] = gate_ref[c] * (acc_ref[...] + zf * bias_ref[c])
        return carry

    lax.fori_loop(0, cb, channel, 0)


def _longconv(z_cm, gate_cm, kc_ext, bias_cm, order, nb, bsz, cb=8):
    c, rows, p = z_cm.shape
    kern = functools.partial(_longconv_kernel, nb=nb, bsz=bsz, cb=cb)
    cblocks = c // cb
    return pl.pallas_call(
        kern,
        out_shape=jax.ShapeDtypeStruct((c, rows, p), F32),
        grid=(cblocks,),
        in_specs=[pl.BlockSpec((cb, rows, p), lambda i: (i, 0, 0)),
                  pl.BlockSpec((cb, rows, p), lambda i: (i, 0, 0)),
                  pl.BlockSpec((cb, kc_ext.shape[1]), lambda i: (order * cblocks + i, 0)),
                  pl.BlockSpec((cb, 1, p), lambda i: (order * cblocks + i, 0, 0))],
        out_specs=pl.BlockSpec((cb, rows, p), lambda i: (i, 0, 0)),
        scratch_shapes=[pltpu.VMEM((p, 2 * nb * p), BF16), pltpu.VMEM((rows, p), F32)],
        compiler_params=_params(("parallel",)),
        name="hyena_longconv",
    )(z_cm, gate_cm, kc_ext, bias_cm)


def _outproj_kernel(x_ref, g1_ref, a_ref, b_ref, w_ref, o_ref):
    da = a_ref.shape[1]
    y = _dot(a_ref[...].astype(BF16), w_ref[:da, :]) + _dot(b_ref[...].astype(BF16), w_ref[da:, :])
    o_ref[...] = x_ref[...] + g1_ref[0] * y


def _outproj(x2, g1, oa, ob, w, seq, tm=512):
    t, d = x2.shape
    tpb = seq // tm
    return pl.pallas_call(
        _outproj_kernel,
        out_shape=jax.ShapeDtypeStruct((t, d), F32),
        grid=(t // tm,),
        in_specs=[pl.BlockSpec((tm, d), lambda i: (i, 0)),
                  pl.BlockSpec((1, 1, d), lambda i: (i // tpb, 0, 0)),
                  pl.BlockSpec((tm, oa.shape[1]), lambda i: (i, 0)),
                  pl.BlockSpec((tm, ob.shape[1]), lambda i: (i, 0)),
                  pl.BlockSpec(w.shape, lambda i: (0, 0))],
        out_specs=pl.BlockSpec((tm, d), lambda i: (i, 0)),
        compiler_params=_params(("parallel",)),
        name="outproj0",
    )(x2, g1, oa, ob, w)


def _confin_kernel(x_ref, g_ref, sh_ref, sc_ref, w_ref, b_ref, o_ref):
    h = _norm_mod(x_ref[...], g_ref[...], sh_ref[0], sc_ref[0]).astype(BF16)
    n = o_ref.shape[1]
    a = _dot(h, w_ref[:, :n]) + b_ref[:, :n]
    gt = _dot(h, w_ref[:, n:]) + b_ref[:, n:]
    o_ref[...] = a * jax.nn.sigmoid(gt)


def _confin(x2, g, sh, sc, w1, b1, seq, tm=512):
    t, d = x2.shape
    n = w1.shape[1] // 2
    tpb = seq // tm
    return pl.pallas_call(
        _confin_kernel,
        out_shape=jax.ShapeDtypeStruct((t, n), F32),
        grid=(t // tm,),
        in_specs=[pl.BlockSpec((tm, d), lambda i: (i, 0)),
                  pl.BlockSpec((1, d), lambda i: (0, 0)),
                  pl.BlockSpec((1, 1, d), lambda i: (i // tpb, 0, 0)),
                  pl.BlockSpec((1, 1, d), lambda i: (i // tpb, 0, 0)),
                  pl.BlockSpec(w1.shape, lambda i: (0, 0)),
                  pl.BlockSpec((1, 2 * n), lambda i: (0, 0))],
        out_specs=pl.BlockSpec((tm, n), lambda i: (i, 0)),
        compiler_params=_params(("parallel",)),
        name="conformer_in",
    )(x2, g, sh, sc, w1, b1.reshape(1, 2 * n))


def _confout_kernel(x_ref, g1_ref, a_ref, lg_ref, lb_ref, w_ref, b_ref, o_ref):
    a = a_ref[...]
    mu = jnp.mean(a, axis=-1, keepdims=True)
    ac = a - mu
    var = jnp.mean(ac * ac, axis=-1, keepdims=True)
    y = ac * lax.rsqrt(var + EPS) * lg_ref[...] + lb_ref[...]
    y = y * jax.nn.sigmoid(y)
    o_ref[...] = x_ref[...] + g1_ref[0] * (_dot(y.astype(BF16), w_ref[...]) + b_ref[...])


def _confout(x2, g1, a2, ln_g, ln_b, w2, b2, seq, tm=512):
    t, d = x2.shape
    n = a2.shape[1]
    tpb = seq // tm
    return pl.pallas_call(
        _confout_kernel,
        out_shape=jax.ShapeDtypeStruct((t, d), F32),
        grid=(t // tm,),
        in_specs=[pl.BlockSpec((tm, d), lambda i: (i, 0)),
                  pl.BlockSpec((1, 1, d), lambda i: (i // tpb, 0, 0)),
                  pl.BlockSpec((tm, n), lambda i: (i, 0)),
                  pl.BlockSpec((1, n), lambda i: (0, 0)),
                  pl.BlockSpec((1, n), lambda i: (0, 0)),
                  pl.BlockSpec(w2.shape, lambda i: (0, 0)),
                  pl.BlockSpec((1, d), lambda i: (0, 0))],
        out_specs=pl.BlockSpec((tm, d), lambda i: (i, 0)),
        compiler_params=_params(("parallel",)),
        name="conformer_out",
    )(x2, g1, a2, ln_g.reshape(1, n), ln_b.reshape(1, n), w2, b2.reshape(1, d))


def _router_kernel(x_ref, g_ref, sh_ref, sc_ref, wr_ref, br_ref, h_ref, route_ref, cnt_ref, run_ref):
    i = pl.program_id(0)

    @pl.when(i == 0)
    def _():
        run_ref[...] = jnp.zeros_like(run_ref)

    h = _norm_mod(x_ref[...], g_ref[...], sh_ref[0], sc_ref[0])
    hh, hl = _split(h)
    h_ref[...] = hh
    logits = _dot(hh, wr_ref[0]) + _dot(hl, wr_ref[0]) + _dot(hh, wr_ref[1]) + br_ref[...]
    tm = logits.shape[0]
    lane = lax.broadcasted_iota(jnp.int32, (tm, LANES), 1)
    ninf = jnp.float32(-jnp.inf)

    def first_argmax(v, m):
        return jnp.min(jnp.where(v == m, lane, LANES), axis=-1, keepdims=True)

    gl = jnp.where(lane < N_GROUPS, logits, ninf)
    gmax = jnp.max(gl, axis=-1, keepdims=True)
    g_w = 1.0 / jnp.sum(jnp.exp(gl - gmax), axis=-1, keepdims=True)
    g_idx = first_argmax(gl, gmax)
    e_lo = ROUTE_LANE0 + EXPERTS_PER_GROUP * g_idx
    el = jnp.where((lane >= e_lo) & (lane < e_lo + EXPERTS_PER_GROUP), logits, ninf)
    m1 = jnp.max(el, axis=-1, keepdims=True)
    esum = jnp.sum(jnp.exp(el - m1), axis=-1, keepdims=True)
    i1 = first_argmax(el, m1)
    el2 = jnp.where(lane == i1, ninf, el)
    m2 = jnp.max(el2, axis=-1, keepdims=True)
    i2 = first_argmax(el2, m2)
    p1 = 1.0 / esum
    p2 = jnp.exp(m2 - m1) / esum
    w1 = g_w * (p1 / (p1 + p2))
    w2 = g_w * (p2 / (p1 + p2))

    oh = jnp.where((lane == i1) | (lane == i2), 1.0, 0.0)
    r_i = lax.broadcasted_iota(jnp.int32, (tm, tm), 0)
    c_i = lax.broadcasted_iota(jnp.int32, (tm, tm), 1)
    tri = jnp.where(c_i < r_i, 1.0, 0.0).astype(BF16)
    before = _dot(tri, oh.astype(BF16)) + run_ref[...]
    rank1 = jnp.sum(jnp.where(lane == i1, before, 0.0), axis=-1, keepdims=True)
    rank2 = jnp.sum(jnp.where(lane == i2, before, 0.0), axis=-1, keepdims=True)
    run_ref[...] = run_ref[...] + jnp.sum(oh, axis=0, keepdims=True)
    cnt_ref[...] = run_ref[...]

    e1 = (i1 - ROUTE_LANE0).astype(F32)
    e2 = (i2 - ROUTE_LANE0).astype(F32)
    vals = (e1, e2, rank1, rank2, w1, w2)
    out = jnp.zeros((tm, LANES), F32)
    for k, v in enumerate(vals):
        out = jnp.where(lane == k, v, out)
    route_ref[...] = out


def _router(x2, g, sh, sc, wr, br, seq, tm=512):
    t, d = x2.shape
    tpb = seq // tm
    return pl.pallas_call(
        _router_kernel,
        out_shape=(jax.ShapeDtypeStruct((t, d), BF16), jax.ShapeDtypeStruct((t, LANES), F32),
                   jax.ShapeDtypeStruct((1, LANES), F32)),
        grid=(t // tm,),
        in_specs=[pl.BlockSpec((tm, d), lambda i: (i, 0)),
                  pl.BlockSpec((1, d), lambda i: (0, 0)),
                  pl.BlockSpec((1, 1, d), lambda i: (i // tpb, 0, 0)),
                  pl.BlockSpec((1, 1, d), lambda i: (i // tpb, 0, 0)),
                  pl.BlockSpec(wr.shape, lambda i: (0, 0, 0)),
                  pl.BlockSpec((1, LANES), lambda i: (0, 0))],
        out_specs=(pl.BlockSpec((tm, d), lambda i: (i, 0)), pl.BlockSpec((tm, LANES), lambda i: (i, 0)),
                   pl.BlockSpec((1, LANES), lambda i: (0, 0))),
        scratch_shapes=[pltpu.VMEM((1, LANES), F32)],
        compiler_params=_params(("arbitrary",)),
        name="moe_router",
    )(x2, g, sh, sc, wr, br)


def _expert_kernel(te_ref, nv_ref, x_ref, wg_ref, wu_ref, wd_ref, o_ref):
    @pl.when(pl.program_id(0) < nv_ref[0])
    def _():
        x = x_ref[...]
        a = _dot(x, wg_ref[0])
        u = _dot(x, wu_ref[0])
        he = (a * jax.nn.sigmoid(a)) * u
        o_ref[...] = _dot(he.astype(BF16), wd_ref[0])


def _experts(xs, tile_expert, n_valid, wg, wu, wd):
    r, d = xs.shape
    de = wg.shape[2]
    nt = r // MOE_TILE
    row = lambda i, te, nv: (jnp.minimum(i, nv[0] - 1), 0)
    return pl.pallas_call(
        _expert_kernel,
        out_shape=jax.ShapeDtypeStruct((r, d), F32),
        grid_spec=pltpu.PrefetchScalarGridSpec(
            num_scalar_prefetch=2,
            grid=(nt,),
            in_specs=[pl.BlockSpec((MOE_TILE, d), row),
                      pl.BlockSpec((1, d, de), lambda i, te, nv: (te[i], 0, 0)),
                      pl.BlockSpec((1, d, de), lambda i, te, nv: (te[i], 0, 0)),
                      pl.BlockSpec((1, de, d), lambda i, te, nv: (te[i], 0, 0))],
            out_specs=pl.BlockSpec((MOE_TILE, d), row)),
        compiler_params=_params(("arbitrary",)),
        name="moe_experts",
    )(tile_expert, n_valid, xs, wg, wu, wd)


def _combine_kernel(x_ref, g2_ref, route_ref, y1_ref, y2_ref, fg_ref, o_ref, *, final):
    r = route_ref[...]
    w1 = r[:, 4:5]
    w2 = r[:, 5:6]
    x = x_ref[...] + g2_ref[0] * (w1 * y1_ref[...] + w2 * y2_ref[...])
    if final:
        x = (x * lax.rsqrt(jnp.mean(x * x, axis=-1, keepdims=True) + EPS)) * fg_ref[...]
    o_ref[...] = x


def _combine(x2, g2, route, y1, y2, final_g, seq, final, tm=512):
    t, d = x2.shape
    tpb = seq // tm
    kern = functools.partial(_combine_kernel, final=final)
    return pl.pallas_call(
        kern,
        out_shape=jax.ShapeDtypeStruct((t, d), F32),
        grid=(t // tm,),
        in_specs=[pl.BlockSpec((tm, d), lambda i: (i, 0)),
                  pl.BlockSpec((1, 1, d), lambda i: (i // tpb, 0, 0)),
                  pl.BlockSpec((tm, LANES), lambda i: (i, 0)),
                  pl.BlockSpec((tm, d), lambda i: (i, 0)),
                  pl.BlockSpec((tm, d), lambda i: (i, 0)),
                  pl.BlockSpec((1, d), lambda i: (0, 0))],
        out_specs=pl.BlockSpec((tm, d), lambda i: (i, 0)),
        compiler_params=_params(("parallel",)),
        name="moe_combine",
    )(x2, g2, route, y1, y2, final_g)


def _moe(x2, g, sh, sc, g2, wg_r, bg_r, we_r, be_r, w_gate, w_up, w_down, final_g, seq, final):
    t, d = x2.shape
    wr = jnp.zeros((d, LANES), F32).at[:, :N_GROUPS].set(wg_r).at[:, ROUTE_LANE0:ROUTE_LANE0 + N_EXPERTS].set(we_r)
    wr_hi = wr.astype(BF16)
    wr_lo = (wr - wr_hi.astype(F32)).astype(BF16)
    br = jnp.zeros((1, LANES), F32).at[0, :N_GROUPS].set(bg_r).at[0, ROUTE_LANE0:ROUTE_LANE0 + N_EXPERTS].set(be_r)
    h, route, cnt = _router(x2, g, sh, sc, jnp.stack([wr_hi, wr_lo]), br, seq)

    counts = cnt[0, ROUTE_LANE0:ROUTE_LANE0 + N_EXPERTS].astype(jnp.int32)
    tiles = (counts + MOE_TILE - 1) // MOE_TILE
    tile_end = jnp.cumsum(tiles)
    offs = (tile_end - tiles) * MOE_TILE
    nt = (2 * t) // MOE_TILE + N_EXPERTS
    tile_expert = jnp.minimum(jnp.searchsorted(tile_end, jnp.arange(nt, dtype=jnp.int32), side="right"),
                              N_EXPERTS - 1).astype(jnp.int32)
    n_valid = tile_end[-1:].astype(jnp.int32)
    e12 = route[:, 0:2].astype(jnp.int32)
    pos = offs[e12] + route[:, 2:4].astype(jnp.int32)
    tok = jnp.broadcast_to(jnp.arange(t, dtype=jnp.int32)[:, None], (t, 2))
    sorted_tok = jnp.zeros((nt * MOE_TILE,), jnp.int32).at[pos.reshape(-1)].set(tok.reshape(-1))
    xs = jnp.take(h, sorted_tok, axis=0)
    ys = _experts(xs, tile_expert, n_valid, w_gate.astype(BF16), w_up.astype(BF16), w_down.astype(BF16))
    y1 = jnp.take(ys, pos[:, 0], axis=0)
    y2 = jnp.take(ys, pos[:, 1], axis=0)
    return _combine(x2, g2, route, y1, y2, final_g, seq, final)


def _rope_tables(seq, width):
    hd = DA_HEAD_DIM
    half = hd // 2
    quarter = half // 2
    pos = jnp.arange(seq)
    row = (pos // GRID_W).astype(F32)
    col = (pos % GRID_W).astype(F32)
    inv = ROPE_THETA ** (-jnp.arange(0, half, 2, dtype=F32) / half)
    i = jnp.arange(hd)
    p = jnp.where((i < half)[None, :], row[:, None], col[:, None])
    ang = p * inv[i % quarter][None, :]
    sign = jnp.where((i % half) < quarter, -1.0, 1.0)[None, :]
    reps = width // hd
    return jnp.tile(jnp.cos(ang), (1, reps)), jnp.tile(jnp.sin(ang) * sign, (1, reps))


def _rope_partner(width):
    i = jnp.arange(width)
    quarter = DA_HEAD_DIM // 4
    return jnp.where((i % (2 * quarter)) < quarter, i + quarter, i - quarter)


def kernel(x, c, ctx, c_ctx, ada_w, ada_b, norm1_g, norm2_g, final_g, w_in0, w_out0, lam_q1, lam_k1, lam_q2, lam_k2, subln_g, hy_short_w, hy_short_b, hy_w1, hy_b1, hy_fr1, hy_w2, hy_b2, hy_fr2, hy_w3, hy_b3, hy_bias, cv_w1, cv_b1, cv_dw_w, cv_dw_b, cv_ln_g, cv_ln_b, cv_w2, cv_b2, moe_wg, moe_bg, moe_we, moe_be, moe_w_gate, moe_w_up, moe_w_down):
    bsz, seq, d = x.shape
    lctx = ctx.shape[1]
    depth = ada_w.shape[0]
    t = bsz * seq
    hyw = d - DA_WIDTH
    x2 = x.reshape(t, d)

    rows = ((bsz + 1 + SUBLANES - 1) // SUBLANES) * SUBLANES
    cs = jnp.zeros((rows, d), F32).at[:bsz].set(c).at[bsz].set(c_ctx)
    mods = _ada(cs, ada_w, ada_b)

    def tok_mod(i, k):
        return mods[i, :bsz, k * d:(k + 1) * d].reshape(bsz, 1, d)

    def ctx_mod(i, k):
        return mods[i, bsz:bsz + 1, k * d:(k + 1) * d].reshape(1, 1, d)

    for i in range(depth):
        j = i // 2
        g_n1 = norm1_g[i].reshape(1, d)
        if i % 2 == 0:
            assert not any(m % 2 == 0 for m in range(i + 1, depth)), "context-stream update is not implemented"
            lam_init = 0.8 - 0.6 * math.exp(-0.3 * i)
            w_in = w_in0[j]
            part = _rope_partner(2 * DA_WIDTH)
            w_ext = jnp.concatenate([w_in, w_in[:, :2 * DA_WIDTH][:, part]], axis=1).astype(BF16)
            cw = 2 * LANES
            cos, sin = _rope_tables(seq, cw)
            q, k, v, u = _inproj(x2, g_n1, tok_mod(i, 0), tok_mod(i, 1), cos, sin, w_ext, seq)
            kc, vc = _ctxproj(ctx.reshape(bsz * lctx, d), g_n1, ctx_mod(i, 0), ctx_mod(i, 1),
                              w_in[:, DA_WIDTH:3 * DA_WIDTH].astype(BF16), lctx)
            k_all = jnp.concatenate([kc.reshape(bsz, lctx, DA_WIDTH), k.reshape(bsz, seq, DA_WIDTH)], axis=1)
            v_all = jnp.concatenate([vc.reshape(bsz, lctx, DA_WIDTH), v.reshape(bsz, seq, DA_WIDTH)], axis=1)
            lam_p = jnp.stack([lam_q1[j], lam_k1[j], lam_q2[j], lam_k2[j]])
            o_a = _attention(q, jnp.swapaxes(k_all, 1, 2), v_all, lam_p, subln_g[j].reshape(1, -1), seq, lam_init)

            uc = _dwconv(u.reshape(bsz, seq, 3 * hyw), hy_short_w[j], hy_short_b[j])
            nb = seq // LC_P
            ucm = uc.reshape(bsz, nb, LC_P, 3 * hyw).transpose(3, 1, 0, 2).reshape(3 * hyw, nb * bsz, LC_P)
            kc_ext = _hyena_filters(seq, hy_w1[j], hy_b1[j], hy_fr1[j], hy_w2[j], hy_b2[j], hy_fr2[j],
                                    hy_w3[j], hy_b3[j])
            bias_cm = jnp.broadcast_to(hy_bias[j].reshape(HY_ORDER * hyw, 1, 1), (HY_ORDER * hyw, 1, LC_P))
            z1 = _longconv(ucm[:hyw], ucm[hyw:2 * hyw], kc_ext, bias_cm, 0, nb, bsz)
            z2 = _longconv(z1, ucm[2 * hyw:], kc_ext, bias_cm, 1, nb, bsz)
            o_b = z2.reshape(hyw, nb, bsz, LC_P).transpose(2, 1, 3, 0).reshape(t, hyw)
            x2 = _outproj(x2, tok_mod(i, 2), o_a, o_b, w_out0[j].astype(BF16), seq)
        else:
            a = _confin(x2, g_n1, tok_mod(i, 0), tok_mod(i, 1), cv_w1[j].astype(BF16), cv_b1[j], seq)
            a = _dwconv(a.reshape(bsz, seq, -1), cv_dw_w[j], cv_dw_b[j]).reshape(t, -1)
            x2 = _confout(x2, tok_mod(i, 2), a, cv_ln_g[j], cv_ln_b[j], cv_w2[j].astype(BF16), cv_b2[j], seq)
        x2 = _moe(x2, norm2_g[i].reshape(1, d), tok_mod(i, 3), tok_mod(i, 4), tok_mod(i, 5),
                  moe_wg[i], moe_bg[i], moe_we[i], moe_be[i], moe_w_gate[i], moe_w_up[i], moe_w_down[i],
                  final_g.reshape(1, d), seq, final=(i == depth - 1))
    return x2.reshape(bsz, seq, d)
```

```python
import functools
import math

import jax
import jax.numpy as jnp
from jax import lax
from jax.experimental import pallas as pl
from jax.experimental.pallas import tpu as pltpu

F32 = jnp.float32
BF16 = jnp.bfloat16

GRID_W = 64
DA_HEADS = 4
DA_HEAD_DIM = 64
DA_WIDTH = DA_HEADS * 2 * DA_HEAD_DIM
HY_ORDER = 2
HY_TARGET = 1e-2
HY_MIN_DECAY = math.log(HY_TARGET) / 0.3
HY_MAX_DECAY = math.log(HY_TARGET) / 1.5
N_GROUPS = 4
EXPERTS_PER_GROUP = 8
N_EXPERTS = N_GROUPS * EXPERTS_PER_GROUP
ROPE_THETA = 10000.0
EPS = 1e-6

LANES = 128
SUBLANES = 8
VMEM_LIMIT = 52 * 1024 * 1024
ROUTE_LANE0 = N_GROUPS
MOE_TILE = 256
LC_P = 128
LC_W = 1024


def _params(sem):
    return pltpu.CompilerParams(dimension_semantics=sem, vmem_limit_bytes=VMEM_LIMIT)


def _split(a):
    hi = a.astype(BF16)
    lo = (a - hi.astype(F32)).astype(BF16)
    return hi, lo


def _dot(a, b):
    return jnp.dot(a, b, preferred_element_type=F32)


def _dot3(a, b):
    ah, al = _split(a)
    bh, bl = _split(b)
    return _dot(ah, bh) + _dot(al, bh) + _dot(ah, bl)


def _norm_mod(x, g, sh, sc):
    y = x * lax.rsqrt(jnp.mean(x * x, axis=-1, keepdims=True) + EPS)
    return (y * g) * (1.0 + sc) + sh


def _ada_kernel(c_ref, w_ref, b_ref, o_ref):
    c = c_ref[...]
    s = c * jax.nn.sigmoid(c)
    o_ref[0] = _dot3(s, w_ref[0]) + b_ref[0]


def _ada(cs, ada_w, ada_b):
    depth, d, n6 = ada_w.shape
    rows = cs.shape[0]
    tn = 1536
    return pl.pallas_call(
        _ada_kernel,
        out_shape=jax.ShapeDtypeStruct((depth, rows, n6), F32),
        grid=(depth, n6 // tn),
        in_specs=[pl.BlockSpec((rows, d), lambda l, j: (0, 0)),
                  pl.BlockSpec((1, d, tn), lambda l, j: (l, 0, j)),
                  pl.BlockSpec((1, 1, tn), lambda l, j: (l, 0, j))],
        out_specs=pl.BlockSpec((1, rows, tn), lambda l, j: (l, 0, j)),
        compiler_params=_params(("parallel", "parallel")),
        name="adaln",
    )(cs, ada_w, ada_b.reshape(depth, 1, n6))


def _inproj_kernel(x_ref, g_ref, sh_ref, sc_ref, cos_ref, sin_ref, w_ref,
                   q_ref, k_ref, v_ref, u_ref, *, d_attn, d_hy):
    h = _norm_mod(x_ref[...], g_ref[...], sh_ref[0], sc_ref[0]).astype(BF16)
    cos = cos_ref[...]
    sin = sin_ref[...]
    rot0 = 3 * d_attn + d_hy
    cw = cos.shape[1]

    def mm(c0, c1):
        return _dot(h, w_ref[:, c0:c1])

    for j in range(d_attn // cw):
        qa = mm(j * cw, (j + 1) * cw)
        qr = mm(rot0 + j * cw, rot0 + (j + 1) * cw)
        q_ref[:, j * cw:(j + 1) * cw] = ((qa * cos + qr * sin) * (DA_HEAD_DIM ** -0.5)).astype(BF16)
        ka = mm(d_attn + j * cw, d_attn + (j + 1) * cw)
        kr = mm(rot0 + d_attn + j * cw, rot0 + d_attn + (j + 1) * cw)
        k_ref[:, j * cw:(j + 1) * cw] = (ka * cos + kr * sin).astype(BF16)
    v_ref[...] = mm(2 * d_attn, 3 * d_attn).astype(BF16)
    for j in range(d_hy // 512):
        u_ref[:, j * 512:(j + 1) * 512] = mm(3 * d_attn + j * 512, 3 * d_attn + (j + 1) * 512)


def _inproj(x2, g, sh, sc, cos, sin, w_ext, seq, tm=512):
    t, d = x2.shape
    tpb = seq // tm
    d_attn = DA_WIDTH
    d_hy = w_ext.shape[1] - 5 * d_attn
    cw = cos.shape[1]
    kern = functools.partial(_inproj_kernel, d_attn=d_attn, d_hy=d_hy)
    return pl.pallas_call(
        kern,
        out_shape=(jax.ShapeDtypeStruct((t, d_attn), BF16), jax.ShapeDtypeStruct((t, d_attn), BF16),
                   jax.ShapeDtypeStruct((t, d_attn), BF16), jax.ShapeDtypeStruct((t, d_hy), F32)),
        grid=(t // tm,),
        in_specs=[pl.BlockSpec((tm, d), lambda i: (i, 0)),
                  pl.BlockSpec((1, d), lambda i: (0, 0)),
                  pl.BlockSpec((1, 1, d), lambda i: (i // tpb, 0, 0)),
                  pl.BlockSpec((1, 1, d), lambda i: (i // tpb, 0, 0)),
                  pl.BlockSpec((tm, cw), lambda i: (i % tpb, 0)),
                  pl.BlockSpec((tm, cw), lambda i: (i % tpb, 0)),
                  pl.BlockSpec(w_ext.shape, lambda i: (0, 0))],
        out_specs=(pl.BlockSpec((tm, d_attn), lambda i: (i, 0)), pl.BlockSpec((tm, d_attn), lambda i: (i, 0)),
                   pl.BlockSpec((tm, d_attn), lambda i: (i, 0)), pl.BlockSpec((tm, d_hy), lambda i: (i, 0))),
        compiler_params=_params(("parallel",)),
        name="inproj0",
    )(x2, g, sh, sc, cos, sin, w_ext)


def _ctxproj_kernel(x_ref, g_ref, sh_ref, sc_ref, w_ref, k_ref, v_ref):
    h = _norm_mod(x_ref[...], g_ref[...], sh_ref[0], sc_ref[0]).astype(BF16)
    n = k_ref.shape[1]
    k_ref[...] = _dot(h, w_ref[:, :n]).astype(BF16)
    v_ref[...] = _dot(h, w_ref[:, n:]).astype(BF16)


def _ctxproj(c2, g, sh, sc, w_kv, tm):
    t, d = c2.shape
    n = w_kv.shape[1] // 2
    return pl.pallas_call(
        _ctxproj_kernel,
        out_shape=(jax.ShapeDtypeStruct((t, n), BF16), jax.ShapeDtypeStruct((t, n), BF16)),
        grid=(t // tm,),
        in_specs=[pl.BlockSpec((tm, d), lambda i: (i, 0)),
                  pl.BlockSpec((1, d), lambda i: (0, 0)),
                  pl.BlockSpec((1, 1, d), lambda i: (0, 0, 0)),
                  pl.BlockSpec((1, 1, d), lambda i: (0, 0, 0)),
                  pl.BlockSpec(w_kv.shape, lambda i: (0, 0))],
        out_specs=(pl.BlockSpec((tm, n), lambda i: (i, 0)), pl.BlockSpec((tm, n), lambda i: (i, 0))),
        compiler_params=_params(("parallel",)),
        name="ctxproj",
    )(c2, g, sh, sc, w_kv)


def _attn_kernel(lam_ref, g_ref, q_ref, kc_ref, vc_ref, k_ref, v_ref, o_ref, *, lam_init):
    lp = lam_ref[...]
    lam = (jnp.exp(jnp.sum(lp[0:1] * lp[1:2], axis=-1, keepdims=True))
           - jnp.exp(jnp.sum(lp[2:3] * lp[3:4], axis=-1, keepdims=True)) + lam_init)
    q = q_ref[...]
    lane = lax.broadcasted_iota(jnp.int32, q.shape, 1)
    zero = jnp.zeros_like(q)
    nt = (((1,), (1,)), ((), ()))
    kc = kc_ref[...]
    kx = k_ref[...]

    def head(qh):
        sc = lax.dot_general(qh, kc, nt, preferred_element_type=F32)
        sx = lax.dot_general(qh, kx, nt, preferred_element_type=F32)
        m = jnp.maximum(jnp.max(sc, axis=-1, keepdims=True), jnp.max(sx, axis=-1, keepdims=True))
        ec = jnp.exp(sc - m)
        ex = jnp.exp(sx - m)
        return ec, ex, jnp.sum(ec, axis=-1, keepdims=True) + jnp.sum(ex, axis=-1, keepdims=True)

    ec1, ex1, l1 = head(jnp.where(lane < DA_HEAD_DIM, q, zero))
    ec2, ex2, l2 = head(jnp.where(lane >= DA_HEAD_DIM, q, zero))
    r1 = 1.0 / l1
    r2 = lam / l2
    o = (_dot((ec1 * r1 - ec2 * r2).astype(BF16), vc_ref[...])
         + _dot((ex1 * r1 - ex2 * r2).astype(BF16), v_ref[...]))
    y = o * lax.rsqrt(jnp.mean(o * o, axis=-1, keepdims=True) + EPS)
    o_ref[...] = (y * g_ref[...]) * (1.0 - lam_init)


def _attention(q, kc, vc, k, v, lam_p, subln_g, seq, lctx, lam_init, tq=256):
    t = q.shape[0]
    b = t // seq
    hw = 2 * DA_HEAD_DIM
    nq = seq // tq
    kern = functools.partial(_attn_kernel, lam_init=lam_init)
    return pl.pallas_call(
        kern,
        out_shape=jax.ShapeDtypeStruct((t, DA_WIDTH), F32),
        grid=(b, DA_HEADS, nq),
        in_specs=[pl.BlockSpec(lam_p.shape, lambda bi, h, i: (0, 0)),
                  pl.BlockSpec((1, hw), lambda bi, h, i: (0, 0)),
                  pl.BlockSpec((tq, hw), lambda bi, h, i: (bi * nq + i, h)),
                  pl.BlockSpec((lctx, hw), lambda bi, h, i: (bi, h)),
                  pl.BlockSpec((lctx, hw), lambda bi, h, i: (bi, h)),
                  pl.BlockSpec((seq, hw), lambda bi, h, i: (bi, h)),
                  pl.BlockSpec((seq, hw), lambda bi, h, i: (bi, h))],
        out_specs=pl.BlockSpec((tq, hw), lambda bi, h, i: (bi * nq + i, h)),
        compiler_params=_params(("parallel", "parallel", "parallel")),
        name="diff_attn",
    )(lam_p, subln_g, q, kc, vc, k, v)


def _dwconv_kernel(x_ref, w_ref, b_ref, o_ref, pad_ref, *, kw, halo, channel_major):
    s = x_ref.shape[1]
    pl_ = (kw - 1) // 2
    zeros = jnp.zeros((halo, x_ref.shape[2]), F32)
    pad_ref[0:halo, :] = zeros
    pad_ref[halo + s:halo + s + halo, :] = zeros
    pad_ref[halo:halo + s, :] = x_ref[0]
    w = w_ref[...]
    acc = jnp.zeros((s, x_ref.shape[2]), F32) + b_ref[...]
    for j in range(kw):
        off = halo - pl_ + j
        acc = acc + pad_ref[off:off + s, :] * w[j:j + 1, :]
    if channel_major:
        o_ref[...] = acc.T
    else:
        o_ref[0] = acc


def _dwconv(x3, w, bias, channel_major=False):
    b, s, c = x3.shape
    kw = w.shape[0]
    halo = 2 * SUBLANES
    kern = functools.partial(_dwconv_kernel, kw=kw, halo=halo, channel_major=channel_major)
    if channel_major:
        out_shape = jax.ShapeDtypeStruct((c, b * s), F32)
        out_spec = pl.BlockSpec((LANES, s), lambda bi, ci: (ci, bi))
    else:
        out_shape = jax.ShapeDtypeStruct((b, s, c), F32)
        out_spec = pl.BlockSpec((1, s, LANES), lambda bi, ci: (bi, 0, ci))
    return pl.pallas_call(
        kern,
        out_shape=out_shape,
        grid=(b, c // LANES),
        in_specs=[pl.BlockSpec((1, s, LANES), lambda bi, ci: (bi, 0, ci)),
                  pl.BlockSpec((kw, LANES), lambda bi, ci: (0, ci)),
                  pl.BlockSpec((1, LANES), lambda bi, ci: (0, ci))],
        out_specs=out_spec,
        scratch_shapes=[pltpu.VMEM((s + 2 * halo, LANES), F32)],
        compiler_params=_params(("parallel", "parallel")),
        name="dwconv",
    )(x3, w, bias.reshape(1, c))


def _filter_kernel(zt_ref, t_ref, w1_ref, b1_ref, f1_ref, w2_ref, b2_ref, f2_ref, w3_ref, b3_ref, dl_ref,
                   o_ref, h_ref, *, n):
    @pl.when(pl.program_id(0) == 0)
    def _():
        h1 = jnp.sin(f1_ref[...] * (_dot3(w1_ref[...], zt_ref[...]) + b1_ref[...]))
        h_ref[...] = jnp.sin(f2_ref[...] * (_dot3(w2_ref[...], h1) + b2_ref[...]))

    h2 = h_ref[...]
    rows = o_ref.shape[0]
    fwd = _dot3(w3_ref[0], h2[:, :n]) + b3_ref[0]
    bwd = _dot3(w3_ref[1], h2[:, n:]) + b3_ref[1]
    decay = jnp.exp(-t_ref[...] * dl_ref[...])
    lane = lax.broadcasted_iota(jnp.int32, (rows, n), 1)
    kf = fwd * decay[:, :n]
    kb = jnp.where(lane == 0, 0.0, bwd * decay[:, n:])
    inv = 1.0 / (jnp.sum(jnp.abs(kf), axis=-1, keepdims=True) + jnp.sum(jnp.abs(kb), axis=-1, keepdims=True))
    kf = kf * inv
    kb = kb * inv
    o_ref[:, 0:LANES] = kb[:, n - LANES:]
    o_ref[:, LANES:LANES + n] = kf
    o_ref[:, LANES + n:] = kb


def _hyena_filters(n, w1, b1, fr1, w2, b2, fr2, w3, b3):
    emb, ffn = w1.shape
    c = w3.shape[1] // (2 * HY_ORDER)
    bands = (emb - 1) // 2
    t = jnp.linspace(0.0, 1.0, n, dtype=F32)[:, None]
    ang = (2.0 * math.pi / n) * jnp.arange(n, dtype=F32)[:, None] * jnp.linspace(1e-4, bands - 1, bands, dtype=F32)[None, :]
    z = jnp.concatenate([t, jnp.cos(ang), -jnp.sin(ang)], axis=-1)
    rev = (n - jnp.arange(n)) % n
    z2 = jnp.concatenate([z, z[rev]], axis=0)
    t2 = jnp.concatenate([t, t[rev]], axis=0).reshape(1, 2 * n)
    emb_p = ((emb + SUBLANES - 1) // SUBLANES) * SUBLANES
    zt = jnp.zeros((emb_p, 2 * n), F32).at[:emb].set(z2.T)
    w1t = jnp.zeros((ffn, emb_p), F32).at[:, :emb].set(w1.T)
    deltas = jnp.abs(jnp.linspace(HY_MIN_DECAY, HY_MAX_DECAY, c, dtype=F32))
    w3t = w3.T.reshape(HY_ORDER, 2, c, ffn).transpose(1, 0, 2, 3).reshape(2, HY_ORDER * c, ffn)
    b3t = b3.reshape(HY_ORDER, 2, c).transpose(1, 0, 2).reshape(2, HY_ORDER * c, 1)
    dl = jnp.tile(deltas, HY_ORDER).reshape(HY_ORDER * c, 1)
    rows = LANES
    kern = functools.partial(_filter_kernel, n=n)
    col = lambda v: v.reshape(ffn, 1)
    return pl.pallas_call(
        kern,
        out_shape=jax.ShapeDtypeStruct((HY_ORDER * c, LANES + 2 * n), F32),
        grid=(HY_ORDER * c // rows,),
        in_specs=[pl.BlockSpec(zt.shape, lambda i: (0, 0)),
                  pl.BlockSpec(t2.shape, lambda i: (0, 0)),
                  pl.BlockSpec(w1t.shape, lambda i: (0, 0)),
                  pl.BlockSpec((ffn, 1), lambda i: (0, 0)),
                  pl.BlockSpec((ffn, 1), lambda i: (0, 0)),
                  pl.BlockSpec((ffn, ffn), lambda i: (0, 0)),
                  pl.BlockSpec((ffn, 1), lambda i: (0, 0)),
                  pl.BlockSpec((ffn, 1), lambda i: (0, 0)),
                  pl.BlockSpec((2, rows, ffn), lambda i: (0, i, 0)),
                  pl.BlockSpec((2, rows, 1), lambda i: (0, i, 0)),
                  pl.BlockSpec((rows, 1), lambda i: (i, 0))],
        out_specs=pl.BlockSpec((rows, LANES + 2 * n), lambda i: (i, 0)),
        scratch_shapes=[pltpu.VMEM((ffn, 2 * n), F32)],
        compiler_params=_params(("arbitrary",)),
        name="hyena_filters",
    )(zt, t2, w1t, col(b1), col(fr1), w2.T, col(b2), col(fr2), w3t, b3t, dl)


def _longconv_kernel(z_ref, gate_ref, kc_ref, bias_ref, o_ref, r_ref, zs_ref, acc_ref, *, nb, bsz, cb):
    p = LC_P
    n2 = 2 * nb * p

    def channel(c, carry):
        w = min(LC_W, n2)
        for ci in range(n2 // w):
            win = kc_ref[pl.ds(c, 1), ci * w:ci * w + w + LANES]
            rolled = pltpu.roll(jnp.broadcast_to(win, (p, w + LANES)), 0, 1, stride=1, stride_axis=0)
            r_ref[:, ci * w:(ci + 1) * w] = rolled[:, LANES:].astype(BF16)
        for s1 in range(nb):
            zs_ref[s1 * bsz:(s1 + 1) * bsz, :] = z_ref[c, pl.ds(s1, bsz, stride=nb), :]
        acc_ref[...] = jnp.zeros_like(acc_ref)
        for pi in range(nb):
            d = -nb + 2 * pi
            off = (d % (2 * nb)) * p
            lo = max(0, -d - 1)
            hi = min(nb, nb - d)
            lhs = zs_ref[lo * bsz:hi * bsz, :].astype(BF16)
            out = _dot(lhs, r_ref[:, off:off + 2 * p])
            for k in range(2):
                dk = d + k
                a0, a1 = max(0, -dk), min(nb, nb - dk)
                if a1 <= a0:
                    continue
                acc_ref[(a0 + dk) * bsz:(a1 + dk) * bsz, :] += out[(a0 - lo) * bsz:(a1 - lo) * bsz, k * p:(k + 1) * p]
        bias = bias_ref[c]
        for s1 in range(nb):
            rows = slice(s1 * bsz, (s1 + 1) * bsz)
            o_ref[c, pl.ds(s1, bsz, stride=nb), :] = (gate_ref[c, pl.ds(s1, bsz, stride=nb), :]
                                                       * (acc_ref[rows, :] + zs_ref[rows, :] * bias))
        return carry

    lax.fori_loop(0, cb, channel, 0)


def _longconv(z_cm, z_blk0, gate_cm, gate_blk0, kc_ext, bias_cm, order, c, nb, bsz, cb=8):
    rows, p = z_cm.shape[1:]
    kern = functools.partial(_longconv_kernel, nb=nb, bsz=bsz, cb=cb)
    cblocks = c // cb
    return pl.pallas_call(
        kern,
        out_shape=jax.ShapeDtypeStruct((c, rows, p), F32),
        grid=(cblocks,),
        in_specs=[pl.BlockSpec((cb, rows, p), lambda i: (z_blk0 + i, 0, 0)),
                  pl.BlockSpec((cb, rows, p), lambda i: (gate_blk0 + i, 0, 0)),
                  pl.BlockSpec((cb, kc_ext.shape[1]), lambda i: (order * cblocks + i, 0)),
                  pl.BlockSpec((cb, 1, p), lambda i: (order * cblocks + i, 0, 0))],
        out_specs=pl.BlockSpec((cb, rows, p), lambda i: (i, 0, 0)),
        scratch_shapes=[pltpu.VMEM((p, 2 * nb * p), BF16), pltpu.VMEM((rows, p), F32), pltpu.VMEM((rows, p), F32)],
        compiler_params=_params(("parallel",)),
        name="hyena_longconv",
    )(z_cm, gate_cm, kc_ext, bias_cm)


def _outproj_kernel(x_ref, g1_ref, a_ref, b_ref, w_ref, o_ref):
    da = a_ref.shape[1]
    y = _dot(a_ref[...].astype(BF16), w_ref[:da, :]) + _dot(b_ref[...].T.astype(BF16), w_ref[da:, :])
    o_ref[...] = x_ref[...] + g1_ref[0] * y


def _outproj(x2, g1, oa, ob, w, seq, tm=512):
    t, d = x2.shape
    tpb = seq // tm
    return pl.pallas_call(
        _outproj_kernel,
        out_shape=jax.ShapeDtypeStruct((t, d), F32),
        grid=(t // tm,),
        in_specs=[pl.BlockSpec((tm, d), lambda i: (i, 0)),
                  pl.BlockSpec((1, 1, d), lambda i: (i // tpb, 0, 0)),
                  pl.BlockSpec((tm, oa.shape[1]), lambda i: (i, 0)),
                  pl.BlockSpec((ob.shape[0], tm), lambda i: (0, i)),
                  pl.BlockSpec(w.shape, lambda i: (0, 0))],
        out_specs=pl.BlockSpec((tm, d), lambda i: (i, 0)),
        compiler_params=_params(("parallel",)),
        name="outproj0",
    )(x2, g1, oa, ob, w)


def _confin_kernel(x_ref, g_ref, sh_ref, sc_ref, w_ref, b_ref, o_ref):
    h = _norm_mod(x_ref[...], g_ref[...], sh_ref[0], sc_ref[0]).astype(BF16)
    n = o_ref.shape[1]
    a = _dot(h, w_ref[:, :n]) + b_ref[:, :n]
    gt = _dot(h, w_ref[:, n:]) + b_ref[:, n:]
    o_ref[...] = a * jax.nn.sigmoid(gt)


def _confin(x2, g, sh, sc, w1, b1, seq, tm=512):
    t, d = x2.shape
    n = w1.shape[1] // 2
    tpb = seq // tm
    return pl.pallas_call(
        _confin_kernel,
        out_shape=jax.ShapeDtypeStruct((t, n), F32),
        grid=(t // tm,),
        in_specs=[pl.BlockSpec((tm, d), lambda i: (i, 0)),
                  pl.BlockSpec((1, d), lambda i: (0, 0)),
                  pl.BlockSpec((1, 1, d), lambda i: (i // tpb, 0, 0)),
                  pl.BlockSpec((1, 1, d), lambda i: (i // tpb, 0, 0)),
                  pl.BlockSpec(w1.shape, lambda i: (0, 0)),
                  pl.BlockSpec((1, 2 * n), lambda i: (0, 0))],
        out_specs=pl.BlockSpec((tm, n), lambda i: (i, 0)),
        compiler_params=_params(("parallel",)),
        name="conformer_in",
    )(x2, g, sh, sc, w1, b1.reshape(1, 2 * n))


def _confout_kernel(x_ref, g1_ref, a_ref, lg_ref, lb_ref, w_ref, b_ref, o_ref):
    a = a_ref[...]
    mu = jnp.mean(a, axis=-1, keepdims=True)
    ac = a - mu
    var = jnp.mean(ac * ac, axis=-1, keepdims=True)
    y = ac * lax.rsqrt(var + EPS) * lg_ref[...] + lb_ref[...]
    y = y * jax.nn.sigmoid(y)
    o_ref[...] = x_ref[...] + g1_ref[0] * (_dot(y.astype(BF16), w_ref[...]) + b_ref[...])


def _confout(x2, g1, a2, ln_g, ln_b, w2, b2, seq, tm=512):
    t, d = x2.shape
    n = a2.shape[1]
    tpb = seq // tm
    return pl.pallas_call(
        _confout_kernel,
        out_shape=jax.ShapeDtypeStruct((t, d), F32),
        grid=(t // tm,),
        in_specs=[pl.BlockSpec((tm, d), lambda i: (i, 0)),
                  pl.BlockSpec((1, 1, d), lambda i: (i // tpb, 0, 0)),
                  pl.BlockSpec((tm, n), lambda i: (i, 0)),
                  pl.BlockSpec((1, n), lambda i: (0, 0)),
                  pl.BlockSpec((1, n), lambda i: (0, 0)),
                  pl.BlockSpec(w2.shape, lambda i: (0, 0)),
                  pl.BlockSpec((1, d), lambda i: (0, 0))],
        out_specs=pl.BlockSpec((tm, d), lambda i: (i, 0)),
        compiler_params=_params(("parallel",)),
        name="conformer_out",
    )(x2, g1, a2, ln_g.reshape(1, n), ln_b.reshape(1, n), w2, b2.reshape(1, d))


def _router_kernel(x_ref, g_ref, sh_ref, sc_ref, wr_ref, br_ref, h_ref, route_ref, cnt_ref, run_ref):
    i = pl.program_id(0)

    @pl.when(i == 0)
    def _():
        run_ref[...] = jnp.zeros_like(run_ref)

    h = _norm_mod(x_ref[...], g_ref[...], sh_ref[0], sc_ref[0])
    hh, hl = _split(h)
    h_ref[...] = hh
    logits = _dot(hh, wr_ref[0]) + _dot(hl, wr_ref[0]) + _dot(hh, wr_ref[1]) + br_ref[...]
    tm = logits.shape[0]
    lane = lax.broadcasted_iota(jnp.int32, (tm, LANES), 1)
    ninf = jnp.float32(-jnp.inf)

    def first_argmax(v, m):
        return jnp.min(jnp.where(v == m, lane, LANES), axis=-1, keepdims=True)

    gl = jnp.where(lane < N_GROUPS, logits, ninf)
    gmax = jnp.max(gl, axis=-1, keepdims=True)
    g_w = 1.0 / jnp.sum(jnp.exp(gl - gmax), axis=-1, keepdims=True)
    g_idx = first_argmax(gl, gmax)
    e_lo = ROUTE_LANE0 + EXPERTS_PER_GROUP * g_idx
    el = jnp.where((lane >= e_lo) & (lane < e_lo + EXPERTS_PER_GROUP), logits, ninf)
    m1 = jnp.max(el, axis=-1, keepdims=True)
    esum = jnp.sum(jnp.exp(el - m1), axis=-1, keepdims=True)
    i1 = first_argmax(el, m1)
    el2 = jnp.where(lane == i1, ninf, el)
    m2 = jnp.max(el2, axis=-1, keepdims=True)
    i2 = first_argmax(el2, m2)
    p1 = 1.0 / esum
    p2 = jnp.exp(m2 - m1) / esum
    w1 = g_w * (p1 / (p1 + p2))
    w2 = g_w * (p2 / (p1 + p2))

    oh = jnp.where((lane == i1) | (lane == i2), 1.0, 0.0)
    r_i = lax.broadcasted_iota(jnp.int32, (tm, tm), 0)
    c_i = lax.broadcasted_iota(jnp.int32, (tm, tm), 1)
    tri = jnp.where(c_i < r_i, 1.0, 0.0).astype(BF16)
    before = _dot(tri, oh.astype(BF16)) + run_ref[...]
    rank1 = jnp.sum(jnp.where(lane == i1, before, 0.0), axis=-1, keepdims=True)
    rank2 = jnp.sum(jnp.where(lane == i2, before, 0.0), axis=-1, keepdims=True)
    run_ref[...] = run_ref[...] + jnp.sum(oh, axis=0, keepdims=True)
    cnt_ref[...] = run_ref[...]

    e1 = (i1 - ROUTE_LANE0).astype(F32)
    e2 = (i2 - ROUTE_LANE0).astype(F32)
    vals = (e1, e2, rank1, rank2, w1, w2)
    out = jnp.zeros((tm, LANES), F32)
    for k, v in enumerate(vals):
        out = jnp.where(lane == k, v, out)
    route_ref[...] = out


def _router(x2, g, sh, sc, wr, br, seq, tm=512):
    t, d = x2.shape
    tpb = seq // tm
    return pl.pallas_call(
        _router_kernel,
        out_shape=(jax.ShapeDtypeStruct((t, d), BF16), jax.ShapeDtypeStruct((t, LANES), F32),
                   jax.ShapeDtypeStruct((1, LANES), F32)),
        grid=(t // tm,),
        in_specs=[pl.BlockSpec((tm, d), lambda i: (i, 0)),
                  pl.BlockSpec((1, d), lambda i: (0, 0)),
                  pl.BlockSpec((1, 1, d), lambda i: (i // tpb, 0, 0)),
                  pl.BlockSpec((1, 1, d), lambda i: (i // tpb, 0, 0)),
                  pl.BlockSpec(wr.shape, lambda i: (0, 0, 0)),
                  pl.BlockSpec((1, LANES), lambda i: (0, 0))],
        out_specs=(pl.BlockSpec((tm, d), lambda i: (i, 0)), pl.BlockSpec((tm, LANES), lambda i: (i, 0)),
                   pl.BlockSpec((1, LANES), lambda i: (0, 0))),
        scratch_shapes=[pltpu.VMEM((1, LANES), F32)],
        compiler_params=_params(("arbitrary",)),
        name="moe_router",
    )(x2, g, sh, sc, wr, br)


def _expert_kernel(te_ref, nv_ref, x_ref, wg_ref, wu_ref, wd_ref, o_ref, wg_s, wu_s, wd_s):
    i = pl.program_id(0)

    @pl.when((i == 0) | (te_ref[i] != te_ref[jnp.maximum(i - 1, 0)]))
    def _():
        wg_s[...] = wg_ref[0].astype(BF16)
        wu_s[...] = wu_ref[0].astype(BF16)
        wd_s[...] = wd_ref[0].astype(BF16)

    @pl.when(i < nv_ref[0])
    def _():
        x = x_ref[...]
        a = _dot(x, wg_s[...])
        u = _dot(x, wu_s[...])
        he = (a * jax.nn.sigmoid(a)) * u
        o_ref[...] = _dot(he.astype(BF16), wd_s[...])


def _experts(xs, tile_expert, n_valid, wg, wu, wd):
    r, d = xs.shape
    de = wg.shape[2]
    nt = r // MOE_TILE
    row = lambda i, te, nv: (jnp.minimum(i, nv[0] - 1), 0)
    return pl.pallas_call(
        _expert_kernel,
        out_shape=jax.ShapeDtypeStruct((r, d), F32),
        grid_spec=pltpu.PrefetchScalarGridSpec(
            num_scalar_prefetch=2,
            grid=(nt,),
            in_specs=[pl.BlockSpec((MOE_TILE, d), row),
                      pl.BlockSpec((1, d, de), lambda i, te, nv: (te[i], 0, 0)),
                      pl.BlockSpec((1, d, de), lambda i, te, nv: (te[i], 0, 0)),
                      pl.BlockSpec((1, de, d), lambda i, te, nv: (te[i], 0, 0))],
            out_specs=pl.BlockSpec((MOE_TILE, d), row),
            scratch_shapes=[pltpu.VMEM((d, de), BF16), pltpu.VMEM((d, de), BF16), pltpu.VMEM((de, d), BF16)]),
        compiler_params=_params(("arbitrary",)),
        name="moe_experts",
    )(tile_expert, n_valid, xs, wg, wu, wd)


def _combine_kernel(x_ref, g2_ref, route_ref, y1_ref, y2_ref, fg_ref, o_ref, *, final):
    r = route_ref[...]
    w1 = r[:, 4:5]
    w2 = r[:, 5:6]
    x = x_ref[...] + g2_ref[0] * (w1 * y1_ref[...] + w2 * y2_ref[...])
    if final:
        x = (x * lax.rsqrt(jnp.mean(x * x, axis=-1, keepdims=True) + EPS)) * fg_ref[...]
    o_ref[...] = x


def _combine(x2, g2, route, y1, y2, final_g, seq, final, tm=512):
    t, d = x2.shape
    tpb = seq // tm
    kern = functools.partial(_combine_kernel, final=final)
    return pl.pallas_call(
        kern,
        out_shape=jax.ShapeDtypeStruct((t, d), F32),
        grid=(t // tm,),
        in_specs=[pl.BlockSpec((tm, d), lambda i: (i, 0)),
                  pl.BlockSpec((1, 1, d), lambda i: (i // tpb, 0, 0)),
                  pl.BlockSpec((tm, LANES), lambda i: (i, 0)),
                  pl.BlockSpec((tm, d), lambda i: (i, 0)),
                  pl.BlockSpec((tm, d), lambda i: (i, 0)),
                  pl.BlockSpec((1, d), lambda i: (0, 0))],
        out_specs=pl.BlockSpec((tm, d), lambda i: (i, 0)),
        compiler_params=_params(("parallel",)),
        name="moe_combine",
    )(x2, g2, route, y1, y2, final_g)


def _moe(x2, g, sh, sc, g2, wg_r, bg_r, we_r, be_r, w_gate, w_up, w_down, final_g, seq, final):
    t, d = x2.shape
    wr = jnp.zeros((d, LANES), F32).at[:, :N_GROUPS].set(wg_r).at[:, ROUTE_LANE0:ROUTE_LANE0 + N_EXPERTS].set(we_r)
    wr_hi = wr.astype(BF16)
    wr_lo = (wr - wr_hi.astype(F32)).astype(BF16)
    br = jnp.zeros((1, LANES), F32).at[0, :N_GROUPS].set(bg_r).at[0, ROUTE_LANE0:ROUTE_LANE0 + N_EXPERTS].set(be_r)
    h, route, cnt = _router(x2, g, sh, sc, jnp.stack([wr_hi, wr_lo]), br, seq)

    counts = cnt[0, ROUTE_LANE0:ROUTE_LANE0 + N_EXPERTS].astype(jnp.int32)
    tiles = (counts + MOE_TILE - 1) // MOE_TILE
    tile_end = jnp.cumsum(tiles)
    offs = (tile_end - tiles) * MOE_TILE
    nt = (2 * t) // MOE_TILE + N_EXPERTS
    tile_id = jnp.minimum(jnp.arange(nt, dtype=jnp.int32), tile_end[-1] - 1)
    tile_expert = jnp.sum((tile_id[:, None] >= tile_end[None, :]).astype(jnp.int32), axis=1)
    n_valid = tile_end[-1:].astype(jnp.int32)
    e12 = route[:, 0:2].astype(jnp.int32)
    pos = offs[e12] + route[:, 2:4].astype(jnp.int32)
    tok = jnp.broadcast_to(jnp.arange(t, dtype=jnp.int32)[:, None], (t, 2))
    sorted_tok = jnp.zeros((nt * MOE_TILE,), jnp.int32).at[pos.reshape(-1)].set(tok.reshape(-1))
    xs = jnp.take(h, sorted_tok, axis=0)
    ys = _experts(xs, tile_expert, n_valid, w_gate, w_up, w_down)
    y1 = jnp.take(ys, pos[:, 0], axis=0)
    y2 = jnp.take(ys, pos[:, 1], axis=0)
    return _combine(x2, g2, route, y1, y2, final_g, seq, final)


def _rope_tables(seq, width):
    hd = DA_HEAD_DIM
    half = hd // 2
    quarter = half // 2
    pos = jnp.arange(seq)
    row = (pos // GRID_W).astype(F32)
    col = (pos % GRID_W).astype(F32)
    inv = ROPE_THETA ** (-jnp.arange(0, half, 2, dtype=F32) / half)
    i = jnp.arange(hd)
    p = jnp.where((i < half)[None, :], row[:, None], col[:, None])
    ang = p * inv[i % quarter][None, :]
    sign = jnp.where((i % half) < quarter, -1.0, 1.0)[None, :]
    reps = width // hd
    return jnp.tile(jnp.cos(ang), (1, reps)), jnp.tile(jnp.sin(ang) * sign, (1, reps))


def _rope_partner(width):
    i = jnp.arange(width)
    quarter = DA_HEAD_DIM // 4
    return jnp.where((i % (2 * quarter)) < quarter, i + quarter, i - quarter)


def kernel(x, c, ctx, c_ctx, ada_w, ada_b, norm1_g, norm2_g, final_g, w_in0, w_out0, lam_q1, lam_k1, lam_q2, lam_k2, subln_g, hy_short_w, hy_short_b, hy_w1, hy_b1, hy_fr1, hy_w2, hy_b2, hy_fr2, hy_w3, hy_b3, hy_bias, cv_w1, cv_b1, cv_dw_w, cv_dw_b, cv_ln_g, cv_ln_b, cv_w2, cv_b2, moe_wg, moe_bg, moe_we, moe_be, moe_w_gate, moe_w_up, moe_w_down):
    bsz, seq, d = x.shape
    lctx = ctx.shape[1]
    depth = ada_w.shape[0]
    t = bsz * seq
    hyw = d - DA_WIDTH
    x2 = x.reshape(t, d)

    rows = ((bsz + 1 + SUBLANES - 1) // SUBLANES) * SUBLANES
    cs = jnp.zeros((rows, d), F32).at[:bsz].set(c).at[bsz].set(c_ctx)
    mods = _ada(cs, ada_w, ada_b)

    def tok_mod(i, k):
        return mods[i, :bsz, k * d:(k + 1) * d].reshape(bsz, 1, d)

    def ctx_mod(i, k):
        return mods[i, bsz:bsz + 1, k * d:(k + 1) * d].reshape(1, 1, d)

    for i in range(depth):
        j = i // 2
        g_n1 = norm1_g[i].reshape(1, d)
        if i % 2 == 0:
            assert not any(m % 2 == 0 for m in range(i + 1, depth)), "context-stream update is not implemented"
            lam_init = 0.8 - 0.6 * math.exp(-0.3 * i)
            w_in = w_in0[j]
            part = _rope_partner(2 * DA_WIDTH)
            w_ext = jnp.concatenate([w_in, w_in[:, :2 * DA_WIDTH][:, part]], axis=1).astype(BF16)
            cw = 2 * LANES
            cos, sin = _rope_tables(seq, cw)
            q, k, v, u = _inproj(x2, g_n1, tok_mod(i, 0), tok_mod(i, 1), cos, sin, w_ext, seq)
            kc, vc = _ctxproj(ctx.reshape(bsz * lctx, d), g_n1, ctx_mod(i, 0), ctx_mod(i, 1),
                              w_in[:, DA_WIDTH:3 * DA_WIDTH].astype(BF16), lctx)
            lam_p = jnp.stack([lam_q1[j], lam_k1[j], lam_q2[j], lam_k2[j]])
            o_a = _attention(q, kc, vc, k, v, lam_p, subln_g[j].reshape(1, -1), seq, lctx, lam_init)

            nb = seq // LC_P
            cb = 8
            ucm = _dwconv(u.reshape(bsz, seq, 3 * hyw), hy_short_w[j], hy_short_b[j],
                          channel_major=True).reshape(3 * hyw, bsz * nb, LC_P)
            kc_ext = _hyena_filters(seq, hy_w1[j], hy_b1[j], hy_fr1[j], hy_w2[j], hy_b2[j], hy_fr2[j],
                                    hy_w3[j], hy_b3[j])
            bias_cm = jnp.broadcast_to(hy_bias[j].reshape(HY_ORDER * hyw, 1, 1), (HY_ORDER * hyw, 1, LC_P))
            z1 = _longconv(ucm, 0, ucm, hyw // cb, kc_ext, bias_cm, 0, hyw, nb, bsz, cb)
            z2 = _longconv(z1, 0, ucm, 2 * hyw // cb, kc_ext, bias_cm, 1, hyw, nb, bsz, cb)
            x2 = _outproj(x2, tok_mod(i, 2), o_a, z2.reshape(hyw, t), w_out0[j].astype(BF16), seq)
        else:
            a = _confin(x2, g_n1, tok_mod(i, 0), tok_mod(i, 1), cv_w1[j].astype(BF16), cv_b1[j], seq)
            a = _dwconv(a.reshape(bsz, seq, -1), cv_dw_w[j], cv_dw_b[j]).reshape(t, -1)
            x2 = _confout(x2, tok_mod(i, 2), a, cv_ln_g[j], cv_ln_b[j], cv_w2[j].astype(BF16), cv_b2[j], seq)
        x2 = _moe(x2, norm2_g[i].reshape(1, d), tok_mod(i, 3), tok_mod(i, 4), tok_mod(i, 5),
                  moe_wg[i], moe_bg[i], moe_we[i], moe_be[i], moe_w_gate[i], moe_w_up[i], moe_w_down[i],
                  final_g.reshape(1, d), seq, final=(i == depth - 1))
    return x2.reshape(bsz, seq, d)
```

```python
import functools
import math

import jax
import jax.numpy as jnp
from jax import lax
from jax.experimental import pallas as pl
from jax.experimental.pallas import tpu as pltpu

F32 = jnp.float32
BF16 = jnp.bfloat16

GRID_W = 64
DA_HEADS = 4
DA_HEAD_DIM = 64
DA_WIDTH = DA_HEADS * 2 * DA_HEAD_DIM
HY_ORDER = 2
HY_TARGET = 1e-2
HY_MIN_DECAY = math.log(HY_TARGET) / 0.3
HY_MAX_DECAY = math.log(HY_TARGET) / 1.5
N_GROUPS = 4
EXPERTS_PER_GROUP = 8
N_EXPERTS = N_GROUPS * EXPERTS_PER_GROUP
ROPE_THETA = 10000.0
EPS = 1e-6

LANES = 128
SUBLANES = 8
VMEM_LIMIT = 52 * 1024 * 1024
ROUTE_LANE0 = N_GROUPS
MOE_TILE = 256
LC_P = 128
LC_W = 1024


def _params(sem):
    return pltpu.CompilerParams(dimension_semantics=sem, vmem_limit_bytes=VMEM_LIMIT)


def _split(a):
    hi = a.astype(BF16)
    lo = (a - hi.astype(F32)).astype(BF16)
    return hi, lo


def _dot(a, b):
    return jnp.dot(a, b, preferred_element_type=F32)


def _dot3(a, b):
    ah, al = _split(a)
    bh, bl = _split(b)
    return _dot(ah, bh) + _dot(al, bh) + _dot(ah, bl)


def _norm_mod(x, g, sh, sc):
    y = x * lax.rsqrt(jnp.mean(x * x, axis=-1, keepdims=True) + EPS)
    return (y * g) * (1.0 + sc) + sh


def _ada_kernel(c_ref, w_ref, b_ref, o_ref):
    c = c_ref[...]
    s = c * jax.nn.sigmoid(c)
    o_ref[0] = _dot3(s, w_ref[0]) + b_ref[0]


def _ada(cs, ada_w, ada_b):
    depth, d, n6 = ada_w.shape
    rows = cs.shape[0]
    tn = 1536
    return pl.pallas_call(
        _ada_kernel,
        out_shape=jax.ShapeDtypeStruct((depth, rows, n6), F32),
        grid=(depth, n6 // tn),
        in_specs=[pl.BlockSpec((rows, d), lambda l, j: (0, 0)),
                  pl.BlockSpec((1, d, tn), lambda l, j: (l, 0, j)),
                  pl.BlockSpec((1, 1, tn), lambda l, j: (l, 0, j))],
        out_specs=pl.BlockSpec((1, rows, tn), lambda l, j: (l, 0, j)),
        compiler_params=_params(("parallel", "parallel")),
        name="adaln",
    )(cs, ada_w, ada_b.reshape(depth, 1, n6))


def _inproj_kernel(x_ref, g_ref, sh_ref, sc_ref, cos_ref, sin_ref, w_ref,
                   q_ref, k_ref, v_ref, u_ref, *, d_attn, d_hy):
    h = _norm_mod(x_ref[...], g_ref[...], sh_ref[0], sc_ref[0]).astype(BF16)
    cos = cos_ref[...]
    sin = sin_ref[...]
    rot0 = 3 * d_attn + d_hy
    cw = cos.shape[1]

    def mm(c0, c1):
        return _dot(h, w_ref[:, c0:c1])

    for j in range(d_attn // cw):
        qa = mm(j * cw, (j + 1) * cw)
        qr = mm(rot0 + j * cw, rot0 + (j + 1) * cw)
        q_ref[:, j * cw:(j + 1) * cw] = ((qa * cos + qr * sin) * (DA_HEAD_DIM ** -0.5)).astype(BF16)
        ka = mm(d_attn + j * cw, d_attn + (j + 1) * cw)
        kr = mm(rot0 + d_attn + j * cw, rot0 + d_attn + (j + 1) * cw)
        k_ref[:, j * cw:(j + 1) * cw] = (ka * cos + kr * sin).astype(BF16)
    v_ref[...] = mm(2 * d_attn, 3 * d_attn).astype(BF16)
    for j in range(d_hy // 512):
        u_ref[:, j * 512:(j + 1) * 512] = mm(3 * d_attn + j * 512, 3 * d_attn + (j + 1) * 512)


def _inproj(x2, g, sh, sc, cos, sin, w_ext, seq, tm=512):
    t, d = x2.shape
    tpb = seq // tm
    d_attn = DA_WIDTH
    d_hy = w_ext.shape[1] - 5 * d_attn
    cw = cos.shape[1]
    kern = functools.partial(_inproj_kernel, d_attn=d_attn, d_hy=d_hy)
    return pl.pallas_call(
        kern,
        out_shape=(jax.ShapeDtypeStruct((t, d_attn), BF16), jax.ShapeDtypeStruct((t, d_attn), BF16),
                   jax.ShapeDtypeStruct((t, d_attn), BF16), jax.ShapeDtypeStruct((t, d_hy), F32)),
        grid=(t // tm,),
        in_specs=[pl.BlockSpec((tm, d), lambda i: (i, 0)),
                  pl.BlockSpec((1, d), lambda i: (0, 0)),
                  pl.BlockSpec((1, 1, d), lambda i: (i // tpb, 0, 0)),
                  pl.BlockSpec((1, 1, d), lambda i: (i // tpb, 0, 0)),
                  pl.BlockSpec((tm, cw), lambda i: (i % tpb, 0)),
                  pl.BlockSpec((tm, cw), lambda i: (i % tpb, 0)),
                  pl.BlockSpec(w_ext.shape, lambda i: (0, 0))],
        out_specs=(pl.BlockSpec((tm, d_attn), lambda i: (i, 0)), pl.BlockSpec((tm, d_attn), lambda i: (i, 0)),
                   pl.BlockSpec((tm, d_attn), lambda i: (i, 0)), pl.BlockSpec((tm, d_hy), lambda i: (i, 0))),
        compiler_params=_params(("parallel",)),
        name="inproj0",
    )(x2, g, sh, sc, cos, sin, w_ext)


def _ctxproj_kernel(x_ref, g_ref, sh_ref, sc_ref, w_ref, k_ref, v_ref):
    h = _norm_mod(x_ref[...], g_ref[...], sh_ref[0], sc_ref[0]).astype(BF16)
    n = k_ref.shape[1]
    k_ref[...] = _dot(h, w_ref[:, :n]).astype(BF16)
    v_ref[...] = _dot(h, w_ref[:, n:]).astype(BF16)


def _ctxproj(c2, g, sh, sc, w_kv, tm):
    t, d = c2.shape
    n = w_kv.shape[1] // 2
    return pl.pallas_call(
        _ctxproj_kernel,
        out_shape=(jax.ShapeDtypeStruct((t, n), BF16), jax.ShapeDtypeStruct((t, n), BF16)),
        grid=(t // tm,),
        in_specs=[pl.BlockSpec((tm, d), lambda i: (i, 0)),
                  pl.BlockSpec((1, d), lambda i: (0, 0)),
                  pl.BlockSpec((1, 1, d), lambda i: (0, 0, 0)),
                  pl.BlockSpec((1, 1, d), lambda i: (0, 0, 0)),
                  pl.BlockSpec(w_kv.shape, lambda i: (0, 0))],
        out_specs=(pl.BlockSpec((tm, n), lambda i: (i, 0)), pl.BlockSpec((tm, n), lambda i: (i, 0))),
        compiler_params=_params(("parallel",)),
        name="ctxproj",
    )(c2, g, sh, sc, w_kv)


ATT_KCHUNK = 256
ATT_STRIP = 16
ATT_PV_KEYS = 1024
ATT_LEAD = 2


def _attn_kernel(lam_ref, g_ref, zero_ref, q_ref, kc_ref, vc_ref, k_ref, v_ref, o_ref,
                 s_a, s_b, e_a, e_b, l_a, l_b, acc_ref, *, lam_init, lctx, seq):
    i = pl.program_id(2)
    tq = q_ref.shape[0]

    @pl.when(i == 0)
    def _():
        s_b[...] = jnp.zeros_like(s_b)
        e_a[...] = jnp.zeros_like(e_a)
        l_a[...] = jnp.ones_like(l_a)

    chunks = ([(kc_ref, vc_ref, r0, r0) for r0 in range(0, lctx, ATT_KCHUNK)]
              + [(k_ref, v_ref, r0, lctx + r0) for r0 in range(0, seq, ATT_KCHUNK)])
    pieces = ([(vc_ref, 0, lctx, 0)]
              + [(v_ref, r0, min(ATT_PV_KEYS, seq - r0), lctx + r0) for r0 in range(0, seq, ATT_PV_KEYS)])
    strips = list(range(0, tq, ATT_STRIP))
    per_chunk = -(-len(strips) // len(chunks))

    def stages(s_w, s_r, e_w, e_r, l_w, l_r):
        q = q_ref[...]
        lane = lax.broadcasted_iota(jnp.int32, q.shape, 1)
        zero = jnp.zeros_like(q)
        nt = (((1,), (1,)), ((), ()))
        qh = (jnp.where(lane < DA_HEAD_DIM, q, zero), jnp.where(lane >= DA_HEAD_DIM, q, zero))
        gate = []
        for ci, (kr, vr, r0, col) in enumerate(chunks):
            rows = slice(r0, r0 + ATT_KCHUNK)
            cols = slice(col, col + ATT_KCHUNK)
            held = gate[ci - ATT_LEAD] if ci >= ATT_LEAD else None
            for hd in range(2):
                lhs = qh[hd] if held is None else qh[hd] + held
                s_w[hd, :, cols] = lax.dot_general(lhs, kr[rows, :], nt, preferred_element_type=F32)
            for pi, (pv, p0, pn, pcol) in enumerate(pieces):
                if (pi * len(chunks)) // len(pieces) != ci:
                    continue
                vals = pv[p0:p0 + pn, :] if held is None else pv[p0:p0 + pn, :] + held
                for hd in range(2):
                    part = _dot(e_r[hd, :, pcol:pcol + pn], vals)
                    if pi == 0:
                        acc_ref[hd] = part
                    else:
                        acc_ref[hd] += part
            sums = None
            for r in strips[ci * per_chunk:(ci + 1) * per_chunk]:
                st = slice(r, r + ATT_STRIP)
                for hd in range(2):
                    cblocks = [slice(c0, c0 + ATT_KCHUNK) for c0 in range(0, lctx + seq, ATT_KCHUNK)]
                    mx = s_r[hd, st, cblocks[0]]
                    for cs in cblocks[1:]:
                        mx = jnp.maximum(mx, s_r[hd, st, cs])
                    m = jnp.broadcast_to(jnp.max(mx, axis=-1, keepdims=True), mx.shape)
                    tot = None
                    for cs in cblocks:
                        e = jnp.exp(s_r[hd, st, cs] - m)
                        e_w[hd, st, cs] = e.astype(BF16)
                        tot = e if tot is None else tot + e
                    l = jnp.broadcast_to(jnp.sum(tot, axis=-1, keepdims=True), (ATT_STRIP, LANES))
                    l_w[hd, st, :] = l
                    sums = l if sums is None else sums + l
            if sums is None:
                gate.append(None if not gate else gate[-1])
            else:
                bits = pltpu.bitcast(sums, jnp.int32) & zero_ref[...]
                gate.append(pltpu.bitcast(bits, F32)[0:1, :].astype(BF16))

        lp = lam_ref[...]
        lam = (jnp.exp(jnp.sum(lp[0:1] * lp[1:2], axis=-1, keepdims=True))
               - jnp.exp(jnp.sum(lp[2:3] * lp[3:4], axis=-1, keepdims=True)) + lam_init)
        o = acc_ref[0] * (1.0 / l_r[0]) - acc_ref[1] * (lam / l_r[1])
        y = o * lax.rsqrt(jnp.mean(o * o, axis=-1, keepdims=True) + EPS)
        o_ref[...] = (y * g_ref[...]) * (1.0 - lam_init)

    @pl.when(i % 2 == 0)
    def _():
        stages(s_a, s_b, e_b, e_a, l_b, l_a)

    @pl.when(i % 2 == 1)
    def _():
        stages(s_b, s_a, e_a, e_b, l_a, l_b)


def _attention(q, kc, vc, k, v, lam_p, subln_g, seq, lctx, lam_init, tq=256):
    t = q.shape[0]
    b = t // seq
    hw = 2 * DA_HEAD_DIM
    nq = seq // tq
    lk = lctx + seq
    kern = functools.partial(_attn_kernel, lam_init=lam_init, lctx=lctx, seq=seq)
    q_tile = lambda bi, h, i: (bi * nq + jnp.minimum(i, nq - 1), h)
    o_tile = lambda bi, h, i: (bi * nq + jnp.clip(i - 2, 0, nq - 1), h)
    return pl.pallas_call(
        kern,
        out_shape=jax.ShapeDtypeStruct((t, DA_WIDTH), F32),
        grid=(b, DA_HEADS, nq + 2),
        in_specs=[pl.BlockSpec(lam_p.shape, lambda bi, h, i: (0, 0)),
                  pl.BlockSpec((1, hw), lambda bi, h, i: (0, 0)),
                  pl.BlockSpec((1, LANES), lambda bi, h, i: (0, 0)),
                  pl.BlockSpec((tq, hw), q_tile),
                  pl.BlockSpec((lctx, hw), lambda bi, h, i: (bi, h)),
                  pl.BlockSpec((lctx, hw), lambda bi, h, i: (bi, h)),
                  pl.BlockSpec((seq, hw), lambda bi, h, i: (bi, h)),
                  pl.BlockSpec((seq, hw), lambda bi, h, i: (bi, h))],
        out_specs=pl.BlockSpec((tq, hw), o_tile),
        scratch_shapes=[pltpu.VMEM((2, tq, lk), F32), pltpu.VMEM((2, tq, lk), F32),
                        pltpu.VMEM((2, tq, lk), BF16), pltpu.VMEM((2, tq, lk), BF16),
                        pltpu.VMEM((2, tq, LANES), F32), pltpu.VMEM((2, tq, LANES), F32),
                        pltpu.VMEM((2, tq, hw), F32)],
        compiler_params=_params(("parallel", "parallel", "arbitrary")),
        name="diff_attn",
    )(lam_p, subln_g, jnp.zeros((1, LANES), jnp.int32), q, kc, vc, k, v)


def _dwconv_kernel(x_ref, w_ref, b_ref, o_ref, pad_ref, *, kw, halo, channel_major):
    s = x_ref.shape[1]
    pl_ = (kw - 1) // 2
    zeros = jnp.zeros((halo, x_ref.shape[2]), F32)
    pad_ref[0:halo, :] = zeros
    pad_ref[halo + s:halo + s + halo, :] = zeros
    pad_ref[halo:halo + s, :] = x_ref[0]
    w = w_ref[...]
    acc = jnp.zeros((s, x_ref.shape[2]), F32) + b_ref[...]
    for j in range(kw):
        off = halo - pl_ + j
        acc = acc + pad_ref[off:off + s, :] * w[j:j + 1, :]
    if channel_major:
        acc_t = acc.T
        for g in range(LANES // SUBLANES):
            for blk in range(s // LANES):
                o_ref[g, blk * SUBLANES:(blk + 1) * SUBLANES, :] = (
                    acc_t[g * SUBLANES:(g + 1) * SUBLANES, blk * LANES:(blk + 1) * LANES])
    else:
        o_ref[0] = acc


def _dwconv(x3, w, bias, channel_major=False):
    b, s, c = x3.shape
    kw = w.shape[0]
    halo = 2 * SUBLANES
    kern = functools.partial(_dwconv_kernel, kw=kw, halo=halo, channel_major=channel_major)
    if channel_major:
        gpb = LANES // SUBLANES
        out_shape = jax.ShapeDtypeStruct((c // SUBLANES, b * (s // LANES) * SUBLANES, LANES), F32)
        out_spec = pl.BlockSpec((gpb, (s // LANES) * SUBLANES, LANES), lambda bi, ci: (ci, bi, 0))
    else:
        out_shape = jax.ShapeDtypeStruct((b, s, c), F32)
        out_spec = pl.BlockSpec((1, s, LANES), lambda bi, ci: (bi, 0, ci))
    return pl.pallas_call(
        kern,
        out_shape=out_shape,
        grid=(b, c // LANES),
        in_specs=[pl.BlockSpec((1, s, LANES), lambda bi, ci: (bi, 0, ci)),
                  pl.BlockSpec((kw, LANES), lambda bi, ci: (0, ci)),
                  pl.BlockSpec((1, LANES), lambda bi, ci: (0, ci))],
        out_specs=out_spec,
        scratch_shapes=[pltpu.VMEM((s + 2 * halo, LANES), F32)],
        compiler_params=_params(("parallel", "parallel")),
        name="dwconv",
    )(x3, w, bias.reshape(1, c))


def _filter_kernel(zt_ref, t_ref, w1_ref, b1_ref, f1_ref, w2_ref, b2_ref, f2_ref, w3_ref, b3_ref, dl_ref,
                   o_ref, h_ref, *, n):
    @pl.when(pl.program_id(0) == 0)
    def _():
        h1 = jnp.sin(f1_ref[...] * (_dot3(w1_ref[...], zt_ref[...]) + b1_ref[...]))
        h_ref[...] = jnp.sin(f2_ref[...] * (_dot3(w2_ref[...], h1) + b2_ref[...]))

    h2 = h_ref[...]
    rows = o_ref.shape[0]
    fwd = _dot3(w3_ref[0], h2[:, :n]) + b3_ref[0]
    bwd = _dot3(w3_ref[1], h2[:, n:]) + b3_ref[1]
    decay = jnp.exp(-t_ref[...] * dl_ref[...])
    lane = lax.broadcasted_iota(jnp.int32, (rows, n), 1)
    kf = fwd * decay[:, :n]
    kb = jnp.where(lane == 0, 0.0, bwd * decay[:, n:])
    inv = 1.0 / (jnp.sum(jnp.abs(kf), axis=-1, keepdims=True) + jnp.sum(jnp.abs(kb), axis=-1, keepdims=True))
    kf = kf * inv
    kb = kb * inv
    o_ref[:, 0:LANES] = kb[:, n - LANES:]
    o_ref[:, LANES:LANES + n] = kf
    o_ref[:, LANES + n:] = kb


def _hyena_filters(n, w1, b1, fr1, w2, b2, fr2, w3, b3):
    emb, ffn = w1.shape
    c = w3.shape[1] // (2 * HY_ORDER)
    bands = (emb - 1) // 2
    t = jnp.linspace(0.0, 1.0, n, dtype=F32)[:, None]
    ang = (2.0 * math.pi / n) * jnp.arange(n, dtype=F32)[:, None] * jnp.linspace(1e-4, bands - 1, bands, dtype=F32)[None, :]
    z = jnp.concatenate([t, jnp.cos(ang), -jnp.sin(ang)], axis=-1)
    rev = (n - jnp.arange(n)) % n
    z2 = jnp.concatenate([z, z[rev]], axis=0)
    t2 = jnp.concatenate([t, t[rev]], axis=0).reshape(1, 2 * n)
    emb_p = ((emb + SUBLANES - 1) // SUBLANES) * SUBLANES
    zt = jnp.zeros((emb_p, 2 * n), F32).at[:emb].set(z2.T)
    w1t = jnp.zeros((ffn, emb_p), F32).at[:, :emb].set(w1.T)
    deltas = jnp.abs(jnp.linspace(HY_MIN_DECAY, HY_MAX_DECAY, c, dtype=F32))
    w3t = w3.T.reshape(HY_ORDER, 2, c, ffn).transpose(1, 0, 2, 3).reshape(2, HY_ORDER * c, ffn)
    b3t = b3.reshape(HY_ORDER, 2, c).transpose(1, 0, 2).reshape(2, HY_ORDER * c, 1)
    dl = jnp.tile(deltas, HY_ORDER).reshape(HY_ORDER * c, 1)
    rows = LANES
    kern = functools.partial(_filter_kernel, n=n)
    col = lambda v: v.reshape(ffn, 1)
    return pl.pallas_call(
        kern,
        out_shape=jax.ShapeDtypeStruct((HY_ORDER * c, LANES + 2 * n), F32),
        grid=(HY_ORDER * c // rows,),
        in_specs=[pl.BlockSpec(zt.shape, lambda i: (0, 0)),
                  pl.BlockSpec(t2.shape, lambda i: (0, 0)),
                  pl.BlockSpec(w1t.shape, lambda i: (0, 0)),
                  pl.BlockSpec((ffn, 1), lambda i: (0, 0)),
                  pl.BlockSpec((ffn, 1), lambda i: (0, 0)),
                  pl.BlockSpec((ffn, ffn), lambda i: (0, 0)),
                  pl.BlockSpec((ffn, 1), lambda i: (0, 0)),
                  pl.BlockSpec((ffn, 1), lambda i: (0, 0)),
                  pl.BlockSpec((2, rows, ffn), lambda i: (0, i, 0)),
                  pl.BlockSpec((2, rows, 1), lambda i: (0, i, 0)),
                  pl.BlockSpec((rows, 1), lambda i: (i, 0))],
        out_specs=pl.BlockSpec((rows, LANES + 2 * n), lambda i: (i, 0)),
        scratch_shapes=[pltpu.VMEM((ffn, 2 * n), F32)],
        compiler_params=_params(("arbitrary",)),
        name="hyena_filters",
    )(zt, t2, w1t, col(b1), col(fr1), w2.T, col(b2), col(fr2), w3t, b3t, dl)


def _longconv_kernel(z_ref, gate_ref, kc_ref, bias_ref, o_ref, r_ref, zs_ref, acc_ref, *, nb, bsz, cb):
    p = LC_P
    n2 = 2 * nb * p

    def channel(c, carry):
        w = min(LC_W, n2)
        for ci in range(n2 // w):
            win = kc_ref[pl.ds(c, 1), ci * w:ci * w + w + LANES]
            rolled = pltpu.roll(jnp.broadcast_to(win, (p, w + LANES)), 0, 1, stride=1, stride_axis=0)
            r_ref[:, ci * w:(ci + 1) * w] = rolled[:, LANES:].astype(BF16)
        for s1 in range(nb):
            zs_ref[s1 * bsz:(s1 + 1) * bsz, :] = z_ref[0, pl.ds(c + cb * s1, bsz, stride=cb * nb), :]
        acc_ref[...] = jnp.zeros_like(acc_ref)
        for pi in range(nb):
            d = -nb + 2 * pi
            off = (d % (2 * nb)) * p
            lo = max(0, -d - 1)
            hi = min(nb, nb - d)
            lhs = zs_ref[lo * bsz:hi * bsz, :].astype(BF16)
            out = _dot(lhs, r_ref[:, off:off + 2 * p])
            for k in range(2):
                dk = d + k
                a0, a1 = max(0, -dk), min(nb, nb - dk)
                if a1 <= a0:
                    continue
                acc_ref[(a0 + dk) * bsz:(a1 + dk) * bsz, :] += out[(a0 - lo) * bsz:(a1 - lo) * bsz, k * p:(k + 1) * p]
        bias = bias_ref[c]
        for s1 in range(nb):
            rows = slice(s1 * bsz, (s1 + 1) * bsz)
            where = pl.ds(c + cb * s1, bsz, stride=cb * nb)
            o_ref[0, where, :] = gate_ref[0, where, :] * (acc_ref[rows, :] + zs_ref[rows, :] * bias)
        return carry

    lax.fori_loop(0, cb, channel, 0)


def _longconv(z_cm, z_g0, gate_cm, gate_g0, kc_ext, bias_cm, order, groups, nb, bsz):
    cb = SUBLANES
    rows, p = z_cm.shape[1:]
    kern = functools.partial(_longconv_kernel, nb=nb, bsz=bsz, cb=cb)
    return pl.pallas_call(
        kern,
        out_shape=jax.ShapeDtypeStruct((groups, rows, p), F32),
        grid=(groups,),
        in_specs=[pl.BlockSpec((1, rows, p), lambda i: (z_g0 + i, 0, 0)),
                  pl.BlockSpec((1, rows, p), lambda i: (gate_g0 + i, 0, 0)),
                  pl.BlockSpec((cb, kc_ext.shape[1]), lambda i: (order * groups + i, 0)),
                  pl.BlockSpec((cb, 1, p), lambda i: (order * groups + i, 0, 0))],
        out_specs=pl.BlockSpec((1, rows, p), lambda i: (i, 0, 0)),
        scratch_shapes=[pltpu.VMEM((p, 2 * nb * p), BF16), pltpu.VMEM((nb * bsz, p), F32),
                        pltpu.VMEM((nb * bsz, p), F32)],
        compiler_params=_params(("parallel",)),
        name="hyena_longconv",
    )(z_cm, gate_cm, kc_ext, bias_cm)


def _outproj_kernel(x_ref, g1_ref, a_ref, b_ref, w_ref, o_ref):
    da = a_ref.shape[1]
    ya = _dot(a_ref[...].astype(BF16), w_ref[:da, :])
    groups, rows, p = b_ref.shape
    parts = []
    for blk in range(rows // SUBLANES):
        cm = b_ref[:, blk * SUBLANES:(blk + 1) * SUBLANES, :].reshape(groups * SUBLANES, p)
        parts.append(_dot(cm.T.astype(BF16), w_ref[da:, :]))
    y = ya + jnp.concatenate(parts, axis=0)
    o_ref[...] = x_ref[...] + g1_ref[0] * y


def _outproj(x2, g1, oa, ob, w, seq, tm=512):
    t, d = x2.shape
    tpb = seq // tm
    return pl.pallas_call(
        _outproj_kernel,
        out_shape=jax.ShapeDtypeStruct((t, d), F32),
        grid=(t // tm,),
        in_specs=[pl.BlockSpec((tm, d), lambda i: (i, 0)),
                  pl.BlockSpec((1, 1, d), lambda i: (i // tpb, 0, 0)),
                  pl.BlockSpec((tm, oa.shape[1]), lambda i: (i, 0)),
                  pl.BlockSpec((ob.shape[0], (tm // LANES) * SUBLANES, LANES), lambda i: (0, i, 0)),
                  pl.BlockSpec(w.shape, lambda i: (0, 0))],
        out_specs=pl.BlockSpec((tm, d), lambda i: (i, 0)),
        compiler_params=_params(("parallel",)),
        name="outproj0",
    )(x2, g1, oa, ob, w)


def _confin_kernel(x_ref, g_ref, sh_ref, sc_ref, w_ref, b_ref, o_ref):
    h = _norm_mod(x_ref[...], g_ref[...], sh_ref[0], sc_ref[0]).astype(BF16)
    n = o_ref.shape[1]
    a = _dot(h, w_ref[:, :n]) + b_ref[:, :n]
    gt = _dot(h, w_ref[:, n:]) + b_ref[:, n:]
    o_ref[...] = a * jax.nn.sigmoid(gt)


def _confin(x2, g, sh, sc, w1, b1, seq, tm=512):
    t, d = x2.shape
    n = w1.shape[1] // 2
    tpb = seq // tm
    return pl.pallas_call(
        _confin_kernel,
        out_shape=jax.ShapeDtypeStruct((t, n), F32),
        grid=(t // tm,),
        in_specs=[pl.BlockSpec((tm, d), lambda i: (i, 0)),
                  pl.BlockSpec((1, d), lambda i: (0, 0)),
                  pl.BlockSpec((1, 1, d), lambda i: (i // tpb, 0, 0)),
                  pl.BlockSpec((1, 1, d), lambda i: (i // tpb, 0, 0)),
                  pl.BlockSpec(w1.shape, lambda i: (0, 0)),
                  pl.BlockSpec((1, 2 * n), lambda i: (0, 0))],
        out_specs=pl.BlockSpec((tm, n), lambda i: (i, 0)),
        compiler_params=_params(("parallel",)),
        name="conformer_in",
    )(x2, g, sh, sc, w1, b1.reshape(1, 2 * n))


def _confout_kernel(x_ref, g1_ref, a_ref, lg_ref, lb_ref, w_ref, b_ref, o_ref):
    a = a_ref[...]
    mu = jnp.mean(a, axis=-1, keepdims=True)
    ac = a - mu
    var = jnp.mean(ac * ac, axis=-1, keepdims=True)
    y = ac * lax.rsqrt(var + EPS) * lg_ref[...] + lb_ref[...]
    y = y * jax.nn.sigmoid(y)
    o_ref[...] = x_ref[...] + g1_ref[0] * (_dot(y.astype(BF16), w_ref[...]) + b_ref[...])


def _confout(x2, g1, a2, ln_g, ln_b, w2, b2, seq, tm=512):
    t, d = x2.shape
    n = a2.shape[1]
    tpb = seq // tm
    return pl.pallas_call(
        _confout_kernel,
        out_shape=jax.ShapeDtypeStruct((t, d), F32),
        grid=(t // tm,),
        in_specs=[pl.BlockSpec((tm, d), lambda i: (i, 0)),
                  pl.BlockSpec((1, 1, d), lambda i: (i // tpb, 0, 0)),
                  pl.BlockSpec((tm, n), lambda i: (i, 0)),
                  pl.BlockSpec((1, n), lambda i: (0, 0)),
                  pl.BlockSpec((1, n), lambda i: (0, 0)),
                  pl.BlockSpec(w2.shape, lambda i: (0, 0)),
                  pl.BlockSpec((1, d), lambda i: (0, 0))],
        out_specs=pl.BlockSpec((tm, d), lambda i: (i, 0)),
        compiler_params=_params(("parallel",)),
        name="conformer_out",
    )(x2, g1, a2, ln_g.reshape(1, n), ln_b.reshape(1, n), w2, b2.reshape(1, d))


def _router_kernel(x_ref, g_ref, sh_ref, sc_ref, wr_ref, br_ref, h_ref, route_ref, cnt_ref, run_ref):
    i = pl.program_id(0)

    @pl.when(i == 0)
    def _():
        run_ref[...] = jnp.zeros_like(run_ref)

    h = _norm_mod(x_ref[...], g_ref[...], sh_ref[0], sc_ref[0])
    hh, hl = _split(h)
    h_ref[...] = hh
    logits = _dot(hh, wr_ref[0]) + _dot(hl, wr_ref[0]) + _dot(hh, wr_ref[1]) + br_ref[...]
    tm = logits.shape[0]
    lane = lax.broadcasted_iota(jnp.int32, (tm, LANES), 1)
    ninf = jnp.float32(-jnp.inf)

    def first_argmax(v, m):
        return jnp.min(jnp.where(v == m, lane, LANES), axis=-1, keepdims=True)

    gl = jnp.where(lane < N_GROUPS, logits, ninf)
    gmax = jnp.max(gl, axis=-1, keepdims=True)
    g_w = 1.0 / jnp.sum(jnp.exp(gl - gmax), axis=-1, keepdims=True)
    g_idx = first_argmax(gl, gmax)
    e_lo = ROUTE_LANE0 + EXPERTS_PER_GROUP * g_idx
    el = jnp.where((lane >= e_lo) & (lane < e_lo + EXPERTS_PER_GROUP), logits, ninf)
    m1 = jnp.max(el, axis=-1, keepdims=True)
    esum = jnp.sum(jnp.exp(el - m1), axis=-1, keepdims=True)
    i1 = first_argmax(el, m1)
    el2 = jnp.where(lane == i1, ninf, el)
    m2 = jnp.max(el2, axis=-1, keepdims=True)
    i2 = first_argmax(el2, m2)
    p1 = 1.0 / esum
    p2 = jnp.exp(m2 - m1) / esum
    w1 = g_w * (p1 / (p1 + p2))
    w2 = g_w * (p2 / (p1 + p2))

    oh = jnp.where((lane == i1) | (lane == i2), 1.0, 0.0)
    r_i = lax.broadcasted_iota(jnp.int32, (tm, tm), 0)
    c_i = lax.broadcasted_iota(jnp.int32, (tm, tm), 1)
    tri = jnp.where(c_i < r_i, 1.0, 0.0).astype(BF16)
    before = _dot(tri, oh.astype(BF16)) + run_ref[...]
    rank1 = jnp.sum(jnp.where(lane == i1, before, 0.0), axis=-1, keepdims=True)
    rank2 = jnp.sum(jnp.where(lane == i2, before, 0.0), axis=-1, keepdims=True)
    run_ref[...] = run_ref[...] + jnp.sum(oh, axis=0, keepdims=True)
    cnt_ref[...] = run_ref[...]

    e1 = (i1 - ROUTE_LANE0).astype(F32)
    e2 = (i2 - ROUTE_LANE0).astype(F32)
    vals = (e1, e2, rank1, rank2, w1, w2)
    out = jnp.zeros((tm, LANES), F32)
    for k, v in enumerate(vals):
        out = jnp.where(lane == k, v, out)
    route_ref[...] = out


def _router(x2, g, sh, sc, wr, br, seq, tm=512):
    t, d = x2.shape
    tpb = seq // tm
    return pl.pallas_call(
        _router_kernel,
        out_shape=(jax.ShapeDtypeStruct((t, d), BF16), jax.ShapeDtypeStruct((t, LANES), F32),
                   jax.ShapeDtypeStruct((1, LANES), F32)),
        grid=(t // tm,),
        in_specs=[pl.BlockSpec((tm, d), lambda i: (i, 0)),
                  pl.BlockSpec((1, d), lambda i: (0, 0)),
                  pl.BlockSpec((1, 1, d), lambda i: (i // tpb, 0, 0)),
                  pl.BlockSpec((1, 1, d), lambda i: (i // tpb, 0, 0)),
                  pl.BlockSpec(wr.shape, lambda i: (0, 0, 0)),
                  pl.BlockSpec((1, LANES), lambda i: (0, 0))],
        out_specs=(pl.BlockSpec((tm, d), lambda i: (i, 0)), pl.BlockSpec((tm, LANES), lambda i: (i, 0)),
                   pl.BlockSpec((1, LANES), lambda i: (0, 0))),
        scratch_shapes=[pltpu.VMEM((1, LANES), F32)],
        compiler_params=_params(("arbitrary",)),
        name="moe_router",
    )(x2, g, sh, sc, wr, br)


def _expert_kernel(te_ref, nv_ref, x_ref, wg_ref, wu_ref, wd_ref, o_ref, wg_s, wu_s, wd_s):
    i = pl.program_id(0)

    @pl.when((i == 0) | (te_ref[i] != te_ref[jnp.maximum(i - 1, 0)]))
    def _():
        wg_s[...] = wg_ref[0].astype(BF16)
        wu_s[...] = wu_ref[0].astype(BF16)
        wd_s[...] = wd_ref[0].astype(BF16)

    @pl.when(i < nv_ref[0])
    def _():
        x = x_ref[...]
        a = _dot(x, wg_s[...])
        u = _dot(x, wu_s[...])
        he = (a * jax.nn.sigmoid(a)) * u
        o_ref[...] = _dot(he.astype(BF16), wd_s[...]).astype(o_ref.dtype)


def _experts(xs, tile_expert, n_valid, wg, wu, wd):
    r, d = xs.shape
    de = wg.shape[2]
    nt = r // MOE_TILE
    row = lambda i, te, nv: (jnp.minimum(i, nv[0] - 1), 0)
    return pl.pallas_call(
        _expert_kernel,
        out_shape=jax.ShapeDtypeStruct((r, d), BF16),
        grid_spec=pltpu.PrefetchScalarGridSpec(
            num_scalar_prefetch=2,
            grid=(nt,),
            in_specs=[pl.BlockSpec((MOE_TILE, d), row),
                      pl.BlockSpec((1, d, de), lambda i, te, nv: (te[i], 0, 0)),
                      pl.BlockSpec((1, d, de), lambda i, te, nv: (te[i], 0, 0)),
                      pl.BlockSpec((1, de, d), lambda i, te, nv: (te[i], 0, 0))],
            out_specs=pl.BlockSpec((MOE_TILE, d), row),
            scratch_shapes=[pltpu.VMEM((d, de), BF16), pltpu.VMEM((d, de), BF16), pltpu.VMEM((de, d), BF16)]),
        compiler_params=_params(("arbitrary",)),
        name="moe_experts",
    )(tile_expert, n_valid, xs, wg, wu, wd)


def _combine_kernel(x_ref, g2_ref, route_ref, y1_ref, y2_ref, fg_ref, o_ref, *, final):
    r = route_ref[...]
    w1 = r[:, 4:5]
    w2 = r[:, 5:6]
    x = x_ref[...] + g2_ref[0] * (w1 * y1_ref[...].astype(F32) + w2 * y2_ref[...].astype(F32))
    if final:
        x = (x * lax.rsqrt(jnp.mean(x * x, axis=-1, keepdims=True) + EPS)) * fg_ref[...]
    o_ref[...] = x


def _combine(x2, g2, route, y12, final_g, seq, final, tm=512):
    t, d = x2.shape
    tpb = seq // tm
    nt = t // tm
    kern = functools.partial(_combine_kernel, final=final)
    return pl.pallas_call(
        kern,
        out_shape=jax.ShapeDtypeStruct((t, d), F32),
        grid=(t // tm,),
        in_specs=[pl.BlockSpec((tm, d), lambda i: (i, 0)),
                  pl.BlockSpec((1, 1, d), lambda i: (i // tpb, 0, 0)),
                  pl.BlockSpec((tm, LANES), lambda i: (i, 0)),
                  pl.BlockSpec((tm, d), lambda i: (i, 0)),
                  pl.BlockSpec((tm, d), lambda i: (nt + i, 0)),
                  pl.BlockSpec((1, d), lambda i: (0, 0))],
        out_specs=pl.BlockSpec((tm, d), lambda i: (i, 0)),
        compiler_params=_params(("parallel",)),
        name="moe_combine",
    )(x2, g2, route, y12, y12, final_g)


def _moe(x2, g, sh, sc, g2, wg_r, bg_r, we_r, be_r, w_gate, w_up, w_down, final_g, seq, final):
    t, d = x2.shape
    wr = jnp.zeros((d, LANES), F32).at[:, :N_GROUPS].set(wg_r).at[:, ROUTE_LANE0:ROUTE_LANE0 + N_EXPERTS].set(we_r)
    wr_hi = wr.astype(BF16)
    wr_lo = (wr - wr_hi.astype(F32)).astype(BF16)
    br = jnp.zeros((1, LANES), F32).at[0, :N_GROUPS].set(bg_r).at[0, ROUTE_LANE0:ROUTE_LANE0 + N_EXPERTS].set(be_r)
    h, route, cnt = _router(x2, g, sh, sc, jnp.stack([wr_hi, wr_lo]), br, seq)

    counts = cnt[0, ROUTE_LANE0:ROUTE_LANE0 + N_EXPERTS].astype(jnp.int32)
    tiles = (counts + MOE_TILE - 1) // MOE_TILE
    tile_end = jnp.cumsum(tiles)
    offs = (tile_end - tiles) * MOE_TILE
    nt = (2 * t) // MOE_TILE + N_EXPERTS
    tile_id = jnp.minimum(jnp.arange(nt, dtype=jnp.int32), tile_end[-1] - 1)
    tile_expert = jnp.sum((tile_id[:, None] >= tile_end[None, :]).astype(jnp.int32), axis=1)
    n_valid = tile_end[-1:].astype(jnp.int32)
    e12 = route[:, 0:2].astype(jnp.int32)
    offs12 = jnp.sum(jnp.where(e12[:, :, None] == jnp.arange(N_EXPERTS)[None, None, :], offs[None, None, :], 0), axis=-1)
    pos = (offs12 + route[:, 2:4].astype(jnp.int32)).T.reshape(-1)
    tok = jnp.tile(jnp.arange(t, dtype=jnp.int32), 2)
    sorted_tok = jnp.zeros((nt * MOE_TILE,), jnp.int32).at[pos].set(
        tok, unique_indices=True, indices_are_sorted=False, mode="promise_in_bounds")
    xs = h.at[sorted_tok].get(mode="promise_in_bounds")
    ys = _experts(xs, tile_expert, n_valid, w_gate, w_up, w_down)
    y12 = ys.at[pos].get(mode="promise_in_bounds")
    return _combine(x2, g2, route, y12, final_g, seq, final)


def _rope_tables(seq, width):
    hd = DA_HEAD_DIM
    half = hd // 2
    quarter = half // 2
    pos = jnp.arange(seq)
    row = (pos // GRID_W).astype(F32)
    col = (pos % GRID_W).astype(F32)
    inv = ROPE_THETA ** (-jnp.arange(0, half, 2, dtype=F32) / half)
    i = jnp.arange(hd)
    p = jnp.where((i < half)[None, :], row[:, None], col[:, None])
    ang = p * inv[i % quarter][None, :]
    sign = jnp.where((i % half) < quarter, -1.0, 1.0)[None, :]
    reps = width // hd
    return jnp.tile(jnp.cos(ang), (1, reps)), jnp.tile(jnp.sin(ang) * sign, (1, reps))


def _rope_partner(width):
    i = jnp.arange(width)
    quarter = DA_HEAD_DIM // 4
    return jnp.where((i % (2 * quarter)) < quarter, i + quarter, i - quarter)


def kernel(x, c, ctx, c_ctx, ada_w, ada_b, norm1_g, norm2_g, final_g, w_in0, w_out0, lam_q1, lam_k1, lam_q2, lam_k2, subln_g, hy_short_w, hy_short_b, hy_w1, hy_b1, hy_fr1, hy_w2, hy_b2, hy_fr2, hy_w3, hy_b3, hy_bias, cv_w1, cv_b1, cv_dw_w, cv_dw_b, cv_ln_g, cv_ln_b, cv_w2, cv_b2, moe_wg, moe_bg, moe_we, moe_be, moe_w_gate, moe_w_up, moe_w_down):
    bsz, seq, d = x.shape
    lctx = ctx.shape[1]
    depth = ada_w.shape[0]
    t = bsz * seq
    hyw = d - DA_WIDTH
    x2 = x.reshape(t, d)

    rows = ((bsz + 1 + SUBLANES - 1) // SUBLANES) * SUBLANES
    cs = jnp.zeros((rows, d), F32).at[:bsz].set(c).at[bsz].set(c_ctx)
    mods = _ada(cs, ada_w, ada_b)

    def tok_mod(i, k):
        return mods[i, :bsz, k * d:(k + 1) * d].reshape(bsz, 1, d)

    def ctx_mod(i, k):
        return mods[i, bsz:bsz + 1, k * d:(k + 1) * d].reshape(1, 1, d)

    for i in range(depth):
        j = i // 2
        g_n1 = norm1_g[i].reshape(1, d)
        if i % 2 == 0:
            assert not any(m % 2 == 0 for m in range(i + 1, depth)), "context-stream update is not implemented"
            lam_init = 0.8 - 0.6 * math.exp(-0.3 * i)
            w_in = w_in0[j]
            part = _rope_partner(2 * DA_WIDTH)
            w_ext = jnp.concatenate([w_in, w_in[:, :2 * DA_WIDTH][:, part]], axis=1).astype(BF16)
            cw = 2 * LANES
            cos, sin = _rope_tables(seq, cw)
            q, k, v, u = _inproj(x2, g_n1, tok_mod(i, 0), tok_mod(i, 1), cos, sin, w_ext, seq)
            kc, vc = _ctxproj(ctx.reshape(bsz * lctx, d), g_n1, ctx_mod(i, 0), ctx_mod(i, 1),
                              w_in[:, DA_WIDTH:3 * DA_WIDTH].astype(BF16), lctx)
            lam_p = jnp.stack([lam_q1[j], lam_k1[j], lam_q2[j], lam_k2[j]])
            o_a = _attention(q, kc, vc, k, v, lam_p, subln_g[j].reshape(1, -1), seq, lctx, lam_init)

            nb = seq // LC_P
            groups = hyw // SUBLANES
            ucm = _dwconv(u.reshape(bsz, seq, 3 * hyw), hy_short_w[j], hy_short_b[j], channel_major=True)
            kc_ext = _hyena_filters(seq, hy_w1[j], hy_b1[j], hy_fr1[j], hy_w2[j], hy_b2[j], hy_fr2[j],
                                    hy_w3[j], hy_b3[j])
            bias_cm = jnp.broadcast_to(hy_bias[j].reshape(HY_ORDER * hyw, 1, 1), (HY_ORDER * hyw, 1, LC_P))
            z1 = _longconv(ucm, 0, ucm, groups, kc_ext, bias_cm, 0, groups, nb, bsz)
            z2 = _longconv(z1, 0, ucm, 2 * groups, kc_ext, bias_cm, 1, groups, nb, bsz)
            x2 = _outproj(x2, tok_mod(i, 2), o_a, z2, w_out0[j].astype(BF16), seq)
        else:
            a = _confin(x2, g_n1, tok_mod(i, 0), tok_mod(i, 1), cv_w1[j].astype(BF16), cv_b1[j], seq)
            a = _dwconv(a.reshape(bsz, seq, -1), cv_dw_w[j], cv_dw_b[j]).reshape(t, -1)
            x2 = _confout(x2, tok_mod(i, 2), a, cv_ln_g[j], cv_ln_b[j], cv_w2[j].astype(BF16), cv_b2[j], seq)
        x2 = _moe(x2, norm2_g[i].reshape(1, d), tok_mod(i, 3), tok_mod(i, 4), tok_mod(i, 5),
                  moe_wg[i], moe_bg[i], moe_we[i], moe_be[i], moe_w_gate[i], moe_w_up[i], moe_w_down[i],
                  final_g.reshape(1, d), seq, final=(i == depth - 1))
    return x2.reshape(bsz, seq, d)
```

```python
import functools
import math

import jax
import jax.numpy as jnp
from jax import lax
from jax.experimental import pallas as pl
from jax.experimental.pallas import tpu as pltpu

F32 = jnp.float32
BF16 = jnp.bfloat16

GRID_W = 64
DA_HEADS = 4
DA_HEAD_DIM = 64
DA_WIDTH = DA_HEADS * 2 * DA_HEAD_DIM
HY_ORDER = 2
HY_TARGET = 1e-2
HY_MIN_DECAY = math.log(HY_TARGET) / 0.3
HY_MAX_DECAY = math.log(HY_TARGET) / 1.5
N_GROUPS = 4
EXPERTS_PER_GROUP = 8
N_EXPERTS = N_GROUPS * EXPERTS_PER_GROUP
ROPE_THETA = 10000.0
EPS = 1e-6

LANES = 128
SUBLANES = 8
VMEM_LIMIT = 52 * 1024 * 1024
ROUTE_LANE0 = N_GROUPS
MOE_TILE = 256
LC_P = 128
LC_W = 1024


def _params(sem):
    return pltpu.CompilerParams(dimension_semantics=sem, vmem_limit_bytes=VMEM_LIMIT)


def _split(a):
    hi = a.astype(BF16)
    lo = (a - hi.astype(F32)).astype(BF16)
    return hi, lo


def _dot(a, b):
    return jnp.dot(a, b, preferred_element_type=F32)


def _dot3(a, b):
    ah, al = _split(a)
    bh, bl = _split(b)
    return _dot(ah, bh) + _dot(al, bh) + _dot(ah, bl)


def _norm_mod(x, g, sh, sc):
    y = x * lax.rsqrt(jnp.mean(x * x, axis=-1, keepdims=True) + EPS)
    return (y * g) * (1.0 + sc) + sh


def _ada_kernel(c_ref, w_ref, b_ref, o_ref):
    c = c_ref[...]
    s = c * jax.nn.sigmoid(c)
    o_ref[0] = _dot3(s, w_ref[0]) + b_ref[0]


def _ada(cs, ada_w, ada_b):
    depth, d, n6 = ada_w.shape
    rows = cs.shape[0]
    tn = 1536
    return pl.pallas_call(
        _ada_kernel,
        out_shape=jax.ShapeDtypeStruct((depth, rows, n6), F32),
        grid=(depth, n6 // tn),
        in_specs=[pl.BlockSpec((rows, d), lambda l, j: (0, 0)),
                  pl.BlockSpec((1, d, tn), lambda l, j: (l, 0, j)),
                  pl.BlockSpec((1, 1, tn), lambda l, j: (l, 0, j))],
        out_specs=pl.BlockSpec((1, rows, tn), lambda l, j: (l, 0, j)),
        compiler_params=_params(("parallel", "parallel")),
        name="adaln",
    )(cs, ada_w, ada_b.reshape(depth, 1, n6))


def _inproj_kernel(x_ref, g_ref, sh_ref, sc_ref, cos_ref, sin_ref, w_ref,
                   q_ref, k_ref, v_ref, u_ref, *, d_attn, d_hy):
    h = _norm_mod(x_ref[...], g_ref[...], sh_ref[0], sc_ref[0]).astype(BF16)
    cos = cos_ref[...]
    sin = sin_ref[...]
    rot0 = 3 * d_attn + d_hy
    cw = cos.shape[1]

    def mm(c0, c1):
        return _dot(h, w_ref[:, c0:c1])

    for j in range(d_attn // cw):
        qa = mm(j * cw, (j + 1) * cw)
        qr = mm(rot0 + j * cw, rot0 + (j + 1) * cw)
        q_ref[:, j * cw:(j + 1) * cw] = ((qa * cos + qr * sin) * (DA_HEAD_DIM ** -0.5)).astype(BF16)
        ka = mm(d_attn + j * cw, d_attn + (j + 1) * cw)
        kr = mm(rot0 + d_attn + j * cw, rot0 + d_attn + (j + 1) * cw)
        k_ref[:, j * cw:(j + 1) * cw] = (ka * cos + kr * sin).astype(BF16)
    v_ref[...] = mm(2 * d_attn, 3 * d_attn).astype(BF16)
    for j in range(d_hy // 512):
        u_ref[:, j * 512:(j + 1) * 512] = mm(3 * d_attn + j * 512, 3 * d_attn + (j + 1) * 512)


def _inproj(x2, g, sh, sc, cos, sin, w_ext, seq, tm=512):
    t, d = x2.shape
    tpb = seq // tm
    d_attn = DA_WIDTH
    d_hy = w_ext.shape[1] - 5 * d_attn
    cw = cos.shape[1]
    kern = functools.partial(_inproj_kernel, d_attn=d_attn, d_hy=d_hy)
    return pl.pallas_call(
        kern,
        out_shape=(jax.ShapeDtypeStruct((t, d_attn), BF16), jax.ShapeDtypeStruct((t, d_attn), BF16),
                   jax.ShapeDtypeStruct((t, d_attn), BF16), jax.ShapeDtypeStruct((t, d_hy), F32)),
        grid=(t // tm,),
        in_specs=[pl.BlockSpec((tm, d), lambda i: (i, 0)),
                  pl.BlockSpec((1, d), lambda i: (0, 0)),
                  pl.BlockSpec((1, 1, d), lambda i: (i // tpb, 0, 0)),
                  pl.BlockSpec((1, 1, d), lambda i: (i // tpb, 0, 0)),
                  pl.BlockSpec((tm, cw), lambda i: (i % tpb, 0)),
                  pl.BlockSpec((tm, cw), lambda i: (i % tpb, 0)),
                  pl.BlockSpec(w_ext.shape, lambda i: (0, 0))],
        out_specs=(pl.BlockSpec((tm, d_attn), lambda i: (i, 0)), pl.BlockSpec((tm, d_attn), lambda i: (i, 0)),
                   pl.BlockSpec((tm, d_attn), lambda i: (i, 0)), pl.BlockSpec((tm, d_hy), lambda i: (i, 0))),
        compiler_params=_params(("parallel",)),
        name="inproj0",
    )(x2, g, sh, sc, cos, sin, w_ext)


def _ctxproj_kernel(x_ref, g_ref, sh_ref, sc_ref, w_ref, k_ref, v_ref):
    h = _norm_mod(x_ref[...], g_ref[...], sh_ref[0], sc_ref[0]).astype(BF16)
    n = k_ref.shape[1]
    k_ref[...] = _dot(h, w_ref[:, :n]).astype(BF16)
    v_ref[...] = _dot(h, w_ref[:, n:]).astype(BF16)


def _ctxproj(c2, g, sh, sc, w_kv, tm):
    t, d = c2.shape
    n = w_kv.shape[1] // 2
    return pl.pallas_call(
        _ctxproj_kernel,
        out_shape=(jax.ShapeDtypeStruct((t, n), BF16), jax.ShapeDtypeStruct((t, n), BF16)),
        grid=(t // tm,),
        in_specs=[pl.BlockSpec((tm, d), lambda i: (i, 0)),
                  pl.BlockSpec((1, d), lambda i: (0, 0)),
                  pl.BlockSpec((1, 1, d), lambda i: (0, 0, 0)),
                  pl.BlockSpec((1, 1, d), lambda i: (0, 0, 0)),
                  pl.BlockSpec(w_kv.shape, lambda i: (0, 0))],
        out_specs=(pl.BlockSpec((tm, n), lambda i: (i, 0)), pl.BlockSpec((tm, n), lambda i: (i, 0))),
        compiler_params=_params(("parallel",)),
        name="ctxproj",
    )(c2, g, sh, sc, w_kv)


def _attn_kernel(lam_ref, g_ref, q_ref, kt_ref, v_ref, o_ref, *, lam_init):
    lp = lam_ref[...]
    lam = (jnp.exp(jnp.sum(lp[0:1] * lp[1:2], axis=-1, keepdims=True))
           - jnp.exp(jnp.sum(lp[2:3] * lp[3:4], axis=-1, keepdims=True)) + lam_init)
    q = q_ref[...]
    lane = lax.broadcasted_iota(jnp.int32, q.shape, 1)
    zero = jnp.zeros_like(q)
    kt = kt_ref[0]
    s1 = _dot(jnp.where(lane < DA_HEAD_DIM, q, zero), kt)
    s2 = _dot(jnp.where(lane >= DA_HEAD_DIM, q, zero), kt)
    e1 = jnp.exp(s1 - jnp.max(s1, axis=-1, keepdims=True))
    e2 = jnp.exp(s2 - jnp.max(s2, axis=-1, keepdims=True))
    r1 = 1.0 / jnp.sum(e1, axis=-1, keepdims=True)
    r2 = lam / jnp.sum(e2, axis=-1, keepdims=True)
    a = (e1 * r1 - e2 * r2).astype(BF16)
    o = _dot(a, v_ref[0])
    y = o * lax.rsqrt(jnp.mean(o * o, axis=-1, keepdims=True) + EPS)
    o_ref[...] = (y * g_ref[...]) * (1.0 - lam_init)


def _attention(q, kt_all, v_all, lam_p, subln_g, seq, lam_init, tq=256):
    t = q.shape[0]
    b, _, lk = kt_all.shape
    hw = 2 * DA_HEAD_DIM
    nq = seq // tq
    kern = functools.partial(_attn_kernel, lam_init=lam_init)
    return pl.pallas_call(
        kern,
        out_shape=jax.ShapeDtypeStruct((t, DA_WIDTH), F32),
        grid=(b, DA_HEADS, nq),
        in_specs=[pl.BlockSpec(lam_p.shape, lambda bi, h, i: (0, 0)),
                  pl.BlockSpec((1, hw), lambda bi, h, i: (0, 0)),
                  pl.BlockSpec((tq, hw), lambda bi, h, i: (bi * nq + i, h)),
                  pl.BlockSpec((1, hw, lk), lambda bi, h, i: (bi, h, 0)),
                  pl.BlockSpec((1, lk, hw), lambda bi, h, i: (bi, 0, h))],
        out_specs=pl.BlockSpec((tq, hw), lambda bi, h, i: (bi * nq + i, h)),
        compiler_params=_params(("parallel", "parallel", "parallel")),
        name="diff_attn",
    )(lam_p, subln_g, q, kt_all, v_all)


def _dwconv_kernel(x_ref, w_ref, b_ref, o_ref, pad_ref, *, kw, halo, channel_major):
    s = x_ref.shape[1]
    pl_ = (kw - 1) // 2
    zeros = jnp.zeros((halo, x_ref.shape[2]), F32)
    pad_ref[0:halo, :] = zeros
    pad_ref[halo + s:halo + s + halo, :] = zeros
    pad_ref[halo:halo + s, :] = x_ref[0]
    w = w_ref[...]
    acc = jnp.zeros((s, x_ref.shape[2]), F32) + b_ref[...]
    for j in range(kw):
        off = halo - pl_ + j
        acc = acc + pad_ref[off:off + s, :] * w[j:j + 1, :]
    if channel_major:
        acc_t = acc.T
        for g in range(LANES // SUBLANES):
            for blk in range(s // LANES):
                o_ref[g, blk * SUBLANES:(blk + 1) * SUBLANES, :] = (
                    acc_t[g * SUBLANES:(g + 1) * SUBLANES, blk * LANES:(blk + 1) * LANES])
    else:
        o_ref[0] = acc


def _dwconv(x3, w, bias, channel_major=False):
    b, s, c = x3.shape
    kw = w.shape[0]
    halo = 2 * SUBLANES
    kern = functools.partial(_dwconv_kernel, kw=kw, halo=halo, channel_major=channel_major)
    if channel_major:
        gpb = LANES // SUBLANES
        out_shape = jax.ShapeDtypeStruct((c // SUBLANES, b * (s // LANES) * SUBLANES, LANES), F32)
        out_spec = pl.BlockSpec((gpb, (s // LANES) * SUBLANES, LANES), lambda bi, ci: (ci, bi, 0))
    else:
        out_shape = jax.ShapeDtypeStruct((b, s, c), F32)
        out_spec = pl.BlockSpec((1, s, LANES), lambda bi, ci: (bi, 0, ci))
    return pl.pallas_call(
        kern,
        out_shape=out_shape,
        grid=(b, c // LANES),
        in_specs=[pl.BlockSpec((1, s, LANES), lambda bi, ci: (bi, 0, ci)),
                  pl.BlockSpec((kw, LANES), lambda bi, ci: (0, ci)),
                  pl.BlockSpec((1, LANES), lambda bi, ci: (0, ci))],
        out_specs=out_spec,
        scratch_shapes=[pltpu.VMEM((s + 2 * halo, LANES), F32)],
        compiler_params=_params(("parallel", "parallel")),
        name="dwconv",
    )(x3, w, bias.reshape(1, c))


def _filter_kernel(zt_ref, t_ref, w1_ref, b1_ref, f1_ref, w2_ref, b2_ref, f2_ref, w3_ref, b3_ref, dl_ref,
                   o_ref, h_ref, *, n):
    @pl.when(pl.program_id(0) == 0)
    def _():
        h1 = jnp.sin(f1_ref[...] * (_dot3(w1_ref[...], zt_ref[...]) + b1_ref[...]))
        h_ref[...] = jnp.sin(f2_ref[...] * (_dot3(w2_ref[...], h1) + b2_ref[...]))

    h2 = h_ref[...]
    rows = o_ref.shape[0]
    fwd = _dot3(w3_ref[0], h2[:, :n]) + b3_ref[0]
    bwd = _dot3(w3_ref[1], h2[:, n:]) + b3_ref[1]
    decay = jnp.exp(-t_ref[...] * dl_ref[...])
    lane = lax.broadcasted_iota(jnp.int32, (rows, n), 1)
    kf = fwd * decay[:, :n]
    kb = jnp.where(lane == 0, 0.0, bwd * decay[:, n:])
    inv = 1.0 / (jnp.sum(jnp.abs(kf), axis=-1, keepdims=True) + jnp.sum(jnp.abs(kb), axis=-1, keepdims=True))
    kf = kf * inv
    kb = kb * inv
    o_ref[:, 0:LANES] = kb[:, n - LANES:]
    o_ref[:, LANES:LANES + n] = kf
    o_ref[:, LANES + n:] = kb


def _hyena_filters(n, w1, b1, fr1, w2, b2, fr2, w3, b3):
    emb, ffn = w1.shape
    c = w3.shape[1] // (2 * HY_ORDER)
    bands = (emb - 1) // 2
    t = jnp.linspace(0.0, 1.0, n, dtype=F32)[:, None]
    ang = (2.0 * math.pi / n) * jnp.arange(n, dtype=F32)[:, None] * jnp.linspace(1e-4, bands - 1, bands, dtype=F32)[None, :]
    z = jnp.concatenate([t, jnp.cos(ang), -jnp.sin(ang)], axis=-1)
    rev = (n - jnp.arange(n)) % n
    z2 = jnp.concatenate([z, z[rev]], axis=0)
    t2 = jnp.concatenate([t, t[rev]], axis=0).reshape(1, 2 * n)
    emb_p = ((emb + SUBLANES - 1) // SUBLANES) * SUBLANES
    zt = jnp.zeros((emb_p, 2 * n), F32).at[:emb].set(z2.T)
    w1t = jnp.zeros((ffn, emb_p), F32).at[:, :emb].set(w1.T)
    deltas = jnp.abs(jnp.linspace(HY_MIN_DECAY, HY_MAX_DECAY, c, dtype=F32))
    w3t = w3.T.reshape(HY_ORDER, 2, c, ffn).transpose(1, 0, 2, 3).reshape(2, HY_ORDER * c, ffn)
    b3t = b3.reshape(HY_ORDER, 2, c).transpose(1, 0, 2).reshape(2, HY_ORDER * c, 1)
    dl = jnp.tile(deltas, HY_ORDER).reshape(HY_ORDER * c, 1)
    rows = LANES
    kern = functools.partial(_filter_kernel, n=n)
    col = lambda v: v.reshape(ffn, 1)
    return pl.pallas_call(
        kern,
        out_shape=jax.ShapeDtypeStruct((HY_ORDER * c, LANES + 2 * n), F32),
        grid=(HY_ORDER * c // rows,),
        in_specs=[pl.BlockSpec(zt.shape, lambda i: (0, 0)),
                  pl.BlockSpec(t2.shape, lambda i: (0, 0)),
                  pl.BlockSpec(w1t.shape, lambda i: (0, 0)),
                  pl.BlockSpec((ffn, 1), lambda i: (0, 0)),
                  pl.BlockSpec((ffn, 1), lambda i: (0, 0)),
                  pl.BlockSpec((ffn, ffn), lambda i: (0, 0)),
                  pl.BlockSpec((ffn, 1), lambda i: (0, 0)),
                  pl.BlockSpec((ffn, 1), lambda i: (0, 0)),
                  pl.BlockSpec((2, rows, ffn), lambda i: (0, i, 0)),
                  pl.BlockSpec((2, rows, 1), lambda i: (0, i, 0)),
                  pl.BlockSpec((rows, 1), lambda i: (i, 0))],
        out_specs=pl.BlockSpec((rows, LANES + 2 * n), lambda i: (i, 0)),
        scratch_shapes=[pltpu.VMEM((ffn, 2 * n), F32)],
        compiler_params=_params(("arbitrary",)),
        name="hyena_filters",
    )(zt, t2, w1t, col(b1), col(fr1), w2.T, col(b2), col(fr2), w3t, b3t, dl)


def _longconv_kernel(z_ref, gate_ref, kc_ref, bias_ref, o_ref, r_ref, zs_ref, acc_ref, *, nb, bsz, cb):
    p = LC_P
    n2 = 2 * nb * p

    def channel(c, carry):
        w = min(LC_W, n2)
        for ci in range(n2 // w):
            win = kc_ref[pl.ds(c, 1), ci * w:ci * w + w + LANES]
            rolled = pltpu.roll(jnp.broadcast_to(win, (p, w + LANES)), 0, 1, stride=1, stride_axis=0)
            r_ref[:, ci * w:(ci + 1) * w] = rolled[:, LANES:].astype(BF16)
        for s1 in range(nb):
            zs_ref[s1 * bsz:(s1 + 1) * bsz, :] = z_ref[0, pl.ds(c + cb * s1, bsz, stride=cb * nb), :]
        acc_ref[...] = jnp.zeros_like(acc_ref)
        for pi in range(nb):
            d = -nb + 2 * pi
            off = (d % (2 * nb)) * p
            lo = max(0, -d - 1)
            hi = min(nb, nb - d)
            lhs = zs_ref[lo * bsz:hi * bsz, :].astype(BF16)
            out = _dot(lhs, r_ref[:, off:off + 2 * p])
            for k in range(2):
                dk = d + k
                a0, a1 = max(0, -dk), min(nb, nb - dk)
                if a1 <= a0:
                    continue
                acc_ref[(a0 + dk) * bsz:(a1 + dk) * bsz, :] += out[(a0 - lo) * bsz:(a1 - lo) * bsz, k * p:(k + 1) * p]
        bias = bias_ref[c]
        for s1 in range(nb):
            rows = slice(s1 * bsz, (s1 + 1) * bsz)
            where = pl.ds(c + cb * s1, bsz, stride=cb * nb)
            o_ref[0, where, :] = gate_ref[0, where, :] * (acc_ref[rows, :] + zs_ref[rows, :] * bias)
        return carry

    lax.fori_loop(0, cb, channel, 0)


def _longconv(z_cm, z_g0, gate_cm, gate_g0, kc_ext, bias_cm, order, groups, nb, bsz):
    cb = SUBLANES
    rows, p = z_cm.shape[1:]
    kern = functools.partial(_longconv_kernel, nb=nb, bsz=bsz, cb=cb)
    return pl.pallas_call(
        kern,
        out_shape=jax.ShapeDtypeStruct((groups, rows, p), F32),
        grid=(groups,),
        in_specs=[pl.BlockSpec((1, rows, p), lambda i: (z_g0 + i, 0, 0)),
                  pl.BlockSpec((1, rows, p), lambda i: (gate_g0 + i, 0, 0)),
                  pl.BlockSpec((cb, kc_ext.shape[1]), lambda i: (order * groups + i, 0)),
                  pl.BlockSpec((cb, 1, p), lambda i: (order * groups + i, 0, 0))],
        out_specs=pl.BlockSpec((1, rows, p), lambda i: (i, 0, 0)),
        scratch_shapes=[pltpu.VMEM((p, 2 * nb * p), BF16), pltpu.VMEM((nb * bsz, p), F32),
                        pltpu.VMEM((nb * bsz, p), F32)],
        compiler_params=_params(("parallel",)),
        name="hyena_longconv",
    )(z_cm, gate_cm, kc_ext, bias_cm)


def _outproj_kernel(x_ref, g1_ref, a_ref, b_ref, w_ref, o_ref):
    da = a_ref.shape[1]
    ya = _dot(a_ref[...].astype(BF16), w_ref[:da, :])
    groups, rows, p = b_ref.shape
    parts = []
    for blk in range(rows // SUBLANES):
        cm = b_ref[:, blk * SUBLANES:(blk + 1) * SUBLANES, :].reshape(groups * SUBLANES, p)
        parts.append(_dot(cm.T.astype(BF16), w_ref[da:, :]))
    y = ya + jnp.concatenate(parts, axis=0)
    o_ref[...] = x_ref[...] + g1_ref[0] * y


def _outproj(x2, g1, oa, ob, w, seq, tm=512):
    t, d = x2.shape
    tpb = seq // tm
    return pl.pallas_call(
        _outproj_kernel,
        out_shape=jax.ShapeDtypeStruct((t, d), F32),
        grid=(t // tm,),
        in_specs=[pl.BlockSpec((tm, d), lambda i: (i, 0)),
                  pl.BlockSpec((1, 1, d), lambda i: (i // tpb, 0, 0)),
                  pl.BlockSpec((tm, oa.shape[1]), lambda i: (i, 0)),
                  pl.BlockSpec((ob.shape[0], (tm // LANES) * SUBLANES, LANES), lambda i: (0, i, 0)),
                  pl.BlockSpec(w.shape, lambda i: (0, 0))],
        out_specs=pl.BlockSpec((tm, d), lambda i: (i, 0)),
        compiler_params=_params(("parallel",)),
        name="outproj0",
    )(x2, g1, oa, ob, w)


def _confin_kernel(x_ref, g_ref, sh_ref, sc_ref, w_ref, b_ref, o_ref):
    h = _norm_mod(x_ref[...], g_ref[...], sh_ref[0], sc_ref[0]).astype(BF16)
    n = o_ref.shape[1]
    a = _dot(h, w_ref[:, :n]) + b_ref[:, :n]
    gt = _dot(h, w_ref[:, n:]) + b_ref[:, n:]
    o_ref[...] = a * jax.nn.sigmoid(gt)


def _confin(x2, g, sh, sc, w1, b1, seq, tm=512):
    t, d = x2.shape
    n = w1.shape[1] // 2
    tpb = seq // tm
    return pl.pallas_call(
        _confin_kernel,
        out_shape=jax.ShapeDtypeStruct((t, n), F32),
        grid=(t // tm,),
        in_specs=[pl.BlockSpec((tm, d), lambda i: (i, 0)),
                  pl.BlockSpec((1, d), lambda i: (0, 0)),
                  pl.BlockSpec((1, 1, d), lambda i: (i // tpb, 0, 0)),
                  pl.BlockSpec((1, 1, d), lambda i: (i // tpb, 0, 0)),
                  pl.BlockSpec(w1.shape, lambda i: (0, 0)),
                  pl.BlockSpec((1, 2 * n), lambda i: (0, 0))],
        out_specs=pl.BlockSpec((tm, n), lambda i: (i, 0)),
        compiler_params=_params(("parallel",)),
        name="conformer_in",
    )(x2, g, sh, sc, w1, b1.reshape(1, 2 * n))


def _confout_kernel(x_ref, g1_ref, a_ref, lg_ref, lb_ref, w_ref, b_ref, o_ref):
    a = a_ref[...]
    mu = jnp.mean(a, axis=-1, keepdims=True)
    ac = a - mu
    var = jnp.mean(ac * ac, axis=-1, keepdims=True)
    y = ac * lax.rsqrt(var + EPS) * lg_ref[...] + lb_ref[...]
    y = y * jax.nn.sigmoid(y)
    o_ref[...] = x_ref[...] + g1_ref[0] * (_dot(y.astype(BF16), w_ref[...]) + b_ref[...])


def _confout(x2, g1, a2, ln_g, ln_b, w2, b2, seq, tm=512):
    t, d = x2.shape
    n = a2.shape[1]
    tpb = seq // tm
    return pl.pallas_call(
        _confout_kernel,
        out_shape=jax.ShapeDtypeStruct((t, d), F32),
        grid=(t // tm,),
        in_specs=[pl.BlockSpec((tm, d), lambda i: (i, 0)),
                  pl.BlockSpec((1, 1, d), lambda i: (i // tpb, 0, 0)),
                  pl.BlockSpec((tm, n), lambda i: (i, 0)),
                  pl.BlockSpec((1, n), lambda i: (0, 0)),
                  pl.BlockSpec((1, n), lambda i: (0, 0)),
                  pl.BlockSpec(w2.shape, lambda i: (0, 0)),
                  pl.BlockSpec((1, d), lambda i: (0, 0))],
        out_specs=pl.BlockSpec((tm, d), lambda i: (i, 0)),
        compiler_params=_params(("parallel",)),
        name="conformer_out",
    )(x2, g1, a2, ln_g.reshape(1, n), ln_b.reshape(1, n), w2, b2.reshape(1, d))


def _router_kernel(x_ref, g_ref, sh_ref, sc_ref, wr_ref, br_ref, h_ref, route_ref, cnt_ref, run_ref):
    i = pl.program_id(0)

    @pl.when(i == 0)
    def _():
        run_ref[...] = jnp.zeros_like(run_ref)

    h = _norm_mod(x_ref[...], g_ref[...], sh_ref[0], sc_ref[0])
    hh, hl = _split(h)
    h_ref[...] = hh
    logits = _dot(hh, wr_ref[0]) + _dot(hl, wr_ref[0]) + _dot(hh, wr_ref[1]) + br_ref[...]
    tm = logits.shape[0]
    lane = lax.broadcasted_iota(jnp.int32, (tm, LANES), 1)
    ninf = jnp.float32(-jnp.inf)

    def first_argmax(v, m):
        return jnp.min(jnp.where(v == m, lane, LANES), axis=-1, keepdims=True)

    gl = jnp.where(lane < N_GROUPS, logits, ninf)
    gmax = jnp.max(gl, axis=-1, keepdims=True)
    g_w = 1.0 / jnp.sum(jnp.exp(gl - gmax), axis=-1, keepdims=True)
    g_idx = first_argmax(gl, gmax)
    e_lo = ROUTE_LANE0 + EXPERTS_PER_GROUP * g_idx
    el = jnp.where((lane >= e_lo) & (lane < e_lo + EXPERTS_PER_GROUP), logits, ninf)
    m1 = jnp.max(el, axis=-1, keepdims=True)
    esum = jnp.sum(jnp.exp(el - m1), axis=-1, keepdims=True)
    i1 = first_argmax(el, m1)
    el2 = jnp.where(lane == i1, ninf, el)
    m2 = jnp.max(el2, axis=-1, keepdims=True)
    i2 = first_argmax(el2, m2)
    p1 = 1.0 / esum
    p2 = jnp.exp(m2 - m1) / esum
    w1 = g_w * (p1 / (p1 + p2))
    w2 = g_w * (p2 / (p1 + p2))

    oh = jnp.where((lane == i1) | (lane == i2), 1.0, 0.0)
    r_i = lax.broadcasted_iota(jnp.int32, (tm, tm), 0)
    c_i = lax.broadcasted_iota(jnp.int32, (tm, tm), 1)
    tri = jnp.where(c_i < r_i, 1.0, 0.0).astype(BF16)
    before = _dot(tri, oh.astype(BF16)) + run_ref[...]
    rank1 = jnp.sum(jnp.where(lane == i1, before, 0.0), axis=-1, keepdims=True)
    rank2 = jnp.sum(jnp.where(lane == i2, before, 0.0), axis=-1, keepdims=True)
    run_ref[...] = run_ref[...] + jnp.sum(oh, axis=0, keepdims=True)
    cnt_ref[...] = run_ref[...]

    e1 = (i1 - ROUTE_LANE0).astype(F32)
    e2 = (i2 - ROUTE_LANE0).astype(F32)
    vals = (e1, e2, rank1, rank2, w1, w2)
    out = jnp.zeros((tm, LANES), F32)
    for k, v in enumerate(vals):
        out = jnp.where(lane == k, v, out)
    route_ref[...] = out


def _router(x2, g, sh, sc, wr, br, seq, tm=512):
    t, d = x2.shape
    tpb = seq // tm
    return pl.pallas_call(
        _router_kernel,
        out_shape=(jax.ShapeDtypeStruct((t, d), BF16), jax.ShapeDtypeStruct((t, LANES), F32),
                   jax.ShapeDtypeStruct((1, LANES), F32)),
        grid=(t // tm,),
        in_specs=[pl.BlockSpec((tm, d), lambda i: (i, 0)),
                  pl.BlockSpec((1, d), lambda i: (0, 0)),
                  pl.BlockSpec((1, 1, d), lambda i: (i // tpb, 0, 0)),
                  pl.BlockSpec((1, 1, d), lambda i: (i // tpb, 0, 0)),
                  pl.BlockSpec(wr.shape, lambda i: (0, 0, 0)),
                  pl.BlockSpec((1, LANES), lambda i: (0, 0))],
        out_specs=(pl.BlockSpec((tm, d), lambda i: (i, 0)), pl.BlockSpec((tm, LANES), lambda i: (i, 0)),
                   pl.BlockSpec((1, LANES), lambda i: (0, 0))),
        scratch_shapes=[pltpu.VMEM((1, LANES), F32)],
        compiler_params=_params(("arbitrary",)),
        name="moe_router",
    )(x2, g, sh, sc, wr, br)


def _expert_kernel(te_ref, nv_ref, x_ref, wg_ref, wu_ref, wd_ref, o_ref, wg_s, wu_s, wd_s):
    i = pl.program_id(0)

    @pl.when((i == 0) | (te_ref[i] != te_ref[jnp.maximum(i - 1, 0)]))
    def _():
        wg_s[...] = wg_ref[0].astype(BF16)
        wu_s[...] = wu_ref[0].astype(BF16)
        wd_s[...] = wd_ref[0].astype(BF16)

    @pl.when(i < nv_ref[0])
    def _():
        x = x_ref[...]
        a = _dot(x, wg_s[...])
        u = _dot(x, wu_s[...])
        he = (a * jax.nn.sigmoid(a)) * u
        o_ref[...] = _dot(he.astype(BF16), wd_s[...]).astype(o_ref.dtype)


def _experts(xs, tile_expert, n_valid, wg, wu, wd, layer):
    r, d = xs.shape
    de = wg.shape[2]
    nt = r // MOE_TILE
    row = lambda i, te, nv: (jnp.minimum(i, nv[0] - 1), 0)
    wsel = lambda i, te, nv: (layer * N_EXPERTS + te[i], 0, 0)
    return pl.pallas_call(
        _expert_kernel,
        out_shape=jax.ShapeDtypeStruct((r, d), BF16),
        grid_spec=pltpu.PrefetchScalarGridSpec(
            num_scalar_prefetch=2,
            grid=(nt,),
            in_specs=[pl.BlockSpec((MOE_TILE, d), row),
                      pl.BlockSpec((1, d, de), wsel),
                      pl.BlockSpec((1, d, de), wsel),
                      pl.BlockSpec((1, de, d), wsel)],
            out_specs=pl.BlockSpec((MOE_TILE, d), row),
            scratch_shapes=[pltpu.VMEM((d, de), BF16), pltpu.VMEM((d, de), BF16), pltpu.VMEM((de, d), BF16)]),
        compiler_params=_params(("arbitrary",)),
        name="moe_experts",
    )(tile_expert, n_valid, xs, wg, wu, wd)


def _combine_kernel(x_ref, g2_ref, route_ref, y1_ref, y2_ref, fg_ref, o_ref, *, final):
    r = route_ref[...]
    w1 = r[:, 4:5]
    w2 = r[:, 5:6]
    x = x_ref[...] + g2_ref[0] * (w1 * y1_ref[...].astype(F32) + w2 * y2_ref[...].astype(F32))
    if final:
        x = (x * lax.rsqrt(jnp.mean(x * x, axis=-1, keepdims=True) + EPS)) * fg_ref[...]
    o_ref[...] = x


def _combine(x2, g2, route, y12, final_g, seq, final, tm=512):
    t, d = x2.shape
    tpb = seq // tm
    nt = t // tm
    kern = functools.partial(_combine_kernel, final=final)
    return pl.pallas_call(
        kern,
        out_shape=jax.ShapeDtypeStruct((t, d), F32),
        grid=(t // tm,),
        in_specs=[pl.BlockSpec((tm, d), lambda i: (i, 0)),
                  pl.BlockSpec((1, 1, d), lambda i: (i // tpb, 0, 0)),
                  pl.BlockSpec((tm, LANES), lambda i: (i, 0)),
                  pl.BlockSpec((tm, d), lambda i: (i, 0)),
                  pl.BlockSpec((tm, d), lambda i: (nt + i, 0)),
                  pl.BlockSpec((1, d), lambda i: (0, 0))],
        out_specs=pl.BlockSpec((tm, d), lambda i: (i, 0)),
        compiler_params=_params(("parallel",)),
        name="moe_combine",
    )(x2, g2, route, y12, y12, final_g)


def _moe(x2, g, sh, sc, g2, wg_r, bg_r, we_r, be_r, w_gate, w_up, w_down, layer, final_g, seq, final):
    t, d = x2.shape
    wr = jnp.zeros((d, LANES), F32).at[:, :N_GROUPS].set(wg_r).at[:, ROUTE_LANE0:ROUTE_LANE0 + N_EXPERTS].set(we_r)
    wr_hi = wr.astype(BF16)
    wr_lo = (wr - wr_hi.astype(F32)).astype(BF16)
    br = jnp.zeros((1, LANES), F32).at[0, :N_GROUPS].set(bg_r).at[0, ROUTE_LANE0:ROUTE_LANE0 + N_EXPERTS].set(be_r)
    h, route, cnt = _router(x2, g, sh, sc, jnp.stack([wr_hi, wr_lo]), br, seq)

    counts = cnt[0, ROUTE_LANE0:ROUTE_LANE0 + N_EXPERTS].astype(jnp.int32)
    tiles = (counts + MOE_TILE - 1) // MOE_TILE
    tile_end = jnp.cumsum(tiles)
    offs = (tile_end - tiles) * MOE_TILE
    nt = (2 * t) // MOE_TILE + N_EXPERTS
    tile_id = jnp.minimum(jnp.arange(nt, dtype=jnp.int32), tile_end[-1] - 1)
    tile_expert = jnp.sum((tile_id[:, None] >= tile_end[None, :]).astype(jnp.int32), axis=1)
    n_valid = tile_end[-1:].astype(jnp.int32)
    e12 = route[:, 0:2].astype(jnp.int32)
    offs12 = jnp.sum(jnp.where(e12[:, :, None] == jnp.arange(N_EXPERTS)[None, None, :], offs[None, None, :], 0), axis=-1)
    pos = (offs12 + route[:, 2:4].astype(jnp.int32)).T.reshape(-1)
    tok = jnp.tile(jnp.arange(t, dtype=jnp.int32), 2)
    sorted_tok = (jnp.arange(nt * MOE_TILE, dtype=jnp.int32) % t).at[pos].set(
        tok, unique_indices=True, indices_are_sorted=False, mode="promise_in_bounds")
    xs = h.at[sorted_tok].get(mode="promise_in_bounds")
    ys = _experts(xs, tile_expert, n_valid, w_gate, w_up, w_down, layer)
    y12 = ys.at[pos].get(mode="promise_in_bounds")
    return _combine(x2, g2, route, y12, final_g, seq, final)


def _rope_tables(seq, width):
    hd = DA_HEAD_DIM
    half = hd // 2
    quarter = half // 2
    pos = jnp.arange(seq)
    row = (pos // GRID_W).astype(F32)
    col = (pos % GRID_W).astype(F32)
    inv = ROPE_THETA ** (-jnp.arange(0, half, 2, dtype=F32) / half)
    i = jnp.arange(hd)
    p = jnp.where((i < half)[None, :], row[:, None], col[:, None])
    ang = p * inv[i % quarter][None, :]
    sign = jnp.where((i % half) < quarter, -1.0, 1.0)[None, :]
    reps = width // hd
    return jnp.tile(jnp.cos(ang), (1, reps)), jnp.tile(jnp.sin(ang) * sign, (1, reps))


def _rope_partner(width):
    i = jnp.arange(width)
    quarter = DA_HEAD_DIM // 4
    return jnp.where((i % (2 * quarter)) < quarter, i + quarter, i - quarter)


def kernel(x, c, ctx, c_ctx, ada_w, ada_b, norm1_g, norm2_g, final_g, w_in0, w_out0, lam_q1, lam_k1, lam_q2, lam_k2, subln_g, hy_short_w, hy_short_b, hy_w1, hy_b1, hy_fr1, hy_w2, hy_b2, hy_fr2, hy_w3, hy_b3, hy_bias, cv_w1, cv_b1, cv_dw_w, cv_dw_b, cv_ln_g, cv_ln_b, cv_w2, cv_b2, moe_wg, moe_bg, moe_we, moe_be, moe_w_gate, moe_w_up, moe_w_down):
    bsz, seq, d = x.shape
    lctx = ctx.shape[1]
    depth = ada_w.shape[0]
    t = bsz * seq
    hyw = d - DA_WIDTH
    x2 = x.reshape(t, d)

    rows = ((bsz + 1 + SUBLANES - 1) // SUBLANES) * SUBLANES
    cs = jnp.zeros((rows, d), F32).at[:bsz].set(c).at[bsz].set(c_ctx)
    mods = _ada(cs, ada_w, ada_b)

    def tok_mod(i, k):
        return mods[i, :bsz, k * d:(k + 1) * d].reshape(bsz, 1, d)

    def ctx_mod(i, k):
        return mods[i, bsz:bsz + 1, k * d:(k + 1) * d].reshape(1, 1, d)

    for i in range(depth):
        j = i // 2
        g_n1 = norm1_g[i].reshape(1, d)
        if i % 2 == 0:
            assert not any(m % 2 == 0 for m in range(i + 1, depth)), "context-stream update is not implemented"
            lam_init = 0.8 - 0.6 * math.exp(-0.3 * i)
            w_in = w_in0[j]
            part = _rope_partner(2 * DA_WIDTH)
            w_ext = jnp.concatenate([w_in, w_in[:, :2 * DA_WIDTH][:, part]], axis=1).astype(BF16)
            cw = 2 * LANES
            cos, sin = _rope_tables(seq, cw)
            q, k, v, u = _inproj(x2, g_n1, tok_mod(i, 0), tok_mod(i, 1), cos, sin, w_ext, seq)
            kc, vc = _ctxproj(ctx.reshape(bsz * lctx, d), g_n1, ctx_mod(i, 0), ctx_mod(i, 1),
                              w_in[:, DA_WIDTH:3 * DA_WIDTH].astype(BF16), lctx)
            lam_p = jnp.stack([lam_q1[j], lam_k1[j], lam_q2[j], lam_k2[j]])
            k_all = jnp.concatenate([kc.reshape(bsz, lctx, DA_WIDTH), k.reshape(bsz, seq, DA_WIDTH)], axis=1)
            v_all = jnp.concatenate([vc.reshape(bsz, lctx, DA_WIDTH), v.reshape(bsz, seq, DA_WIDTH)], axis=1)
            o_a = _attention(q, jnp.swapaxes(k_all, 1, 2), v_all, lam_p, subln_g[j].reshape(1, -1), seq, lam_init)

            nb = seq // LC_P
            groups = hyw // SUBLANES
            ucm = _dwconv(u.reshape(bsz, seq, 3 * hyw), hy_short_w[j], hy_short_b[j], channel_major=True)
            kc_ext = _hyena_filters(seq, hy_w1[j], hy_b1[j], hy_fr1[j], hy_w2[j], hy_b2[j], hy_fr2[j],
                                    hy_w3[j], hy_b3[j])
            bias_cm = jnp.broadcast_to(hy_bias[j].reshape(HY_ORDER * hyw, 1, 1), (HY_ORDER * hyw, 1, LC_P))
            z1 = _longconv(ucm, 0, ucm, groups, kc_ext, bias_cm, 0, groups, nb, bsz)
            z2 = _longconv(z1, 0, ucm, 2 * groups, kc_ext, bias_cm, 1, groups, nb, bsz)
            x2 = _outproj(x2, tok_mod(i, 2), o_a, z2, w_out0[j].astype(BF16), seq)
        else:
            a = _confin(x2, g_n1, tok_mod(i, 0), tok_mod(i, 1), cv_w1[j].astype(BF16), cv_b1[j], seq)
            a = _dwconv(a.reshape(bsz, seq, -1), cv_dw_w[j], cv_dw_b[j]).reshape(t, -1)
            x2 = _confout(x2, tok_mod(i, 2), a, cv_ln_g[j], cv_ln_b[j], cv_w2[j].astype(BF16), cv_b2[j], seq)
        x2 = _moe(x2, norm2_g[i].reshape(1, d), tok_mod(i, 3), tok_mod(i, 4), tok_mod(i, 5),
                  moe_wg[i], moe_bg[i], moe_we[i], moe_be[i],
                  moe_w_gate.reshape((-1,) + moe_w_gate.shape[2:]), moe_w_up.reshape((-1,) + moe_w_up.shape[2:]),
                  moe_w_down.reshape((-1,) + moe_w_down.shape[2:]), i,
                  final_g.reshape(1, d), seq, final=(i == depth - 1))
    return x2.reshape(bsz, seq, d)
```

```python
import functools
import math

import jax
import jax.numpy as jnp
from jax import lax
from jax.experimental import pallas as pl
from jax.experimental.pallas import tpu as pltpu

F32 = jnp.float32
BF16 = jnp.bfloat16

GRID_W = 64
DA_HEADS = 4
DA_HEAD_DIM = 64
DA_WIDTH = DA_HEADS * 2 * DA_HEAD_DIM
HY_ORDER = 2
HY_TARGET = 1e-2
HY_MIN_DECAY = math.log(HY_TARGET) / 0.3
HY_MAX_DECAY = math.log(HY_TARGET) / 1.5
N_GROUPS = 4
EXPERTS_PER_GROUP = 8
N_EXPERTS = N_GROUPS * EXPERTS_PER_GROUP
ROPE_THETA = 10000.0
EPS = 1e-6

LANES = 128
SUBLANES = 8
VMEM_LIMIT = 52 * 1024 * 1024
ROUTE_LANE0 = N_GROUPS
MOE_TILE = 256
LC_P = 128
LC_W = 1024


def _params(sem):
    return pltpu.CompilerParams(dimension_semantics=sem, vmem_limit_bytes=VMEM_LIMIT)


def _split(a):
    hi = a.astype(BF16)
    lo = (a - hi.astype(F32)).astype(BF16)
    return hi, lo


def _dot(a, b):
    return jnp.dot(a, b, preferred_element_type=F32)


def _dot3(a, b):
    ah, al = _split(a)
    bh, bl = _split(b)
    return _dot(ah, bh) + _dot(al, bh) + _dot(ah, bl)


def _norm_mod(x, g, sh, sc):
    y = x * lax.rsqrt(jnp.mean(x * x, axis=-1, keepdims=True) + EPS)
    return (y * g) * (1.0 + sc) + sh


def _ada_kernel(c_ref, w_ref, b_ref, o_ref):
    c = c_ref[...]
    s = c * jax.nn.sigmoid(c)
    o_ref[0] = _dot3(s, w_ref[0]) + b_ref[0]


def _ada(cs, ada_w, ada_b):
    depth, d, n6 = ada_w.shape
    rows = cs.shape[0]
    tn = 1536
    return pl.pallas_call(
        _ada_kernel,
        out_shape=jax.ShapeDtypeStruct((depth, rows, n6), F32),
        grid=(depth, n6 // tn),
        in_specs=[pl.BlockSpec((rows, d), lambda l, j: (0, 0)),
                  pl.BlockSpec((1, d, tn), lambda l, j: (l, 0, j)),
                  pl.BlockSpec((1, 1, tn), lambda l, j: (l, 0, j))],
        out_specs=pl.BlockSpec((1, rows, tn), lambda l, j: (l, 0, j)),
        compiler_params=_params(("parallel", "parallel")),
        name="adaln",
    )(cs, ada_w, ada_b.reshape(depth, 1, n6))


def _inproj_kernel(x_ref, g_ref, sh_ref, sc_ref, cos_ref, sin_ref, w_ref,
                   q_ref, k_ref, v_ref, u_ref, *, d_attn, d_hy):
    h = _norm_mod(x_ref[...], g_ref[...], sh_ref[0], sc_ref[0]).astype(BF16)
    cos = cos_ref[...]
    sin = sin_ref[...]
    rot0 = 3 * d_attn + d_hy
    cw = cos.shape[1]

    def mm(c0, c1):
        return _dot(h, w_ref[:, c0:c1])

    for j in range(d_attn // cw):
        qa = mm(j * cw, (j + 1) * cw)
        qr = mm(rot0 + j * cw, rot0 + (j + 1) * cw)
        q_ref[:, j * cw:(j + 1) * cw] = ((qa * cos + qr * sin) * (DA_HEAD_DIM ** -0.5)).astype(BF16)
        ka = mm(d_attn + j * cw, d_attn + (j + 1) * cw)
        kr = mm(rot0 + d_attn + j * cw, rot0 + d_attn + (j + 1) * cw)
        k_ref[:, j * cw:(j + 1) * cw] = (ka * cos + kr * sin).astype(BF16)
    v_ref[...] = mm(2 * d_attn, 3 * d_attn).astype(BF16)
    for j in range(d_hy // 512):
        u_ref[:, j * 512:(j + 1) * 512] = mm(3 * d_attn + j * 512, 3 * d_attn + (j + 1) * 512)


def _inproj(x2, g, sh, sc, cos, sin, w_ext, seq, tm=512):
    t, d = x2.shape
    tpb = seq // tm
    d_attn = DA_WIDTH
    d_hy = w_ext.shape[1] - 5 * d_attn
    cw = cos.shape[1]
    kern = functools.partial(_inproj_kernel, d_attn=d_attn, d_hy=d_hy)
    return pl.pallas_call(
        kern,
        out_shape=(jax.ShapeDtypeStruct((t, d_attn), BF16), jax.ShapeDtypeStruct((t, d_attn), BF16),
                   jax.ShapeDtypeStruct((t, d_attn), BF16), jax.ShapeDtypeStruct((t, d_hy), F32)),
        grid=(t // tm,),
        in_specs=[pl.BlockSpec((tm, d), lambda i: (i, 0)),
                  pl.BlockSpec((1, d), lambda i: (0, 0)),
                  pl.BlockSpec((1, 1, d), lambda i: (i // tpb, 0, 0)),
                  pl.BlockSpec((1, 1, d), lambda i: (i // tpb, 0, 0)),
                  pl.BlockSpec((tm, cw), lambda i: (i % tpb, 0)),
                  pl.BlockSpec((tm, cw), lambda i: (i % tpb, 0)),
                  pl.BlockSpec(w_ext.shape, lambda i: (0, 0))],
        out_specs=(pl.BlockSpec((tm, d_attn), lambda i: (i, 0)), pl.BlockSpec((tm, d_attn), lambda i: (i, 0)),
                   pl.BlockSpec((tm, d_attn), lambda i: (i, 0)), pl.BlockSpec((tm, d_hy), lambda i: (i, 0))),
        compiler_params=_params(("parallel",)),
        name="inproj0",
    )(x2, g, sh, sc, cos, sin, w_ext)


def _ctxproj_kernel(x_ref, g_ref, sh_ref, sc_ref, w_ref, k_ref, v_ref):
    h = _norm_mod(x_ref[...], g_ref[...], sh_ref[0], sc_ref[0]).astype(BF16)
    n = k_ref.shape[1]
    k_ref[...] = _dot(h, w_ref[:, :n]).astype(BF16)
    v_ref[...] = _dot(h, w_ref[:, n:]).astype(BF16)


def _ctxproj(c2, g, sh, sc, w_kv, tm):
    t, d = c2.shape
    n = w_kv.shape[1] // 2
    return pl.pallas_call(
        _ctxproj_kernel,
        out_shape=(jax.ShapeDtypeStruct((t, n), BF16), jax.ShapeDtypeStruct((t, n), BF16)),
        grid=(t // tm,),
        in_specs=[pl.BlockSpec((tm, d), lambda i: (i, 0)),
                  pl.BlockSpec((1, d), lambda i: (0, 0)),
                  pl.BlockSpec((1, 1, d), lambda i: (0, 0, 0)),
                  pl.BlockSpec((1, 1, d), lambda i: (0, 0, 0)),
                  pl.BlockSpec(w_kv.shape, lambda i: (0, 0))],
        out_specs=(pl.BlockSpec((tm, n), lambda i: (i, 0)), pl.BlockSpec((tm, n), lambda i: (i, 0))),
        compiler_params=_params(("parallel",)),
        name="ctxproj",
    )(c2, g, sh, sc, w_kv)


def _attn_kernel(lam_ref, g_ref, q_ref, kt_ref, v_ref, o_ref, *, lam_init):
    lp = lam_ref[...]
    lam = (jnp.exp(jnp.sum(lp[0:1] * lp[1:2], axis=-1, keepdims=True))
           - jnp.exp(jnp.sum(lp[2:3] * lp[3:4], axis=-1, keepdims=True)) + lam_init)
    q = q_ref[...]
    lane = lax.broadcasted_iota(jnp.int32, q.shape, 1)
    zero = jnp.zeros_like(q)
    kt = kt_ref[0]
    s1 = _dot(jnp.where(lane < DA_HEAD_DIM, q, zero), kt)
    s2 = _dot(jnp.where(lane >= DA_HEAD_DIM, q, zero), kt)
    e1 = jnp.exp(s1 - jnp.max(s1, axis=-1, keepdims=True))
    e2 = jnp.exp(s2 - jnp.max(s2, axis=-1, keepdims=True))
    r1 = 1.0 / jnp.sum(e1, axis=-1, keepdims=True)
    r2 = lam / jnp.sum(e2, axis=-1, keepdims=True)
    a = (e1 * r1 - e2 * r2).astype(BF16)
    o = _dot(a, v_ref[0])
    y = o * lax.rsqrt(jnp.mean(o * o, axis=-1, keepdims=True) + EPS)
    o_ref[...] = (y * g_ref[...]) * (1.0 - lam_init)


def _attention(q, kt_all, v_all, lam_p, subln_g, seq, lam_init, tq=256):
    t = q.shape[0]
    b, _, lk = kt_all.shape
    hw = 2 * DA_HEAD_DIM
    nq = seq // tq
    kern = functools.partial(_attn_kernel, lam_init=lam_init)
    return pl.pallas_call(
        kern,
        out_shape=jax.ShapeDtypeStruct((t, DA_WIDTH), F32),
        grid=(b, DA_HEADS, nq),
        in_specs=[pl.BlockSpec(lam_p.shape, lambda bi, h, i: (0, 0)),
                  pl.BlockSpec((1, hw), lambda bi, h, i: (0, 0)),
                  pl.BlockSpec((tq, hw), lambda bi, h, i: (bi * nq + i, h)),
                  pl.BlockSpec((1, hw, lk), lambda bi, h, i: (bi, h, 0)),
                  pl.BlockSpec((1, lk, hw), lambda bi, h, i: (bi, 0, h))],
        out_specs=pl.BlockSpec((tq, hw), lambda bi, h, i: (bi * nq + i, h)),
        compiler_params=_params(("parallel", "parallel", "parallel")),
        name="diff_attn",
    )(lam_p, subln_g, q, kt_all, v_all)


def _dwconv_kernel(x_ref, w_ref, b_ref, o_ref, pad_ref, *, kw, halo, channel_major):
    s = x_ref.shape[1]
    pl_ = (kw - 1) // 2
    zeros = jnp.zeros((halo, x_ref.shape[2]), F32)
    pad_ref[0:halo, :] = zeros
    pad_ref[halo + s:halo + s + halo, :] = zeros
    pad_ref[halo:halo + s, :] = x_ref[0]
    w = w_ref[...]
    acc = jnp.zeros((s, x_ref.shape[2]), F32) + b_ref[...]
    for j in range(kw):
        off = halo - pl_ + j
        acc = acc + pad_ref[off:off + s, :] * w[j:j + 1, :]
    if channel_major:
        acc_t = acc.T
        for g in range(LANES // SUBLANES):
            for blk in range(s // LANES):
                o_ref[g, blk * SUBLANES:(blk + 1) * SUBLANES, :] = (
                    acc_t[g * SUBLANES:(g + 1) * SUBLANES, blk * LANES:(blk + 1) * LANES])
    else:
        o_ref[0] = acc


def _dwconv(x3, w, bias, channel_major=False):
    b, s, c = x3.shape
    kw = w.shape[0]
    halo = 2 * SUBLANES
    kern = functools.partial(_dwconv_kernel, kw=kw, halo=halo, channel_major=channel_major)
    if channel_major:
        gpb = LANES // SUBLANES
        out_shape = jax.ShapeDtypeStruct((c // SUBLANES, b * (s // LANES) * SUBLANES, LANES), F32)
        out_spec = pl.BlockSpec((gpb, (s // LANES) * SUBLANES, LANES), lambda bi, ci: (ci, bi, 0))
    else:
        out_shape = jax.ShapeDtypeStruct((b, s, c), F32)
        out_spec = pl.BlockSpec((1, s, LANES), lambda bi, ci: (bi, 0, ci))
    return pl.pallas_call(
        kern,
        out_shape=out_shape,
        grid=(b, c // LANES),
        in_specs=[pl.BlockSpec((1, s, LANES), lambda bi, ci: (bi, 0, ci)),
                  pl.BlockSpec((kw, LANES), lambda bi, ci: (0, ci)),
                  pl.BlockSpec((1, LANES), lambda bi, ci: (0, ci))],
        out_specs=out_spec,
        scratch_shapes=[pltpu.VMEM((s + 2 * halo, LANES), F32)],
        compiler_params=_params(("parallel", "parallel")),
        name="dwconv",
    )(x3, w, bias.reshape(1, c))


def _filter_kernel(zt_ref, t_ref, w1_ref, b1_ref, f1_ref, w2_ref, b2_ref, f2_ref, w3_ref, b3_ref, dl_ref,
                   o_ref, h_ref, *, n):
    @pl.when(pl.program_id(0) == 0)
    def _():
        h1 = jnp.sin(f1_ref[...] * (_dot3(w1_ref[...], zt_ref[...]) + b1_ref[...]))
        h_ref[...] = jnp.sin(f2_ref[...] * (_dot3(w2_ref[...], h1) + b2_ref[...]))

    h2 = h_ref[...]
    rows = o_ref.shape[0]
    fwd = _dot3(w3_ref[0], h2[:, :n]) + b3_ref[0]
    bwd = _dot3(w3_ref[1], h2[:, n:]) + b3_ref[1]
    decay = jnp.exp(-t_ref[...] * dl_ref[...])
    lane = lax.broadcasted_iota(jnp.int32, (rows, n), 1)
    kf = fwd * decay[:, :n]
    kb = jnp.where(lane == 0, 0.0, bwd * decay[:, n:])
    inv = 1.0 / (jnp.sum(jnp.abs(kf), axis=-1, keepdims=True) + jnp.sum(jnp.abs(kb), axis=-1, keepdims=True))
    kf = kf * inv
    kb = kb * inv
    o_ref[:, 0:LANES] = kb[:, n - LANES:]
    o_ref[:, LANES:LANES + n] = kf
    o_ref[:, LANES + n:] = kb


def _hyena_filters(n, w1, b1, fr1, w2, b2, fr2, w3, b3):
    emb, ffn = w1.shape
    c = w3.shape[1] // (2 * HY_ORDER)
    bands = (emb - 1) // 2
    t = jnp.linspace(0.0, 1.0, n, dtype=F32)[:, None]
    ang = (2.0 * math.pi / n) * jnp.arange(n, dtype=F32)[:, None] * jnp.linspace(1e-4, bands - 1, bands, dtype=F32)[None, :]
    z = jnp.concatenate([t, jnp.cos(ang), -jnp.sin(ang)], axis=-1)
    rev = (n - jnp.arange(n)) % n
    z2 = jnp.concatenate([z, z[rev]], axis=0)
    t2 = jnp.concatenate([t, t[rev]], axis=0).reshape(1, 2 * n)
    emb_p = ((emb + SUBLANES - 1) // SUBLANES) * SUBLANES
    zt = jnp.zeros((emb_p, 2 * n), F32).at[:emb].set(z2.T)
    w1t = jnp.zeros((ffn, emb_p), F32).at[:, :emb].set(w1.T)
    deltas = jnp.abs(jnp.linspace(HY_MIN_DECAY, HY_MAX_DECAY, c, dtype=F32))
    w3t = w3.T.reshape(HY_ORDER, 2, c, ffn).transpose(1, 0, 2, 3).reshape(2, HY_ORDER * c, ffn)
    b3t = b3.reshape(HY_ORDER, 2, c).transpose(1, 0, 2).reshape(2, HY_ORDER * c, 1)
    dl = jnp.tile(deltas, HY_ORDER).reshape(HY_ORDER * c, 1)
    rows = LANES
    kern = functools.partial(_filter_kernel, n=n)
    col = lambda v: v.reshape(ffn, 1)
    return pl.pallas_call(
        kern,
        out_shape=jax.ShapeDtypeStruct((HY_ORDER * c, LANES + 2 * n), F32),
        grid=(HY_ORDER * c // rows,),
        in_specs=[pl.BlockSpec(zt.shape, lambda i: (0, 0)),
                  pl.BlockSpec(t2.shape, lambda i: (0, 0)),
                  pl.BlockSpec(w1t.shape, lambda i: (0, 0)),
                  pl.BlockSpec((ffn, 1), lambda i: (0, 0)),
                  pl.BlockSpec((ffn, 1), lambda i: (0, 0)),
                  pl.BlockSpec((ffn, ffn), lambda i: (0, 0)),
                  pl.BlockSpec((ffn, 1), lambda i: (0, 0)),
                  pl.BlockSpec((ffn, 1), lambda i: (0, 0)),
                  pl.BlockSpec((2, rows, ffn), lambda i: (0, i, 0)),
                  pl.BlockSpec((2, rows, 1), lambda i: (0, i, 0)),
                  pl.BlockSpec((rows, 1), lambda i: (i, 0))],
        out_specs=pl.BlockSpec((rows, LANES + 2 * n), lambda i: (i, 0)),
        scratch_shapes=[pltpu.VMEM((ffn, 2 * n), F32)],
        compiler_params=_params(("arbitrary",)),
        name="hyena_filters",
    )(zt, t2, w1t, col(b1), col(fr1), w2.T, col(b2), col(fr2), w3t, b3t, dl)


def _longconv_kernel(zero_ref, z_ref, gate_ref, kc_ref, kcn_ref, bias_ref, o_ref, r_a, r_b, zs_ref,
                     *, nb, bsz, cb):
    p = LC_P
    n2 = 2 * nb * p
    w = min(LC_W, n2)
    nchunk = n2 // w
    dots_per_chunk = nb // nchunk

    def build_chunk(src, r_dst, ci):
        k_ref, c = src
        win = k_ref[pl.ds(c, 1), ci * w:ci * w + w + LANES]
        rolled = pltpu.roll(jnp.broadcast_to(win, (p, w + LANES)), 0, 1, stride=1, stride_axis=0)
        r_dst[:, ci * w:(ci + 1) * w] = rolled[:, LANES:].astype(BF16)
        bits = pltpu.bitcast(rolled[0:SUBLANES, LANES:2 * LANES], jnp.int32) & zero_ref[...]
        return pltpu.bitcast(bits, F32)[0:1, :].astype(BF16)

    def conv(c, r_src, c_next, r_next):
        for s1 in range(nb):
            zs_ref[s1 * bsz:(s1 + 1) * bsz, :] = z_ref[0, pl.ds(c + cb * s1, bsz, stride=cb * nb), :]
        acc = [None] * nb
        held = None
        for pi in range(nb):
            if pi % dots_per_chunk == 0:
                held = build_chunk(c_next, r_next, pi // dots_per_chunk)
            d = -nb + 2 * pi
            off = (d % (2 * nb)) * p
            lo = max(0, -d - 1)
            hi = min(nb, nb - d)
            lhs = zs_ref[lo * bsz:hi * bsz, :].astype(BF16) + held
            out = _dot(lhs, r_src[:, off:off + 2 * p])
            for k in range(2):
                dk = d + k
                for s1 in range(max(0, -dk), min(nb, nb - dk)):
                    blk = out[(s1 - lo) * bsz:(s1 - lo + 1) * bsz, k * p:(k + 1) * p]
                    acc[s1 + dk] = blk if acc[s1 + dk] is None else acc[s1 + dk] + blk
        bias = bias_ref[c]
        for s1 in range(nb):
            rows = slice(s1 * bsz, (s1 + 1) * bsz)
            where = pl.ds(c + cb * s1, bsz, stride=cb * nb)
            o_ref[0, where, :] = gate_ref[0, where, :] * (acc[s1] + zs_ref[rows, :] * bias)

    @pl.when(pl.program_id(0) == 0)
    def _():
        for ci in range(nchunk):
            build_chunk((kc_ref, 0), r_a, ci)

    def pair(k, carry):
        c = 2 * k
        conv(c, r_a, (kc_ref, c + 1), r_b)
        conv(c + 1, r_b, (kc_ref, c + 2), r_a)
        return carry

    lax.fori_loop(0, cb // 2 - 1, pair, 0)
    conv(cb - 2, r_a, (kc_ref, cb - 1), r_b)
    conv(cb - 1, r_b, (kcn_ref, 0), r_a)


def _longconv(z_cm, z_g0, gate_cm, gate_g0, kc_ext, bias_cm, order, groups, nb, bsz):
    cb = SUBLANES
    rows, p = z_cm.shape[1:]
    kern = functools.partial(_longconv_kernel, nb=nb, bsz=bsz, cb=cb)
    return pl.pallas_call(
        kern,
        out_shape=jax.ShapeDtypeStruct((groups, rows, p), F32),
        grid=(groups,),
        in_specs=[pl.BlockSpec((1, LANES), lambda i: (0, 0)),
                  pl.BlockSpec((1, rows, p), lambda i: (z_g0 + i, 0, 0)),
                  pl.BlockSpec((1, rows, p), lambda i: (gate_g0 + i, 0, 0)),
                  pl.BlockSpec((cb, kc_ext.shape[1]), lambda i: (order * groups + i, 0)),
                  pl.BlockSpec((cb, kc_ext.shape[1]), lambda i: (order * groups + jnp.minimum(i + 1, groups - 1), 0)),
                  pl.BlockSpec((cb, 1, p), lambda i: (order * groups + i, 0, 0))],
        out_specs=pl.BlockSpec((1, rows, p), lambda i: (i, 0, 0)),
        scratch_shapes=[pltpu.VMEM((p, 2 * nb * p), BF16), pltpu.VMEM((p, 2 * nb * p), BF16),
                        pltpu.VMEM((nb * bsz, p), F32)],
        compiler_params=_params(("arbitrary",)),
        name="hyena_longconv",
    )(jnp.zeros((1, LANES), jnp.int32), z_cm, gate_cm, kc_ext, kc_ext, bias_cm)


def _outproj_kernel(x_ref, g1_ref, a_ref, b_ref, w_ref, o_ref):
    da = a_ref.shape[1]
    ya = _dot(a_ref[...].astype(BF16), w_ref[:da, :])
    groups, rows, p = b_ref.shape
    parts = []
    for blk in range(rows // SUBLANES):
        cm = b_ref[:, blk * SUBLANES:(blk + 1) * SUBLANES, :].reshape(groups * SUBLANES, p)
        parts.append(_dot(cm.T.astype(BF16), w_ref[da:, :]))
    y = ya + jnp.concatenate(parts, axis=0)
    o_ref[...] = x_ref[...] + g1_ref[0] * y


def _outproj(x2, g1, oa, ob, w, seq, tm=512):
    t, d = x2.shape
    tpb = seq // tm
    return pl.pallas_call(
        _outproj_kernel,
        out_shape=jax.ShapeDtypeStruct((t, d), F32),
        grid=(t // tm,),
        in_specs=[pl.BlockSpec((tm, d), lambda i: (i, 0)),
                  pl.BlockSpec((1, 1, d), lambda i: (i // tpb, 0, 0)),
                  pl.BlockSpec((tm, oa.shape[1]), lambda i: (i, 0)),
                  pl.BlockSpec((ob.shape[0], (tm // LANES) * SUBLANES, LANES), lambda i: (0, i, 0)),
                  pl.BlockSpec(w.shape, lambda i: (0, 0))],
        out_specs=pl.BlockSpec((tm, d), lambda i: (i, 0)),
        compiler_params=_params(("parallel",)),
        name="outproj0",
    )(x2, g1, oa, ob, w)


def _confin_kernel(x_ref, g_ref, sh_ref, sc_ref, w_ref, b_ref, o_ref):
    h = _norm_mod(x_ref[...], g_ref[...], sh_ref[0], sc_ref[0]).astype(BF16)
    n = o_ref.shape[1]
    a = _dot(h, w_ref[:, :n]) + b_ref[:, :n]
    gt = _dot(h, w_ref[:, n:]) + b_ref[:, n:]
    o_ref[...] = a * jax.nn.sigmoid(gt)


def _confin(x2, g, sh, sc, w1, b1, seq, tm=512):
    t, d = x2.shape
    n = w1.shape[1] // 2
    tpb = seq // tm
    return pl.pallas_call(
        _confin_kernel,
        out_shape=jax.ShapeDtypeStruct((t, n), F32),
        grid=(t // tm,),
        in_specs=[pl.BlockSpec((tm, d), lambda i: (i, 0)),
                  pl.BlockSpec((1, d), lambda i: (0, 0)),
                  pl.BlockSpec((1, 1, d), lambda i: (i // tpb, 0, 0)),
                  pl.BlockSpec((1, 1, d), lambda i: (i // tpb, 0, 0)),
                  pl.BlockSpec(w1.shape, lambda i: (0, 0)),
                  pl.BlockSpec((1, 2 * n), lambda i: (0, 0))],
        out_specs=pl.BlockSpec((tm, n), lambda i: (i, 0)),
        compiler_params=_params(("parallel",)),
        name="conformer_in",
    )(x2, g, sh, sc, w1, b1.reshape(1, 2 * n))


def _confout_kernel(x_ref, g1_ref, a_ref, lg_ref, lb_ref, w_ref, b_ref, o_ref):
    a = a_ref[...]
    mu = jnp.mean(a, axis=-1, keepdims=True)
    ac = a - mu
    var = jnp.mean(ac * ac, axis=-1, keepdims=True)
    y = ac * lax.rsqrt(var + EPS) * lg_ref[...] + lb_ref[...]
    y = y * jax.nn.sigmoid(y)
    o_ref[...] = x_ref[...] + g1_ref[0] * (_dot(y.astype(BF16), w_ref[...]) + b_ref[...])


def _confout(x2, g1, a2, ln_g, ln_b, w2, b2, seq, tm=512):
    t, d = x2.shape
    n = a2.shape[1]
    tpb = seq // tm
    return pl.pallas_call(
        _confout_kernel,
        out_shape=jax.ShapeDtypeStruct((t, d), F32),
        grid=(t // tm,),
        in_specs=[pl.BlockSpec((tm, d), lambda i: (i, 0)),
                  pl.BlockSpec((1, 1, d), lambda i: (i // tpb, 0, 0)),
                  pl.BlockSpec((tm, n), lambda i: (i, 0)),
                  pl.BlockSpec((1, n), lambda i: (0, 0)),
                  pl.BlockSpec((1, n), lambda i: (0, 0)),
                  pl.BlockSpec(w2.shape, lambda i: (0, 0)),
                  pl.BlockSpec((1, d), lambda i: (0, 0))],
        out_specs=pl.BlockSpec((tm, d), lambda i: (i, 0)),
        compiler_params=_params(("parallel",)),
        name="conformer_out",
    )(x2, g1, a2, ln_g.reshape(1, n), ln_b.reshape(1, n), w2, b2.reshape(1, d))


def _router_kernel(x_ref, g_ref, sh_ref, sc_ref, wr_ref, br_ref, h_ref, route_ref, cnt_ref, run_ref):
    i = pl.program_id(0)

    @pl.when(i == 0)
    def _():
        run_ref[...] = jnp.zeros_like(run_ref)

    h = _norm_mod(x_ref[...], g_ref[...], sh_ref[0], sc_ref[0])
    hh, hl = _split(h)
    h_ref[...] = hh
    logits = _dot(hh, wr_ref[0]) + _dot(hl, wr_ref[0]) + _dot(hh, wr_ref[1]) + br_ref[...]
    tm = logits.shape[0]
    lane = lax.broadcasted_iota(jnp.int32, (tm, LANES), 1)
    ninf = jnp.float32(-jnp.inf)

    def first_argmax(v, m):
        return jnp.min(jnp.where(v == m, lane, LANES), axis=-1, keepdims=True)

    gl = jnp.where(lane < N_GROUPS, logits, ninf)
    gmax = jnp.max(gl, axis=-1, keepdims=True)
    g_w = 1.0 / jnp.sum(jnp.exp(gl - gmax), axis=-1, keepdims=True)
    g_idx = first_argmax(gl, gmax)
    e_lo = ROUTE_LANE0 + EXPERTS_PER_GROUP * g_idx
    el = jnp.where((lane >= e_lo) & (lane < e_lo + EXPERTS_PER_GROUP), logits, ninf)
    m1 = jnp.max(el, axis=-1, keepdims=True)
    esum = jnp.sum(jnp.exp(el - m1), axis=-1, keepdims=True)
    i1 = first_argmax(el, m1)
    el2 = jnp.where(lane == i1, ninf, el)
    m2 = jnp.max(el2, axis=-1, keepdims=True)
    i2 = first_argmax(el2, m2)
    p1 = 1.0 / esum
    p2 = jnp.exp(m2 - m1) / esum
    w1 = g_w * (p1 / (p1 + p2))
    w2 = g_w * (p2 / (p1 + p2))

    oh = jnp.where((lane == i1) | (lane == i2), 1.0, 0.0)
    r_i = lax.broadcasted_iota(jnp.int32, (tm, tm), 0)
    c_i = lax.broadcasted_iota(jnp.int32, (tm, tm), 1)
    tri = jnp.where(c_i < r_i, 1.0, 0.0).astype(BF16)
    before = _dot(tri, oh.astype(BF16)) + run_ref[...]
    rank1 = jnp.sum(jnp.where(lane == i1, before, 0.0), axis=-1, keepdims=True)
    rank2 = jnp.sum(jnp.where(lane == i2, before, 0.0), axis=-1, keepdims=True)
    run_ref[...] = run_ref[...] + jnp.sum(oh, axis=0, keepdims=True)
    cnt_ref[...] = run_ref[...]

    e1 = (i1 - ROUTE_LANE0).astype(F32)
    e2 = (i2 - ROUTE_LANE0).astype(F32)
    vals = (e1, e2, rank1, rank2, w1, w2)
    out = jnp.zeros((tm, LANES), F32)
    for k, v in enumerate(vals):
        out = jnp.where(lane == k, v, out)
    route_ref[...] = out


def _router(x2, g, sh, sc, wr, br, seq, tm=512):
    t, d = x2.shape
    tpb = seq // tm
    return pl.pallas_call(
        _router_kernel,
        out_shape=(jax.ShapeDtypeStruct((t, d), BF16), jax.ShapeDtypeStruct((t, LANES), F32),
                   jax.ShapeDtypeStruct((1, LANES), F32)),
        grid=(t // tm,),
        in_specs=[pl.BlockSpec((tm, d), lambda i: (i, 0)),
                  pl.BlockSpec((1, d), lambda i: (0, 0)),
                  pl.BlockSpec((1, 1, d), lambda i: (i // tpb, 0, 0)),
                  pl.BlockSpec((1, 1, d), lambda i: (i // tpb, 0, 0)),
                  pl.BlockSpec(wr.shape, lambda i: (0, 0, 0)),
                  pl.BlockSpec((1, LANES), lambda i: (0, 0))],
        out_specs=(pl.BlockSpec((tm, d), lambda i: (i, 0)), pl.BlockSpec((tm, LANES), lambda i: (i, 0)),
                   pl.BlockSpec((1, LANES), lambda i: (0, 0))),
        scratch_shapes=[pltpu.VMEM((1, LANES), F32)],
        compiler_params=_params(("arbitrary",)),
        name="moe_router",
    )(x2, g, sh, sc, wr, br)


def _expert_kernel(te_ref, nv_ref, x_ref, wg_ref, wu_ref, wd_ref, o_ref, wg_s, wu_s, wd_s):
    i = pl.program_id(0)

    @pl.when((i == 0) | (te_ref[i] != te_ref[jnp.maximum(i - 1, 0)]))
    def _():
        wg_s[...] = wg_ref[0].astype(BF16)
        wu_s[...] = wu_ref[0].astype(BF16)
        wd_s[...] = wd_ref[0].astype(BF16)

    @pl.when(i < nv_ref[0])
    def _():
        x = x_ref[...]
        a = _dot(x, wg_s[...])
        u = _dot(x, wu_s[...])
        he = (a * jax.nn.sigmoid(a)) * u
        o_ref[...] = _dot(he.astype(BF16), wd_s[...]).astype(o_ref.dtype)


def _experts(xs, tile_expert, n_valid, wg, wu, wd, layer):
    r, d = xs.shape
    de = wg.shape[2]
    nt = r // MOE_TILE
    row = lambda i, te, nv: (jnp.minimum(i, nv[0] - 1), 0)
    wsel = lambda i, te, nv: (layer * N_EXPERTS + te[i], 0, 0)
    return pl.pallas_call(
        _expert_kernel,
        out_shape=jax.ShapeDtypeStruct((r, d), BF16),
        grid_spec=pltpu.PrefetchScalarGridSpec(
            num_scalar_prefetch=2,
            grid=(nt,),
            in_specs=[pl.BlockSpec((MOE_TILE, d), row),
                      pl.BlockSpec((1, d, de), wsel),
                      pl.BlockSpec((1, d, de), wsel),
                      pl.BlockSpec((1, de, d), wsel)],
            out_specs=pl.BlockSpec((MOE_TILE, d), row),
            scratch_shapes=[pltpu.VMEM((d, de), BF16), pltpu.VMEM((d, de), BF16), pltpu.VMEM((de, d), BF16)]),
        compiler_params=_params(("arbitrary",)),
        name="moe_experts",
    )(tile_expert, n_valid, xs, wg, wu, wd)


def _combine_kernel(x_ref, g2_ref, route_ref, y1_ref, y2_ref, fg_ref, o_ref, *, final):
    r = route_ref[...]
    w1 = r[:, 4:5]
    w2 = r[:, 5:6]
    x = x_ref[...] + g2_ref[0] * (w1 * y1_ref[...].astype(F32) + w2 * y2_ref[...].astype(F32))
    if final:
        x = (x * lax.rsqrt(jnp.mean(x * x, axis=-1, keepdims=True) + EPS)) * fg_ref[...]
    o_ref[...] = x


def _combine(x2, g2, route, y12, final_g, seq, final, tm=512):
    t, d = x2.shape
    tpb = seq // tm
    nt = t // tm
    kern = functools.partial(_combine_kernel, final=final)
    return pl.pallas_call(
        kern,
        out_shape=jax.ShapeDtypeStruct((t, d), F32),
        grid=(t // tm,),
        in_specs=[pl.BlockSpec((tm, d), lambda i: (i, 0)),
                  pl.BlockSpec((1, 1, d), lambda i: (i // tpb, 0, 0)),
                  pl.BlockSpec((tm, LANES), lambda i: (i, 0)),
                  pl.BlockSpec((tm, d), lambda i: (i, 0)),
                  pl.BlockSpec((tm, d), lambda i: (nt + i, 0)),
                  pl.BlockSpec((1, d), lambda i: (0, 0))],
        out_specs=pl.BlockSpec((tm, d), lambda i: (i, 0)),
        compiler_params=_params(("parallel",)),
        name="moe_combine",
    )(x2, g2, route, y12, y12, final_g)


def _moe(x2, g, sh, sc, g2, wg_r, bg_r, we_r, be_r, w_gate, w_up, w_down, layer, final_g, seq, final):
    t, d = x2.shape
    wr = jnp.zeros((d, LANES), F32).at[:, :N_GROUPS].set(wg_r).at[:, ROUTE_LANE0:ROUTE_LANE0 + N_EXPERTS].set(we_r)
    wr_hi = wr.astype(BF16)
    wr_lo = (wr - wr_hi.astype(F32)).astype(BF16)
    br = jnp.zeros((1, LANES), F32).at[0, :N_GROUPS].set(bg_r).at[0, ROUTE_LANE0:ROUTE_LANE0 + N_EXPERTS].set(be_r)
    h, route, cnt = _router(x2, g, sh, sc, jnp.stack([wr_hi, wr_lo]), br, seq)

    counts = cnt[0, ROUTE_LANE0:ROUTE_LANE0 + N_EXPERTS].astype(jnp.int32)
    tiles = (counts + MOE_TILE - 1) // MOE_TILE
    tile_end = jnp.cumsum(tiles)
    offs = (tile_end - tiles) * MOE_TILE
    nt = (2 * t) // MOE_TILE + N_EXPERTS
    tile_id = jnp.minimum(jnp.arange(nt, dtype=jnp.int32), tile_end[-1] - 1)
    tile_expert = jnp.sum((tile_id[:, None] >= tile_end[None, :]).astype(jnp.int32), axis=1)
    n_valid = tile_end[-1:].astype(jnp.int32)
    e12 = route[:, 0:2].astype(jnp.int32)
    offs12 = jnp.sum(jnp.where(e12[:, :, None] == jnp.arange(N_EXPERTS)[None, None, :], offs[None, None, :], 0), axis=-1)
    pos = (offs12 + route[:, 2:4].astype(jnp.int32)).T.reshape(-1)
    tok = jnp.tile(jnp.arange(t, dtype=jnp.int32), 2)
    sorted_tok = (jnp.arange(nt * MOE_TILE, dtype=jnp.int32) % t).at[pos].set(
        tok, unique_indices=True, indices_are_sorted=False, mode="promise_in_bounds")
    xs = h.at[sorted_tok].get(mode="promise_in_bounds")
    ys = _experts(xs, tile_expert, n_valid, w_gate, w_up, w_down, layer)
    y12 = ys.at[pos].get(mode="promise_in_bounds")
    return _combine(x2, g2, route, y12, final_g, seq, final)


def _rope_tables(seq, width):
    hd = DA_HEAD_DIM
    half = hd // 2
    quarter = half // 2
    pos = jnp.arange(seq)
    row = (pos // GRID_W).astype(F32)
    col = (pos % GRID_W).astype(F32)
    inv = ROPE_THETA ** (-jnp.arange(0, half, 2, dtype=F32) / half)
    i = jnp.arange(hd)
    p = jnp.where((i < half)[None, :], row[:, None], col[:, None])
    ang = p * inv[i % quarter][None, :]
    sign = jnp.where((i % half) < quarter, -1.0, 1.0)[None, :]
    reps = width // hd
    return jnp.tile(jnp.cos(ang), (1, reps)), jnp.tile(jnp.sin(ang) * sign, (1, reps))


def _rope_partner(width):
    i = jnp.arange(width)
    quarter = DA_HEAD_DIM // 4
    return jnp.where((i % (2 * quarter)) < quarter, i + quarter, i - quarter)


def kernel(x, c, ctx, c_ctx, ada_w, ada_b, norm1_g, norm2_g, final_g, w_in0, w_out0, lam_q1, lam_k1, lam_q2, lam_k2, subln_g, hy_short_w, hy_short_b, hy_w1, hy_b1, hy_fr1, hy_w2, hy_b2, hy_fr2, hy_w3, hy_b3, hy_bias, cv_w1, cv_b1, cv_dw_w, cv_dw_b, cv_ln_g, cv_ln_b, cv_w2, cv_b2, moe_wg, moe_bg, moe_we, moe_be, moe_w_gate, moe_w_up, moe_w_down):
    bsz, seq, d = x.shape
    lctx = ctx.shape[1]
    depth = ada_w.shape[0]
    t = bsz * seq
    hyw = d - DA_WIDTH
    x2 = x.reshape(t, d)

    rows = ((bsz + 1 + SUBLANES - 1) // SUBLANES) * SUBLANES
    cs = jnp.zeros((rows, d), F32).at[:bsz].set(c).at[bsz].set(c_ctx)
    mods = _ada(cs, ada_w, ada_b)

    def tok_mod(i, k):
        return mods[i, :bsz, k * d:(k + 1) * d].reshape(bsz, 1, d)

    def ctx_mod(i, k):
        return mods[i, bsz:bsz + 1, k * d:(k + 1) * d].reshape(1, 1, d)

    for i in range(depth):
        j = i // 2
        g_n1 = norm1_g[i].reshape(1, d)
        if i % 2 == 0:
            assert not any(m % 2 == 0 for m in range(i + 1, depth)), "context-stream update is not implemented"
            lam_init = 0.8 - 0.6 * math.exp(-0.3 * i)
            w_in = w_in0[j]
            part = _rope_partner(2 * DA_WIDTH)
            w_ext = jnp.concatenate([w_in, w_in[:, :2 * DA_WIDTH][:, part]], axis=1).astype(BF16)
            cw = 2 * LANES
            cos, sin = _rope_tables(seq, cw)
            q, k, v, u = _inproj(x2, g_n1, tok_mod(i, 0), tok_mod(i, 1), cos, sin, w_ext, seq)
            kc, vc = _ctxproj(ctx.reshape(bsz * lctx, d), g_n1, ctx_mod(i, 0), ctx_mod(i, 1),
                              w_in[:, DA_WIDTH:3 * DA_WIDTH].astype(BF16), lctx)
            lam_p = jnp.stack([lam_q1[j], lam_k1[j], lam_q2[j], lam_k2[j]])
            k_all = jnp.concatenate([kc.reshape(bsz, lctx, DA_WIDTH), k.reshape(bsz, seq, DA_WIDTH)], axis=1)
            v_all = jnp.concatenate([vc.reshape(bsz, lctx, DA_WIDTH), v.reshape(bsz, seq, DA_WIDTH)], axis=1)
            o_a = _attention(q, jnp.swapaxes(k_all, 1, 2), v_all, lam_p, subln_g[j].reshape(1, -1), seq, lam_init)

            nb = seq // LC_P
            groups = hyw // SUBLANES
            ucm = _dwconv(u.reshape(bsz, seq, 3 * hyw), hy_short_w[j], hy_short_b[j], channel_major=True)
            kc_ext = _hyena_filters(seq, hy_w1[j], hy_b1[j], hy_fr1[j], hy_w2[j], hy_b2[j], hy_fr2[j],
                                    hy_w3[j], hy_b3[j])
            bias_cm = jnp.broadcast_to(hy_bias[j].reshape(HY_ORDER * hyw, 1, 1), (HY_ORDER * hyw, 1, LC_P))
            z1 = _longconv(ucm, 0, ucm, groups, kc_ext, bias_cm, 0, groups, nb, bsz)
            z2 = _longconv(z1, 0, ucm, 2 * groups, kc_ext, bias_cm, 1, groups, nb, bsz)
            x2 = _outproj(x2, tok_mod(i, 2), o_a, z2, w_out0[j].astype(BF16), seq)
        else:
            a = _confin(x2, g_n1, tok_mod(i, 0), tok_mod(i, 1), cv_w1[j].astype(BF16), cv_b1[j], seq)
            a = _dwconv(a.reshape(bsz, seq, -1), cv_dw_w[j], cv_dw_b[j]).reshape(t, -1)
            x2 = _confout(x2, tok_mod(i, 2), a, cv_ln_g[j], cv_ln_b[j], cv_w2[j].astype(BF16), cv_b2[j], seq)
        x2 = _moe(x2, norm2_g[i].reshape(1, d), tok_mod(i, 3), tok_mod(i, 4), tok_mod(i, 5),
                  moe_wg[i], moe_bg[i], moe_we[i], moe_be[i],
                  moe_w_gate.reshape((-1,) + moe_w_gate.shape[2:]), moe_w_up.reshape((-1,) + moe_w_up.shape[2:]),
                  moe_w_down.reshape((-1,) + moe_w_down.shape[2:]), i,
                  final_g.reshape(1, d), seq, final=(i == depth - 1))
    return x2.reshape(bsz, seq, d)
```

```python
import functools
import math

import jax
import jax.numpy as jnp
from jax import lax
from jax.experimental import pallas as pl
from jax.experimental.pallas import tpu as pltpu

F32 = jnp.float32
BF16 = jnp.bfloat16

GRID_W = 64
DA_HEADS = 4
DA_HEAD_DIM = 64
DA_WIDTH = DA_HEADS * 2 * DA_HEAD_DIM
HY_ORDER = 2
HY_TARGET = 1e-2
HY_MIN_DECAY = math.log(HY_TARGET) / 0.3
HY_MAX_DECAY = math.log(HY_TARGET) / 1.5
N_GROUPS = 4
EXPERTS_PER_GROUP = 8
N_EXPERTS = N_GROUPS * EXPERTS_PER_GROUP
ROPE_THETA = 10000.0
EPS = 1e-6

LANES = 128
SUBLANES = 8
VMEM_LIMIT = 52 * 1024 * 1024
ROUTE_LANE0 = N_GROUPS
MOE_TILE = 256
LC_P = 128
LC_W = 1024


def _params(sem, flags=None):
    return pltpu.CompilerParams(dimension_semantics=sem, vmem_limit_bytes=VMEM_LIMIT, flags=flags)


def _split(a):
    hi = a.astype(BF16)
    lo = (a - hi.astype(F32)).astype(BF16)
    return hi, lo


def _dot(a, b):
    return jnp.dot(a, b, preferred_element_type=F32)


def _dot3(a, b):
    ah, al = _split(a)
    bh, bl = _split(b)
    return _dot(ah, bh) + _dot(al, bh) + _dot(ah, bl)


def _norm_mod(x, g, sh, sc):
    y = x * lax.rsqrt(jnp.mean(x * x, axis=-1, keepdims=True) + EPS)
    return (y * g) * (1.0 + sc) + sh


def _ada_kernel(c_ref, w_ref, b_ref, o_ref):
    c = c_ref[...]
    s = c * jax.nn.sigmoid(c)
    o_ref[0] = _dot3(s, w_ref[0]) + b_ref[0]


def _ada(cs, ada_w, ada_b):
    depth, d, n6 = ada_w.shape
    rows = cs.shape[0]
    tn = 1536
    return pl.pallas_call(
        _ada_kernel,
        out_shape=jax.ShapeDtypeStruct((depth, rows, n6), F32),
        grid=(depth, n6 // tn),
        in_specs=[pl.BlockSpec((rows, d), lambda l, j: (0, 0)),
                  pl.BlockSpec((1, d, tn), lambda l, j: (l, 0, j)),
                  pl.BlockSpec((1, 1, tn), lambda l, j: (l, 0, j))],
        out_specs=pl.BlockSpec((1, rows, tn), lambda l, j: (l, 0, j)),
        compiler_params=_params(("parallel", "parallel")),
        name="adaln",
    )(cs, ada_w, ada_b.reshape(depth, 1, n6))


def _inproj_kernel(x_ref, g_ref, sh_ref, sc_ref, cos_ref, sin_ref, w_ref,
                   q_ref, k_ref, v_ref, u_ref, *, d_attn, d_hy):
    h = _norm_mod(x_ref[...], g_ref[...], sh_ref[0], sc_ref[0]).astype(BF16)
    cos = cos_ref[...]
    sin = sin_ref[...]
    cw = cos.shape[1]
    quarter = DA_HEAD_DIM // 4
    lane = lax.broadcasted_iota(jnp.int32, cos.shape, 1)
    first = (lane % (2 * quarter)) < quarter

    def mm(c0, c1):
        return _dot(h, w_ref[:, c0:c1])

    def rope(a):
        partner = jnp.where(first, pltpu.roll(a, cw - quarter, 1), pltpu.roll(a, quarter, 1))
        return a * cos + partner * sin

    for j in range(d_attn // cw):
        q_ref[:, j * cw:(j + 1) * cw] = (rope(mm(j * cw, (j + 1) * cw)) * (DA_HEAD_DIM ** -0.5)).astype(BF16)
        k_ref[:, j * cw:(j + 1) * cw] = rope(mm(d_attn + j * cw, d_attn + (j + 1) * cw)).astype(BF16)
    v_ref[...] = mm(2 * d_attn, 3 * d_attn).astype(BF16)
    for j in range(d_hy // 512):
        u_ref[:, j * 512:(j + 1) * 512] = mm(3 * d_attn + j * 512, 3 * d_attn + (j + 1) * 512)


def _inproj(x2, g, sh, sc, cos, sin, w_ext, seq, tm=512):
    t, d = x2.shape
    tpb = seq // tm
    d_attn = DA_WIDTH
    d_hy = w_ext.shape[1] - 3 * d_attn
    cw = cos.shape[1]
    kern = functools.partial(_inproj_kernel, d_attn=d_attn, d_hy=d_hy)
    return pl.pallas_call(
        kern,
        out_shape=(jax.ShapeDtypeStruct((t, d_attn), BF16), jax.ShapeDtypeStruct((t, d_attn), BF16),
                   jax.ShapeDtypeStruct((t, d_attn), BF16), jax.ShapeDtypeStruct((t, d_hy), F32)),
        grid=(t // tm,),
        in_specs=[pl.BlockSpec((tm, d), lambda i: (i, 0)),
                  pl.BlockSpec((1, d), lambda i: (0, 0)),
                  pl.BlockSpec((1, 1, d), lambda i: (i // tpb, 0, 0)),
                  pl.BlockSpec((1, 1, d), lambda i: (i // tpb, 0, 0)),
                  pl.BlockSpec((tm, cw), lambda i: (i % tpb, 0)),
                  pl.BlockSpec((tm, cw), lambda i: (i % tpb, 0)),
                  pl.BlockSpec(w_ext.shape, lambda i: (0, 0))],
        out_specs=(pl.BlockSpec((tm, d_attn), lambda i: (i, 0)), pl.BlockSpec((tm, d_attn), lambda i: (i, 0)),
                   pl.BlockSpec((tm, d_attn), lambda i: (i, 0)), pl.BlockSpec((tm, d_hy), lambda i: (i, 0))),
        compiler_params=_params(("parallel",)),
        name="inproj0",
    )(x2, g, sh, sc, cos, sin, w_ext)


def _ctxproj_kernel(x_ref, g_ref, sh_ref, sc_ref, w_ref, k_ref, v_ref):
    h = _norm_mod(x_ref[...], g_ref[...], sh_ref[0], sc_ref[0]).astype(BF16)
    n = k_ref.shape[1]
    k_ref[...] = _dot(h, w_ref[:, :n]).astype(BF16)
    v_ref[...] = _dot(h, w_ref[:, n:]).astype(BF16)


def _ctxproj(c2, g, sh, sc, w_kv, tm):
    t, d = c2.shape
    n = w_kv.shape[1] // 2
    return pl.pallas_call(
        _ctxproj_kernel,
        out_shape=(jax.ShapeDtypeStruct((t, n), BF16), jax.ShapeDtypeStruct((t, n), BF16)),
        grid=(t // tm,),
        in_specs=[pl.BlockSpec((tm, d), lambda i: (i, 0)),
                  pl.BlockSpec((1, d), lambda i: (0, 0)),
                  pl.BlockSpec((1, 1, d), lambda i: (0, 0, 0)),
                  pl.BlockSpec((1, 1, d), lambda i: (0, 0, 0)),
                  pl.BlockSpec(w_kv.shape, lambda i: (0, 0))],
        out_specs=(pl.BlockSpec((tm, n), lambda i: (i, 0)), pl.BlockSpec((tm, n), lambda i: (i, 0))),
        compiler_params=_params(("parallel",)),
        name="ctxproj",
    )(c2, g, sh, sc, w_kv)


def _attn_kernel(lam_ref, g_ref, q_ref, kt_ref, v_ref, o_ref, *, lam_init):
    lp = lam_ref[...]
    lam = (jnp.exp(jnp.sum(lp[0:1] * lp[1:2], axis=-1, keepdims=True))
           - jnp.exp(jnp.sum(lp[2:3] * lp[3:4], axis=-1, keepdims=True)) + lam_init)
    q = q_ref[...]
    lane = lax.broadcasted_iota(jnp.int32, q.shape, 1)
    zero = jnp.zeros_like(q)
    kt = kt_ref[0]
    s1 = _dot(jnp.where(lane < DA_HEAD_DIM, q, zero), kt)
    s2 = _dot(jnp.where(lane >= DA_HEAD_DIM, q, zero), kt)
    e1 = jnp.exp(s1 - jnp.max(s1, axis=-1, keepdims=True))
    e2 = jnp.exp(s2 - jnp.max(s2, axis=-1, keepdims=True))
    r1 = 1.0 / jnp.sum(e1, axis=-1, keepdims=True)
    r2 = lam / jnp.sum(e2, axis=-1, keepdims=True)
    a = (e1 * r1 - e2 * r2).astype(BF16)
    o = _dot(a, v_ref[0])
    y = o * lax.rsqrt(jnp.mean(o * o, axis=-1, keepdims=True) + EPS)
    o_ref[...] = (y * g_ref[...]) * (1.0 - lam_init)


def _attention(q, kt_all, v_all, lam_p, subln_g, seq, lam_init, tq=256):
    t = q.shape[0]
    b, _, lk = kt_all.shape
    hw = 2 * DA_HEAD_DIM
    nq = seq // tq
    kern = functools.partial(_attn_kernel, lam_init=lam_init)
    return pl.pallas_call(
        kern,
        out_shape=jax.ShapeDtypeStruct((t, DA_WIDTH), F32),
        grid=(b, DA_HEADS, nq),
        in_specs=[pl.BlockSpec(lam_p.shape, lambda bi, h, i: (0, 0)),
                  pl.BlockSpec((1, hw), lambda bi, h, i: (0, 0)),
                  pl.BlockSpec((tq, hw), lambda bi, h, i: (bi * nq + i, h)),
                  pl.BlockSpec((1, hw, lk), lambda bi, h, i: (bi, h, 0)),
                  pl.BlockSpec((1, lk, hw), lambda bi, h, i: (bi, 0, h))],
        out_specs=pl.BlockSpec((tq, hw), lambda bi, h, i: (bi * nq + i, h)),
        compiler_params=_params(("parallel", "parallel", "parallel")),
        name="diff_attn",
    )(lam_p, subln_g, q, kt_all, v_all)


def _dwconv_kernel(x_ref, w_ref, b_ref, o_ref, pad_ref, *, kw, halo, channel_major):
    s = x_ref.shape[1]
    pl_ = (kw - 1) // 2
    zeros = jnp.zeros((halo, x_ref.shape[2]), F32)
    pad_ref[0:halo, :] = zeros
    pad_ref[halo + s:halo + s + halo, :] = zeros
    pad_ref[halo:halo + s, :] = x_ref[0]
    w = w_ref[...]
    acc = jnp.zeros((s, x_ref.shape[2]), F32) + b_ref[...]
    for j in range(kw):
        off = halo - pl_ + j
        acc = acc + pad_ref[off:off + s, :] * w[j:j + 1, :]
    if channel_major:
        acc_t = acc.T
        for g in range(LANES // SUBLANES):
            for blk in range(s // LANES):
                o_ref[g, blk] = acc_t[g * SUBLANES:(g + 1) * SUBLANES, blk * LANES:(blk + 1) * LANES]
    else:
        o_ref[0] = acc


def _dwconv(x3, w, bias, channel_major=False):
    b, s, c = x3.shape
    kw = w.shape[0]
    halo = 2 * SUBLANES
    kern = functools.partial(_dwconv_kernel, kw=kw, halo=halo, channel_major=channel_major)
    if channel_major:
        gpb = LANES // SUBLANES
        out_shape = jax.ShapeDtypeStruct((c // SUBLANES, s // LANES, b * SUBLANES, LANES), F32)
        out_spec = pl.BlockSpec((gpb, s // LANES, SUBLANES, LANES), lambda bi, ci: (ci, 0, bi, 0))
    else:
        out_shape = jax.ShapeDtypeStruct((b, s, c), F32)
        out_spec = pl.BlockSpec((1, s, LANES), lambda bi, ci: (bi, 0, ci))
    return pl.pallas_call(
        kern,
        out_shape=out_shape,
        grid=(b, c // LANES),
        in_specs=[pl.BlockSpec((1, s, LANES), lambda bi, ci: (bi, 0, ci)),
                  pl.BlockSpec((kw, LANES), lambda bi, ci: (0, ci)),
                  pl.BlockSpec((1, LANES), lambda bi, ci: (0, ci))],
        out_specs=out_spec,
        scratch_shapes=[pltpu.VMEM((s + 2 * halo, LANES), F32)],
        compiler_params=_params(("parallel", "parallel")),
        name="dwconv",
    )(x3, w, bias.reshape(1, c))


def _filter_kernel(zt_ref, t_ref, w1_ref, b1_ref, f1_ref, w2_ref, b2_ref, f2_ref, w3_ref, b3_ref, dl_ref,
                   o_ref, h_ref, *, n):
    @pl.when(pl.program_id(0) == 0)
    def _():
        h1 = jnp.sin(f1_ref[...] * (_dot3(w1_ref[...], zt_ref[...]) + b1_ref[...]))
        h_ref[...] = jnp.sin(f2_ref[...] * (_dot3(w2_ref[...], h1) + b2_ref[...]))

    h2 = h_ref[...]
    rows = o_ref.shape[0]
    fwd = _dot3(w3_ref[0], h2[:, :n]) + b3_ref[0]
    bwd = _dot3(w3_ref[1], h2[:, n:]) + b3_ref[1]
    decay = jnp.exp(-t_ref[...] * dl_ref[...])
    lane = lax.broadcasted_iota(jnp.int32, (rows, n), 1)
    kf = fwd * decay[:, :n]
    kb = jnp.where(lane == 0, 0.0, bwd * decay[:, n:])
    inv = 1.0 / (jnp.sum(jnp.abs(kf), axis=-1, keepdims=True) + jnp.sum(jnp.abs(kb), axis=-1, keepdims=True))
    kf = kf * inv
    kb = kb * inv
    o_ref[:, 0:LANES] = kb[:, n - LANES:]
    o_ref[:, LANES:LANES + n] = kf
    o_ref[:, LANES + n:] = kb


def _hyena_filters(n, w1, b1, fr1, w2, b2, fr2, w3, b3):
    emb, ffn = w1.shape
    c = w3.shape[1] // (2 * HY_ORDER)
    bands = (emb - 1) // 2
    t = jnp.linspace(0.0, 1.0, n, dtype=F32)[:, None]
    ang = (2.0 * math.pi / n) * jnp.arange(n, dtype=F32)[:, None] * jnp.linspace(1e-4, bands - 1, bands, dtype=F32)[None, :]
    z = jnp.concatenate([t, jnp.cos(ang), -jnp.sin(ang)], axis=-1)
    rev = (n - jnp.arange(n)) % n
    z2 = jnp.concatenate([z, z[rev]], axis=0)
    t2 = jnp.concatenate([t, t[rev]], axis=0).reshape(1, 2 * n)
    emb_p = ((emb + SUBLANES - 1) // SUBLANES) * SUBLANES
    zt = jnp.zeros((emb_p, 2 * n), F32).at[:emb].set(z2.T)
    w1t = jnp.zeros((ffn, emb_p), F32).at[:, :emb].set(w1.T)
    deltas = jnp.abs(jnp.linspace(HY_MIN_DECAY, HY_MAX_DECAY, c, dtype=F32))
    w3t = w3.T.reshape(HY_ORDER, 2, c, ffn).transpose(1, 0, 2, 3).reshape(2, HY_ORDER * c, ffn)
    b3t = b3.reshape(HY_ORDER, 2, c).transpose(1, 0, 2).reshape(2, HY_ORDER * c, 1)
    dl = jnp.tile(deltas, HY_ORDER).reshape(HY_ORDER * c, 1)
    rows = LANES
    kern = functools.partial(_filter_kernel, n=n)
    col = lambda v: v.reshape(ffn, 1)
    return pl.pallas_call(
        kern,
        out_shape=jax.ShapeDtypeStruct((HY_ORDER * c, LANES + 2 * n), F32),
        grid=(HY_ORDER * c // rows,),
        in_specs=[pl.BlockSpec(zt.shape, lambda i: (0, 0)),
                  pl.BlockSpec(t2.shape, lambda i: (0, 0)),
                  pl.BlockSpec(w1t.shape, lambda i: (0, 0)),
                  pl.BlockSpec((ffn, 1), lambda i: (0, 0)),
                  pl.BlockSpec((ffn, 1), lambda i: (0, 0)),
                  pl.BlockSpec((ffn, ffn), lambda i: (0, 0)),
                  pl.BlockSpec((ffn, 1), lambda i: (0, 0)),
                  pl.BlockSpec((ffn, 1), lambda i: (0, 0)),
                  pl.BlockSpec((2, rows, ffn), lambda i: (0, i, 0)),
                  pl.BlockSpec((2, rows, 1), lambda i: (0, i, 0)),
                  pl.BlockSpec((rows, 1), lambda i: (i, 0))],
        out_specs=pl.BlockSpec((rows, LANES + 2 * n), lambda i: (i, 0)),
        scratch_shapes=[pltpu.VMEM((ffn, 2 * n), F32)],
        compiler_params=_params(("arbitrary",)),
        name="hyena_filters",
    )(zt, t2, w1t, col(b1), col(fr1), w2.T, col(b2), col(fr2), w3t, b3t, dl)


def _longconv_kernel(zero_ref, z_ref, gate_ref, kc_ref, kcn_ref, bias_ref, o_ref, r_a, r_b, zs_ref,
                     *, nb, bsz, cb):
    p = LC_P
    n2 = 2 * nb * p
    w = min(LC_W, n2)
    nchunk = n2 // w
    dots_per_chunk = nb // nchunk

    def build_chunk(src, r_dst, ci):
        k_ref, c = src
        win = k_ref[pl.ds(c, 1), ci * w:ci * w + w + LANES]
        rolled = pltpu.roll(jnp.broadcast_to(win, (p, w + LANES)), 0, 1, stride=1, stride_axis=0)
        r_dst[:, ci * w:(ci + 1) * w] = rolled[:, LANES:].astype(BF16)
        bits = pltpu.bitcast(rolled[0:SUBLANES, LANES:2 * LANES], jnp.int32) & zero_ref[...]
        return pltpu.bitcast(bits, F32)[0:1, :].astype(BF16)

    def conv(c, r_src, c_next, r_next):
        for s1 in range(nb):
            zs_ref[s1 * bsz:(s1 + 1) * bsz, :] = z_ref[0, s1, pl.ds(c, bsz, stride=cb), :]
        acc = [None] * nb
        held = None
        for pi in range(nb):
            if pi % dots_per_chunk == 0:
                held = build_chunk(c_next, r_next, pi // dots_per_chunk)
            d = -nb + 2 * pi
            off = (d % (2 * nb)) * p
            lo = max(0, -d - 1)
            hi = min(nb, nb - d)
            lhs = zs_ref[lo * bsz:hi * bsz, :].astype(BF16) + held
            out = _dot(lhs, r_src[:, off:off + 2 * p])
            for k in range(2):
                dk = d + k
                for s1 in range(max(0, -dk), min(nb, nb - dk)):
                    blk = out[(s1 - lo) * bsz:(s1 - lo + 1) * bsz, k * p:(k + 1) * p]
                    acc[s1 + dk] = blk if acc[s1 + dk] is None else acc[s1 + dk] + blk
        bias = bias_ref[c]
        for s1 in range(nb):
            rows = slice(s1 * bsz, (s1 + 1) * bsz)
            where = pl.ds(c, bsz, stride=cb)
            o_ref[0, s1, where, :] = gate_ref[0, s1, where, :] * (acc[s1] + zs_ref[rows, :] * bias)

    @pl.when(pl.program_id(0) == 0)
    def _():
        for ci in range(nchunk):
            build_chunk((kc_ref, 0), r_a, ci)

    def pair(k, carry):
        c = 2 * k
        conv(c, r_a, (kc_ref, c + 1), r_b)
        conv(c + 1, r_b, (kc_ref, c + 2), r_a)
        return carry

    lax.fori_loop(0, cb // 2 - 1, pair, 0)
    conv(cb - 2, r_a, (kc_ref, cb - 1), r_b)
    conv(cb - 1, r_b, (kcn_ref, 0), r_a)


def _longconv(z_cm, z_g0, gate_cm, gate_g0, kc_ext, bias_cm, order, groups, nb, bsz):
    cb = SUBLANES
    rows, p = z_cm.shape[2:]
    blk = (1, nb, rows, p)
    kern = functools.partial(_longconv_kernel, nb=nb, bsz=bsz, cb=cb)
    return pl.pallas_call(
        kern,
        out_shape=jax.ShapeDtypeStruct((groups, nb, rows, p), F32),
        grid=(groups,),
        in_specs=[pl.BlockSpec((1, LANES), lambda i: (0, 0)),
                  pl.BlockSpec(blk, lambda i: (z_g0 + i, 0, 0, 0)),
                  pl.BlockSpec(blk, lambda i: (gate_g0 + i, 0, 0, 0)),
                  pl.BlockSpec((cb, kc_ext.shape[1]), lambda i: (order * groups + i, 0)),
                  pl.BlockSpec((cb, kc_ext.shape[1]), lambda i: (order * groups + jnp.minimum(i + 1, groups - 1), 0)),
                  pl.BlockSpec((cb, 1, p), lambda i: (order * groups + i, 0, 0))],
        out_specs=pl.BlockSpec(blk, lambda i: (i, 0, 0, 0)),
        scratch_shapes=[pltpu.VMEM((p, 2 * nb * p), BF16), pltpu.VMEM((p, 2 * nb * p), BF16),
                        pltpu.VMEM((nb * bsz, p), F32)],
        compiler_params=_params(("arbitrary",)),
        name="hyena_longconv",
    )(jnp.zeros((1, LANES), jnp.int32), z_cm, gate_cm, kc_ext, kc_ext, bias_cm)


def _outproj_kernel(x_ref, g1_ref, a_ref, b_ref, w_ref, o_ref):
    da = a_ref.shape[1]
    ya = _dot(a_ref[...].astype(BF16), w_ref[:da, :])
    groups, nblk, cg, p = b_ref.shape
    parts = []
    for blk in range(nblk):
        cm = b_ref[:, blk, :, :].reshape(groups * cg, p)
        parts.append(_dot(cm.T.astype(BF16), w_ref[da:, :]))
    y = ya + jnp.concatenate(parts, axis=0)
    o_ref[...] = x_ref[...] + g1_ref[0] * y


def _outproj(x2, g1, oa, ob, w, seq, tm=512):
    t, d = x2.shape
    tpb = seq // tm
    return pl.pallas_call(
        _outproj_kernel,
        out_shape=jax.ShapeDtypeStruct((t, d), F32),
        grid=(t // tm,),
        in_specs=[pl.BlockSpec((tm, d), lambda i: (i, 0)),
                  pl.BlockSpec((1, 1, d), lambda i: (i // tpb, 0, 0)),
                  pl.BlockSpec((tm, oa.shape[1]), lambda i: (i, 0)),
                  pl.BlockSpec((ob.shape[0], tm // LANES, SUBLANES, LANES), lambda i: (0, i % tpb, i // tpb, 0)),
                  pl.BlockSpec(w.shape, lambda i: (0, 0))],
        out_specs=pl.BlockSpec((tm, d), lambda i: (i, 0)),
        compiler_params=_params(("parallel",)),
        name="outproj0",
    )(x2, g1, oa, ob, w)


def _confin_kernel(x_ref, g_ref, sh_ref, sc_ref, w_ref, b_ref, o_ref):
    h = _norm_mod(x_ref[...], g_ref[...], sh_ref[0], sc_ref[0]).astype(BF16)
    n = o_ref.shape[1]
    a = _dot(h, w_ref[:, :n]) + b_ref[:, :n]
    gt = _dot(h, w_ref[:, n:]) + b_ref[:, n:]
    o_ref[...] = a * jax.nn.sigmoid(gt)


def _confin(x2, g, sh, sc, w1, b1, seq, tm=512):
    t, d = x2.shape
    n = w1.shape[1] // 2
    tpb = seq // tm
    return pl.pallas_call(
        _confin_kernel,
        out_shape=jax.ShapeDtypeStruct((t, n), F32),
        grid=(t // tm,),
        in_specs=[pl.BlockSpec((tm, d), lambda i: (i, 0)),
                  pl.BlockSpec((1, d), lambda i: (0, 0)),
                  pl.BlockSpec((1, 1, d), lambda i: (i // tpb, 0, 0)),
                  pl.BlockSpec((1, 1, d), lambda i: (i // tpb, 0, 0)),
                  pl.BlockSpec(w1.shape, lambda i: (0, 0)),
                  pl.BlockSpec((1, 2 * n), lambda i: (0, 0))],
        out_specs=pl.BlockSpec((tm, n), lambda i: (i, 0)),
        compiler_params=_params(("parallel",)),
        name="conformer_in",
    )(x2, g, sh, sc, w1, b1.reshape(1, 2 * n))


def _confout_kernel(x_ref, g1_ref, a_ref, lg_ref, lb_ref, w_ref, b_ref, o_ref):
    a = a_ref[...]
    mu = jnp.mean(a, axis=-1, keepdims=True)
    ac = a - mu
    var = jnp.mean(ac * ac, axis=-1, keepdims=True)
    y = ac * lax.rsqrt(var + EPS) * lg_ref[...] + lb_ref[...]
    y = y * jax.nn.sigmoid(y)
    o_ref[...] = x_ref[...] + g1_ref[0] * (_dot(y.astype(BF16), w_ref[...]) + b_ref[...])


def _confout(x2, g1, a2, ln_g, ln_b, w2, b2, seq, tm=512):
    t, d = x2.shape
    n = a2.shape[1]
    tpb = seq // tm
    return pl.pallas_call(
        _confout_kernel,
        out_shape=jax.ShapeDtypeStruct((t, d), F32),
        grid=(t // tm,),
        in_specs=[pl.BlockSpec((tm, d), lambda i: (i, 0)),
                  pl.BlockSpec((1, 1, d), lambda i: (i // tpb, 0, 0)),
                  pl.BlockSpec((tm, n), lambda i: (i, 0)),
                  pl.BlockSpec((1, n), lambda i: (0, 0)),
                  pl.BlockSpec((1, n), lambda i: (0, 0)),
                  pl.BlockSpec(w2.shape, lambda i: (0, 0)),
                  pl.BlockSpec((1, d), lambda i: (0, 0))],
        out_specs=pl.BlockSpec((tm, d), lambda i: (i, 0)),
        compiler_params=_params(("parallel",)),
        name="conformer_out",
    )(x2, g1, a2, ln_g.reshape(1, n), ln_b.reshape(1, n), w2, b2.reshape(1, d))


def _router_kernel(x_ref, g_ref, sh_ref, sc_ref, wr_ref, br_ref, h_ref, route_ref, cnt_ref, run_ref):
    i = pl.program_id(0)

    @pl.when(i == 0)
    def _():
        run_ref[...] = jnp.zeros_like(run_ref)

    h = _norm_mod(x_ref[...], g_ref[...], sh_ref[0], sc_ref[0])
    hh, hl = _split(h)
    h_ref[...] = hh
    logits = _dot(hh, wr_ref[0]) + _dot(hl, wr_ref[0]) + _dot(hh, wr_ref[1]) + br_ref[...]
    tm = logits.shape[0]
    lane = lax.broadcasted_iota(jnp.int32, (tm, LANES), 1)
    ninf = jnp.float32(-jnp.inf)

    def first_argmax(v, m):
        return jnp.min(jnp.where(v == m, lane, LANES), axis=-1, keepdims=True)

    gl = jnp.where(lane < N_GROUPS, logits, ninf)
    gmax = jnp.max(gl, axis=-1, keepdims=True)
    g_w = 1.0 / jnp.sum(jnp.exp(gl - gmax), axis=-1, keepdims=True)
    g_idx = first_argmax(gl, gmax)
    e_lo = ROUTE_LANE0 + EXPERTS_PER_GROUP * g_idx
    el = jnp.where((lane >= e_lo) & (lane < e_lo + EXPERTS_PER_GROUP), logits, ninf)
    m1 = jnp.max(el, axis=-1, keepdims=True)
    esum = jnp.sum(jnp.exp(el - m1), axis=-1, keepdims=True)
    i1 = first_argmax(el, m1)
    el2 = jnp.where(lane == i1, ninf, el)
    m2 = jnp.max(el2, axis=-1, keepdims=True)
    i2 = first_argmax(el2, m2)
    p1 = 1.0 / esum
    p2 = jnp.exp(m2 - m1) / esum
    w1 = g_w * (p1 / (p1 + p2))
    w2 = g_w * (p2 / (p1 + p2))

    oh = jnp.where((lane == i1) | (lane == i2), 1.0, 0.0)
    r_i = lax.broadcasted_iota(jnp.int32, (tm, tm), 0)
    c_i = lax.broadcasted_iota(jnp.int32, (tm, tm), 1)
    tri = jnp.where(c_i < r_i, 1.0, 0.0).astype(BF16)
    before = _dot(tri, oh.astype(BF16)) + run_ref[...]
    rank1 = jnp.sum(jnp.where(lane == i1, before, 0.0), axis=-1, keepdims=True)
    rank2 = jnp.sum(jnp.where(lane == i2, before, 0.0), axis=-1, keepdims=True)
    run_ref[...] = run_ref[...] + jnp.sum(oh, axis=0, keepdims=True)
    cnt_ref[...] = run_ref[...]

    e1 = (i1 - ROUTE_LANE0).astype(F32)
    e2 = (i2 - ROUTE_LANE0).astype(F32)
    vals = (e1, e2, rank1, rank2, w1, w2)
    out = jnp.zeros((tm, LANES), F32)
    for k, v in enumerate(vals):
        out = jnp.where(lane == k, v, out)
    route_ref[...] = out


def _router(x2, g, sh, sc, wr, br, seq, tm=512):
    t, d = x2.shape
    tpb = seq // tm
    return pl.pallas_call(
        _router_kernel,
        out_shape=(jax.ShapeDtypeStruct((t, d), BF16), jax.ShapeDtypeStruct((t, LANES), F32),
                   jax.ShapeDtypeStruct((1, LANES), F32)),
        grid=(t // tm,),
        in_specs=[pl.BlockSpec((tm, d), lambda i: (i, 0)),
                  pl.BlockSpec((1, d), lambda i: (0, 0)),
                  pl.BlockSpec((1, 1, d), lambda i: (i // tpb, 0, 0)),
                  pl.BlockSpec((1, 1, d), lambda i: (i // tpb, 0, 0)),
                  pl.BlockSpec(wr.shape, lambda i: (0, 0, 0)),
                  pl.BlockSpec((1, LANES), lambda i: (0, 0))],
        out_specs=(pl.BlockSpec((tm, d), lambda i: (i, 0)), pl.BlockSpec((tm, LANES), lambda i: (i, 0)),
                   pl.BlockSpec((1, LANES), lambda i: (0, 0))),
        scratch_shapes=[pltpu.VMEM((1, LANES), F32)],
        compiler_params=_params(("arbitrary",)),
        name="moe_router",
    )(x2, g, sh, sc, wr, br)


def _expert_kernel(te_ref, nv_ref, x_ref, wg_ref, wu_ref, wd_ref, o_ref, wg_s, wu_s, wd_s):
    i = pl.program_id(0)

    @pl.when((i == 0) | (te_ref[i] != te_ref[jnp.maximum(i - 1, 0)]))
    def _():
        wg_s[...] = wg_ref[0].astype(BF16)
        wu_s[...] = wu_ref[0].astype(BF16)
        wd_s[...] = wd_ref[0].astype(BF16)

    @pl.when(i < nv_ref[0])
    def _():
        x = x_ref[...]
        a = _dot(x, wg_s[...])
        u = _dot(x, wu_s[...])
        he = (a * jax.nn.sigmoid(a)) * u
        o_ref[...] = _dot(he.astype(BF16), wd_s[...]).astype(o_ref.dtype)


def _experts(xs, tile_expert, n_valid, wg, wu, wd, layer):
    r, d = xs.shape
    de = wg.shape[2]
    nt = r // MOE_TILE
    row = lambda i, te, nv: (jnp.minimum(i, nv[0] - 1), 0)
    wsel = lambda i, te, nv: (layer * N_EXPERTS + te[i], 0, 0)
    return pl.pallas_call(
        _expert_kernel,
        out_shape=jax.ShapeDtypeStruct((r, d), BF16),
        grid_spec=pltpu.PrefetchScalarGridSpec(
            num_scalar_prefetch=2,
            grid=(nt,),
            in_specs=[pl.BlockSpec((MOE_TILE, d), row),
                      pl.BlockSpec((1, d, de), wsel),
                      pl.BlockSpec((1, d, de), wsel),
                      pl.BlockSpec((1, de, d), wsel)],
            out_specs=pl.BlockSpec((MOE_TILE, d), row),
            scratch_shapes=[pltpu.VMEM((d, de), BF16), pltpu.VMEM((d, de), BF16), pltpu.VMEM((de, d), BF16)]),
        compiler_params=_params(("arbitrary",)),
        name="moe_experts",
    )(tile_expert, n_valid, xs, wg, wu, wd)


def _combine_kernel(x_ref, g2_ref, route_ref, y1_ref, y2_ref, fg_ref, o_ref, *, final):
    r = route_ref[...]
    w1 = r[:, 4:5]
    w2 = r[:, 5:6]
    x = x_ref[...] + g2_ref[0] * (w1 * y1_ref[...].astype(F32) + w2 * y2_ref[...].astype(F32))
    if final:
        x = (x * lax.rsqrt(jnp.mean(x * x, axis=-1, keepdims=True) + EPS)) * fg_ref[...]
    o_ref[...] = x


def _combine(x2, g2, route, y12, final_g, seq, final, tm=512):
    t, d = x2.shape
    tpb = seq // tm
    nt = t // tm
    kern = functools.partial(_combine_kernel, final=final)
    return pl.pallas_call(
        kern,
        out_shape=jax.ShapeDtypeStruct((t, d), F32),
        grid=(t // tm,),
        in_specs=[pl.BlockSpec((tm, d), lambda i: (i, 0)),
                  pl.BlockSpec((1, 1, d), lambda i: (i // tpb, 0, 0)),
                  pl.BlockSpec((tm, LANES), lambda i: (i, 0)),
                  pl.BlockSpec((tm, d), lambda i: (i, 0)),
                  pl.BlockSpec((tm, d), lambda i: (nt + i, 0)),
                  pl.BlockSpec((1, d), lambda i: (0, 0))],
        out_specs=pl.BlockSpec((tm, d), lambda i: (i, 0)),
        compiler_params=_params(("parallel",)),
        name="moe_combine",
    )(x2, g2, route, y12, y12, final_g)


def _moe(x2, g, sh, sc, g2, wg_r, bg_r, we_r, be_r, w_gate, w_up, w_down, layer, final_g, seq, final):
    t, d = x2.shape
    wr = jnp.zeros((d, LANES), F32).at[:, :N_GROUPS].set(wg_r).at[:, ROUTE_LANE0:ROUTE_LANE0 + N_EXPERTS].set(we_r)
    wr_hi = wr.astype(BF16)
    wr_lo = (wr - wr_hi.astype(F32)).astype(BF16)
    br = jnp.zeros((1, LANES), F32).at[0, :N_GROUPS].set(bg_r).at[0, ROUTE_LANE0:ROUTE_LANE0 + N_EXPERTS].set(be_r)
    h, route, cnt = _router(x2, g, sh, sc, jnp.stack([wr_hi, wr_lo]), br, seq)

    counts = cnt[0, ROUTE_LANE0:ROUTE_LANE0 + N_EXPERTS].astype(jnp.int32)
    tiles = (counts + MOE_TILE - 1) // MOE_TILE
    tile_end = jnp.cumsum(tiles)
    offs = (tile_end - tiles) * MOE_TILE
    nt = (2 * t) // MOE_TILE + N_EXPERTS
    tile_id = jnp.minimum(jnp.arange(nt, dtype=jnp.int32), tile_end[-1] - 1)
    tile_expert = jnp.sum((tile_id[:, None] >= tile_end[None, :]).astype(jnp.int32), axis=1)
    n_valid = tile_end[-1:].astype(jnp.int32)
    e12 = route[:, 0:2].astype(jnp.int32)
    offs12 = jnp.sum(jnp.where(e12[:, :, None] == jnp.arange(N_EXPERTS)[None, None, :], offs[None, None, :], 0), axis=-1)
    pos = (offs12 + route[:, 2:4].astype(jnp.int32)).T.reshape(-1)
    tok = jnp.tile(jnp.arange(t, dtype=jnp.int32), 2)
    sorted_tok = (jnp.arange(nt * MOE_TILE, dtype=jnp.int32) % t).at[pos].set(
        tok, unique_indices=True, indices_are_sorted=False, mode="promise_in_bounds")
    xs = h.at[sorted_tok].get(mode="promise_in_bounds")
    ys = _experts(xs, tile_expert, n_valid, w_gate, w_up, w_down, layer)
    y12 = ys.at[pos].get(mode="promise_in_bounds")
    return _combine(x2, g2, route, y12, final_g, seq, final)


def _rope_tables(seq, width):
    hd = DA_HEAD_DIM
    half = hd // 2
    quarter = half // 2
    pos = jnp.arange(seq)
    row = (pos // GRID_W).astype(F32)
    col = (pos % GRID_W).astype(F32)
    inv = ROPE_THETA ** (-jnp.arange(0, half, 2, dtype=F32) / half)
    i = jnp.arange(hd)
    p = jnp.where((i < half)[None, :], row[:, None], col[:, None])
    ang = p * inv[i % quarter][None, :]
    sign = jnp.where((i % half) < quarter, -1.0, 1.0)[None, :]
    reps = width // hd
    return jnp.tile(jnp.cos(ang), (1, reps)), jnp.tile(jnp.sin(ang) * sign, (1, reps))


def kernel(x, c, ctx, c_ctx, ada_w, ada_b, norm1_g, norm2_g, final_g, w_in0, w_out0, lam_q1, lam_k1, lam_q2, lam_k2, subln_g, hy_short_w, hy_short_b, hy_w1, hy_b1, hy_fr1, hy_w2, hy_b2, hy_fr2, hy_w3, hy_b3, hy_bias, cv_w1, cv_b1, cv_dw_w, cv_dw_b, cv_ln_g, cv_ln_b, cv_w2, cv_b2, moe_wg, moe_bg, moe_we, moe_be, moe_w_gate, moe_w_up, moe_w_down):
    bsz, seq, d = x.shape
    lctx = ctx.shape[1]
    depth = ada_w.shape[0]
    t = bsz * seq
    hyw = d - DA_WIDTH
    x2 = x.reshape(t, d)

    rows = ((bsz + 1 + SUBLANES - 1) // SUBLANES) * SUBLANES
    cs = jnp.zeros((rows, d), F32).at[:bsz].set(c).at[bsz].set(c_ctx)
    mods = _ada(cs, ada_w, ada_b)

    def tok_mod(i, k):
        return mods[i, :bsz, k * d:(k + 1) * d].reshape(bsz, 1, d)

    def ctx_mod(i, k):
        return mods[i, bsz:bsz + 1, k * d:(k + 1) * d].reshape(1, 1, d)

    for i in range(depth):
        j = i // 2
        g_n1 = norm1_g[i].reshape(1, d)
        if i % 2 == 0:
            assert not any(m % 2 == 0 for m in range(i + 1, depth)), "context-stream update is not implemented"
            lam_init = 0.8 - 0.6 * math.exp(-0.3 * i)
            w_in = w_in0[j]
            cw = 2 * LANES
            cos, sin = _rope_tables(seq, cw)
            q, k, v, u = _inproj(x2, g_n1, tok_mod(i, 0), tok_mod(i, 1), cos, sin, w_in.astype(BF16), seq)
            kc, vc = _ctxproj(ctx.reshape(bsz * lctx, d), g_n1, ctx_mod(i, 0), ctx_mod(i, 1),
                              w_in[:, DA_WIDTH:3 * DA_WIDTH].astype(BF16), lctx)
            lam_p = jnp.stack([lam_q1[j], lam_k1[j], lam_q2[j], lam_k2[j]])
            k_all = jnp.concatenate([kc.reshape(bsz, lctx, DA_WIDTH), k.reshape(bsz, seq, DA_WIDTH)], axis=1)
            v_all = jnp.concatenate([vc.reshape(bsz, lctx, DA_WIDTH), v.reshape(bsz, seq, DA_WIDTH)], axis=1)
            o_a = _attention(q, jnp.swapaxes(k_all, 1, 2), v_all, lam_p, subln_g[j].reshape(1, -1), seq, lam_init)

            nb = seq // LC_P
            groups = hyw // SUBLANES
            ucm = _dwconv(u.reshape(bsz, seq, 3 * hyw), hy_short_w[j], hy_short_b[j], channel_major=True)
            kc_ext = _hyena_filters(seq, hy_w1[j], hy_b1[j], hy_fr1[j], hy_w2[j], hy_b2[j], hy_fr2[j],
                                    hy_w3[j], hy_b3[j])
            bias_cm = jnp.broadcast_to(hy_bias[j].reshape(HY_ORDER * hyw, 1, 1), (HY_ORDER * hyw, 1, LC_P))
            z1 = _longconv(ucm, 0, ucm, groups, kc_ext, bias_cm, 0, groups, nb, bsz)
            z2 = _longconv(z1, 0, ucm, 2 * groups, kc_ext, bias_cm, 1, groups, nb, bsz)
            x2 = _outproj(x2, tok_mod(i, 2), o_a, z2, w_out0[j].astype(BF16), seq)
        else:
            a = _confin(x2, g_n1, tok_mod(i, 0), tok_mod(i, 1), cv_w1[j].astype(BF16), cv_b1[j], seq)
            a = _dwconv(a.reshape(bsz, seq, -1), cv_dw_w[j], cv_dw_b[j]).reshape(t, -1)
            x2 = _confout(x2, tok_mod(i, 2), a, cv_ln_g[j], cv_ln_b[j], cv_w2[j].astype(BF16), cv_b2[j], seq)
        x2 = _moe(x2, norm2_g[i].reshape(1, d), tok_mod(i, 3), tok_mod(i, 4), tok_mod(i, 5),
                  moe_wg[i], moe_bg[i], moe_we[i], moe_be[i],
                  moe_w_gate.reshape((-1,) + moe_w_gate.shape[2:]), moe_w_up.reshape((-1,) + moe_w_up.shape[2:]),
                  moe_w_down.reshape((-1,) + moe_w_down.shape[2:]), i,
                  final_g.reshape(1, d), seq, final=(i == depth - 1))
    return x2.reshape(bsz, seq, d)
```

```python
import functools
import math

import jax
import jax.numpy as jnp
from jax import lax
from jax.experimental import pallas as pl
from jax.experimental.pallas import tpu as pltpu

F32 = jnp.float32
BF16 = jnp.bfloat16

GRID_W = 64
DA_HEADS = 4
DA_HEAD_DIM = 64
DA_WIDTH = DA_HEADS * 2 * DA_HEAD_DIM
HY_ORDER = 2
HY_TARGET = 1e-2
HY_MIN_DECAY = math.log(HY_TARGET) / 0.3
HY_MAX_DECAY = math.log(HY_TARGET) / 1.5
N_GROUPS = 4
EXPERTS_PER_GROUP = 8
N_EXPERTS = N_GROUPS * EXPERTS_PER_GROUP
ROPE_THETA = 10000.0
EPS = 1e-6

LANES = 128
SUBLANES = 8
VMEM_LIMIT = 52 * 1024 * 1024
ROUTE_LANE0 = N_GROUPS
MOE_TILE = 256
LC_P = 128
LC_W = 1024


def _params(sem, flags=None):
    return pltpu.CompilerParams(dimension_semantics=sem, vmem_limit_bytes=VMEM_LIMIT, flags=flags)


def _split(a):
    hi = a.astype(BF16)
    lo = (a - hi.astype(F32)).astype(BF16)
    return hi, lo


def _dot(a, b):
    return jnp.dot(a, b, preferred_element_type=F32)


def _dot3(a, b):
    ah, al = _split(a)
    bh, bl = _split(b)
    return _dot(ah, bh) + _dot(al, bh) + _dot(ah, bl)


def _norm_mod(x, g, sh, sc):
    y = x * lax.rsqrt(jnp.mean(x * x, axis=-1, keepdims=True) + EPS)
    return (y * g) * (1.0 + sc) + sh


def _ada_kernel(c_ref, w_ref, b_ref, o_ref):
    c = c_ref[...]
    s = c * jax.nn.sigmoid(c)
    o_ref[0] = _dot3(s, w_ref[0]) + b_ref[0]


def _ada(cs, ada_w, ada_b):
    depth, d, n6 = ada_w.shape
    rows = cs.shape[0]
    tn = 1536
    return pl.pallas_call(
        _ada_kernel,
        out_shape=jax.ShapeDtypeStruct((depth, rows, n6), F32),
        grid=(depth, n6 // tn),
        in_specs=[pl.BlockSpec((rows, d), lambda l, j: (0, 0)),
                  pl.BlockSpec((1, d, tn), lambda l, j: (l, 0, j)),
                  pl.BlockSpec((1, 1, tn), lambda l, j: (l, 0, j))],
        out_specs=pl.BlockSpec((1, rows, tn), lambda l, j: (l, 0, j)),
        compiler_params=_params(("parallel", "parallel")),
        name="adaln",
    )(cs, ada_w, ada_b.reshape(depth, 1, n6))


def _inproj_kernel(x_ref, g_ref, sh_ref, sc_ref, cos_ref, sin_ref, w_ref,
                   q_ref, k_ref, v_ref, u_ref, *, d_attn, d_hy):
    h = _norm_mod(x_ref[...], g_ref[...], sh_ref[0], sc_ref[0]).astype(BF16)
    cos = cos_ref[...]
    sin = sin_ref[...]
    cw = cos.shape[1]
    quarter = DA_HEAD_DIM // 4
    lane = lax.broadcasted_iota(jnp.int32, cos.shape, 1)
    first = (lane % (2 * quarter)) < quarter

    def mm(c0, c1):
        return _dot(h, w_ref[:, c0:c1])

    def rope(a):
        partner = jnp.where(first, pltpu.roll(a, cw - quarter, 1), pltpu.roll(a, quarter, 1))
        return a * cos + partner * sin

    for j in range(d_attn // cw):
        q_ref[:, j * cw:(j + 1) * cw] = (rope(mm(j * cw, (j + 1) * cw)) * (DA_HEAD_DIM ** -0.5)).astype(BF16)
        k_ref[:, j * cw:(j + 1) * cw] = rope(mm(d_attn + j * cw, d_attn + (j + 1) * cw)).astype(BF16)
    v_ref[...] = mm(2 * d_attn, 3 * d_attn).astype(BF16)
    for j in range(d_hy // 512):
        u_ref[:, j * 512:(j + 1) * 512] = mm(3 * d_attn + j * 512, 3 * d_attn + (j + 1) * 512)


def _inproj(x2, g, sh, sc, cos, sin, w_ext, seq, tm=512):
    t, d = x2.shape
    tpb = seq // tm
    d_attn = DA_WIDTH
    d_hy = w_ext.shape[1] - 3 * d_attn
    cw = cos.shape[1]
    kern = functools.partial(_inproj_kernel, d_attn=d_attn, d_hy=d_hy)
    return pl.pallas_call(
        kern,
        out_shape=(jax.ShapeDtypeStruct((t, d_attn), BF16), jax.ShapeDtypeStruct((t, d_attn), BF16),
                   jax.ShapeDtypeStruct((t, d_attn), BF16), jax.ShapeDtypeStruct((t, d_hy), F32)),
        grid=(t // tm,),
        in_specs=[pl.BlockSpec((tm, d), lambda i: (i, 0)),
                  pl.BlockSpec((1, d), lambda i: (0, 0)),
                  pl.BlockSpec((1, 1, d), lambda i: (i // tpb, 0, 0)),
                  pl.BlockSpec((1, 1, d), lambda i: (i // tpb, 0, 0)),
                  pl.BlockSpec((tm, cw), lambda i: (i % tpb, 0)),
                  pl.BlockSpec((tm, cw), lambda i: (i % tpb, 0)),
                  pl.BlockSpec(w_ext.shape, lambda i: (0, 0))],
        out_specs=(pl.BlockSpec((tm, d_attn), lambda i: (i, 0)), pl.BlockSpec((tm, d_attn), lambda i: (i, 0)),
                   pl.BlockSpec((tm, d_attn), lambda i: (i, 0)), pl.BlockSpec((tm, d_hy), lambda i: (i, 0))),
        compiler_params=_params(("parallel",)),
        name="inproj0",
    )(x2, g, sh, sc, cos, sin, w_ext)


def _ctxproj_kernel(x_ref, g_ref, sh_ref, sc_ref, w_ref, k_ref, v_ref):
    h = _norm_mod(x_ref[...], g_ref[...], sh_ref[0], sc_ref[0]).astype(BF16)
    n = k_ref.shape[1]
    k_ref[...] = _dot(h, w_ref[:, :n]).astype(BF16)
    v_ref[...] = _dot(h, w_ref[:, n:]).astype(BF16)


def _ctxproj(c2, g, sh, sc, w_kv, tm):
    t, d = c2.shape
    n = w_kv.shape[1] // 2
    return pl.pallas_call(
        _ctxproj_kernel,
        out_shape=(jax.ShapeDtypeStruct((t, n), BF16), jax.ShapeDtypeStruct((t, n), BF16)),
        grid=(t // tm,),
        in_specs=[pl.BlockSpec((tm, d), lambda i: (i, 0)),
                  pl.BlockSpec((1, d), lambda i: (0, 0)),
                  pl.BlockSpec((1, 1, d), lambda i: (0, 0, 0)),
                  pl.BlockSpec((1, 1, d), lambda i: (0, 0, 0)),
                  pl.BlockSpec(w_kv.shape, lambda i: (0, 0))],
        out_specs=(pl.BlockSpec((tm, n), lambda i: (i, 0)), pl.BlockSpec((tm, n), lambda i: (i, 0))),
        compiler_params=_params(("parallel",)),
        name="ctxproj",
    )(c2, g, sh, sc, w_kv)


def _attn_kernel(lam_ref, g_ref, q_ref, kt_ref, v_ref, o_ref, *, lam_init):
    lp = lam_ref[...]
    lam = (jnp.exp(jnp.sum(lp[0:1] * lp[1:2], axis=-1, keepdims=True))
           - jnp.exp(jnp.sum(lp[2:3] * lp[3:4], axis=-1, keepdims=True)) + lam_init)
    hp = 2 * DA_HEAD_DIM
    scores = []
    for p in range(q_ref.shape[1] // hp):
        cols = slice(p * hp, (p + 1) * hp)
        q = q_ref[:, cols]
        lane = lax.broadcasted_iota(jnp.int32, q.shape, 1)
        zero = jnp.zeros_like(q)
        kt = kt_ref[0, cols, :]
        scores.append((_dot(jnp.where(lane < DA_HEAD_DIM, q, zero), kt),
                       _dot(jnp.where(lane >= DA_HEAD_DIM, q, zero), kt)))
    for p, (s1, s2) in enumerate(scores):
        cols = slice(p * hp, (p + 1) * hp)
        e1 = jnp.exp(s1 - jnp.max(s1, axis=-1, keepdims=True))
        e2 = jnp.exp(s2 - jnp.max(s2, axis=-1, keepdims=True))
        r1 = 1.0 / jnp.sum(e1, axis=-1, keepdims=True)
        r2 = lam / jnp.sum(e2, axis=-1, keepdims=True)
        a = (e1 * r1 - e2 * r2).astype(BF16)
        o = _dot(a, v_ref[0, :, cols])
        y = o * lax.rsqrt(jnp.mean(o * o, axis=-1, keepdims=True) + EPS)
        o_ref[:, cols] = (y * g_ref[...]) * (1.0 - lam_init)


def _attention(q, kt_all, v_all, lam_p, subln_g, seq, lam_init, tq=256, pairs=2):
    t = q.shape[0]
    b, _, lk = kt_all.shape
    hw = 2 * DA_HEAD_DIM * pairs
    nq = seq // tq
    kern = functools.partial(_attn_kernel, lam_init=lam_init)
    return pl.pallas_call(
        kern,
        out_shape=jax.ShapeDtypeStruct((t, DA_WIDTH), F32),
        grid=(b, DA_HEADS // pairs, nq),
        in_specs=[pl.BlockSpec(lam_p.shape, lambda bi, h, i: (0, 0)),
                  pl.BlockSpec((1, 2 * DA_HEAD_DIM), lambda bi, h, i: (0, 0)),
                  pl.BlockSpec((tq, hw), lambda bi, h, i: (bi * nq + i, h)),
                  pl.BlockSpec((1, hw, lk), lambda bi, h, i: (bi, h, 0)),
                  pl.BlockSpec((1, lk, hw), lambda bi, h, i: (bi, 0, h))],
        out_specs=pl.BlockSpec((tq, hw), lambda bi, h, i: (bi * nq + i, h)),
        compiler_params=_params(("parallel", "parallel", "parallel")),
        name="diff_attn",
    )(lam_p, subln_g, q, kt_all, v_all)


def _dwconv_kernel(x_ref, w_ref, b_ref, o_ref, pad_ref, *, kw, halo, channel_major):
    s = x_ref.shape[1]
    pl_ = (kw - 1) // 2
    zeros = jnp.zeros((halo, x_ref.shape[2]), F32)
    pad_ref[0:halo, :] = zeros
    pad_ref[halo + s:halo + s + halo, :] = zeros
    pad_ref[halo:halo + s, :] = x_ref[0]
    w = w_ref[...]
    acc = jnp.zeros((s, x_ref.shape[2]), F32) + b_ref[...]
    for j in range(kw):
        off = halo - pl_ + j
        acc = acc + pad_ref[off:off + s, :] * w[j:j + 1, :]
    if channel_major:
        acc_t = acc.T
        for g in range(LANES // SUBLANES):
            for blk in range(s // LANES):
                o_ref[g, blk] = acc_t[g * SUBLANES:(g + 1) * SUBLANES, blk * LANES:(blk + 1) * LANES]
    else:
        o_ref[0] = acc


def _dwconv(x3, w, bias, channel_major=False):
    b, s, c = x3.shape
    kw = w.shape[0]
    halo = 2 * SUBLANES
    kern = functools.partial(_dwconv_kernel, kw=kw, halo=halo, channel_major=channel_major)
    if channel_major:
        gpb = LANES // SUBLANES
        out_shape = jax.ShapeDtypeStruct((c // SUBLANES, s // LANES, b * SUBLANES, LANES), F32)
        out_spec = pl.BlockSpec((gpb, s // LANES, SUBLANES, LANES), lambda bi, ci: (ci, 0, bi, 0))
    else:
        out_shape = jax.ShapeDtypeStruct((b, s, c), F32)
        out_spec = pl.BlockSpec((1, s, LANES), lambda bi, ci: (bi, 0, ci))
    return pl.pallas_call(
        kern,
        out_shape=out_shape,
        grid=(b, c // LANES),
        in_specs=[pl.BlockSpec((1, s, LANES), lambda bi, ci: (bi, 0, ci)),
                  pl.BlockSpec((kw, LANES), lambda bi, ci: (0, ci)),
                  pl.BlockSpec((1, LANES), lambda bi, ci: (0, ci))],
        out_specs=out_spec,
        scratch_shapes=[pltpu.VMEM((s + 2 * halo, LANES), F32)],
        compiler_params=_params(("parallel", "parallel")),
        name="dwconv",
    )(x3, w, bias.reshape(1, c))


def _filter_kernel(zt_ref, t_ref, w1_ref, b1_ref, f1_ref, w2_ref, b2_ref, f2_ref, w3_ref, b3_ref, dl_ref,
                   o_ref, h_ref, *, n):
    @pl.when(pl.program_id(0) == 0)
    def _():
        h1 = jnp.sin(f1_ref[...] * (_dot3(w1_ref[...], zt_ref[...]) + b1_ref[...]))
        h_ref[...] = jnp.sin(f2_ref[...] * (_dot3(w2_ref[...], h1) + b2_ref[...]))

    h2 = h_ref[...]
    rows = o_ref.shape[0]
    fwd = _dot3(w3_ref[0], h2[:, :n]) + b3_ref[0]
    bwd = _dot3(w3_ref[1], h2[:, n:]) + b3_ref[1]
    decay = jnp.exp(-t_ref[...] * dl_ref[...])
    lane = lax.broadcasted_iota(jnp.int32, (rows, n), 1)
    kf = fwd * decay[:, :n]
    kb = jnp.where(lane == 0, 0.0, bwd * decay[:, n:])
    inv = 1.0 / (jnp.sum(jnp.abs(kf), axis=-1, keepdims=True) + jnp.sum(jnp.abs(kb), axis=-1, keepdims=True))
    kf = kf * inv
    kb = kb * inv
    o_ref[:, 0:LANES] = kb[:, n - LANES:]
    o_ref[:, LANES:LANES + n] = kf
    o_ref[:, LANES + n:] = kb


def _hyena_filters(n, w1, b1, fr1, w2, b2, fr2, w3, b3):
    emb, ffn = w1.shape
    c = w3.shape[1] // (2 * HY_ORDER)
    bands = (emb - 1) // 2
    t = jnp.linspace(0.0, 1.0, n, dtype=F32)[:, None]
    ang = (2.0 * math.pi / n) * jnp.arange(n, dtype=F32)[:, None] * jnp.linspace(1e-4, bands - 1, bands, dtype=F32)[None, :]
    z = jnp.concatenate([t, jnp.cos(ang), -jnp.sin(ang)], axis=-1)
    rev = (n - jnp.arange(n)) % n
    z2 = jnp.concatenate([z, z[rev]], axis=0)
    t2 = jnp.concatenate([t, t[rev]], axis=0).reshape(1, 2 * n)
    emb_p = ((emb + SUBLANES - 1) // SUBLANES) * SUBLANES
    zt = jnp.zeros((emb_p, 2 * n), F32).at[:emb].set(z2.T)
    w1t = jnp.zeros((ffn, emb_p), F32).at[:, :emb].set(w1.T)
    deltas = jnp.abs(jnp.linspace(HY_MIN_DECAY, HY_MAX_DECAY, c, dtype=F32))
    w3t = w3.T.reshape(HY_ORDER, 2, c, ffn).transpose(1, 0, 2, 3).reshape(2, HY_ORDER * c, ffn)
    b3t = b3.reshape(HY_ORDER, 2, c).transpose(1, 0, 2).reshape(2, HY_ORDER * c, 1)
    dl = jnp.tile(deltas, HY_ORDER).reshape(HY_ORDER * c, 1)
    rows = LANES
    kern = functools.partial(_filter_kernel, n=n)
    col = lambda v: v.reshape(ffn, 1)
    return pl.pallas_call(
        kern,
        out_shape=jax.ShapeDtypeStruct((HY_ORDER * c, LANES + 2 * n), F32),
        grid=(HY_ORDER * c // rows,),
        in_specs=[pl.BlockSpec(zt.shape, lambda i: (0, 0)),
                  pl.BlockSpec(t2.shape, lambda i: (0, 0)),
                  pl.BlockSpec(w1t.shape, lambda i: (0, 0)),
                  pl.BlockSpec((ffn, 1), lambda i: (0, 0)),
                  pl.BlockSpec((ffn, 1), lambda i: (0, 0)),
                  pl.BlockSpec((ffn, ffn), lambda i: (0, 0)),
                  pl.BlockSpec((ffn, 1), lambda i: (0, 0)),
                  pl.BlockSpec((ffn, 1), lambda i: (0, 0)),
                  pl.BlockSpec((2, rows, ffn), lambda i: (0, i, 0)),
                  pl.BlockSpec((2, rows, 1), lambda i: (0, i, 0)),
                  pl.BlockSpec((rows, 1), lambda i: (i, 0))],
        out_specs=pl.BlockSpec((rows, LANES + 2 * n), lambda i: (i, 0)),
        scratch_shapes=[pltpu.VMEM((ffn, 2 * n), F32)],
        compiler_params=_params(("arbitrary",)),
        name="hyena_filters",
    )(zt, t2, w1t, col(b1), col(fr1), w2.T, col(b2), col(fr2), w3t, b3t, dl)


def _longconv_kernel(zero_ref, z_ref, gate_ref, kc_ref, kcn_ref, bias_ref, o_ref, r_a, r_b, zs_ref,
                     *, nb, bsz, cb):
    p = LC_P
    n2 = 2 * nb * p
    w = min(LC_W, n2)
    nchunk = n2 // w
    dots_per_chunk = nb // nchunk

    def build_chunk(src, r_dst, ci):
        k_ref, c = src
        win = k_ref[pl.ds(c, 1), ci * w:ci * w + w + LANES]
        rolled = pltpu.roll(jnp.broadcast_to(win, (p, w + LANES)), 0, 1, stride=1, stride_axis=0)
        r_dst[:, ci * w:(ci + 1) * w] = rolled[:, LANES:].astype(BF16)
        bits = pltpu.bitcast(rolled[0:SUBLANES, LANES:2 * LANES], jnp.int32) & zero_ref[...]
        return pltpu.bitcast(bits, F32)[0:1, :].astype(BF16)

    def conv(c, r_src, c_next, r_next):
        for s1 in range(nb):
            zs_ref[s1 * bsz:(s1 + 1) * bsz, :] = z_ref[0, s1, pl.ds(c, bsz, stride=cb), :]
        acc = [None] * nb
        held = None
        for pi in range(nb):
            if pi % dots_per_chunk == 0:
                held = build_chunk(c_next, r_next, pi // dots_per_chunk)
            d = -nb + 2 * pi
            off = (d % (2 * nb)) * p
            lo = max(0, -d - 1)
            hi = min(nb, nb - d)
            lhs = zs_ref[lo * bsz:hi * bsz, :].astype(BF16) + held
            out = _dot(lhs, r_src[:, off:off + 2 * p])
            for k in range(2):
                dk = d + k
                for s1 in range(max(0, -dk), min(nb, nb - dk)):
                    blk = out[(s1 - lo) * bsz:(s1 - lo + 1) * bsz, k * p:(k + 1) * p]
                    acc[s1 + dk] = blk if acc[s1 + dk] is None else acc[s1 + dk] + blk
        bias = bias_ref[c]
        for s1 in range(nb):
            rows = slice(s1 * bsz, (s1 + 1) * bsz)
            where = pl.ds(c, bsz, stride=cb)
            o_ref[0, s1, where, :] = gate_ref[0, s1, where, :] * (acc[s1] + zs_ref[rows, :] * bias)

    @pl.when(pl.program_id(0) == 0)
    def _():
        for ci in range(nchunk):
            build_chunk((kc_ref, 0), r_a, ci)

    def pair(k, carry):
        c = 2 * k
        conv(c, r_a, (kc_ref, c + 1), r_b)
        conv(c + 1, r_b, (kc_ref, c + 2), r_a)
        return carry

    lax.fori_loop(0, cb // 2 - 1, pair, 0)
    conv(cb - 2, r_a, (kc_ref, cb - 1), r_b)
    conv(cb - 1, r_b, (kcn_ref, 0), r_a)


def _longconv(z_cm, z_g0, gate_cm, gate_g0, kc_ext, bias_cm, order, groups, nb, bsz):
    cb = SUBLANES
    rows, p = z_cm.shape[2:]
    blk = (1, nb, rows, p)
    kern = functools.partial(_longconv_kernel, nb=nb, bsz=bsz, cb=cb)
    return pl.pallas_call(
        kern,
        out_shape=jax.ShapeDtypeStruct((groups, nb, rows, p), F32),
        grid=(groups,),
        in_specs=[pl.BlockSpec((1, LANES), lambda i: (0, 0)),
                  pl.BlockSpec(blk, lambda i: (z_g0 + i, 0, 0, 0)),
                  pl.BlockSpec(blk, lambda i: (gate_g0 + i, 0, 0, 0)),
                  pl.BlockSpec((cb, kc_ext.shape[1]), lambda i: (order * groups + i, 0)),
                  pl.BlockSpec((cb, kc_ext.shape[1]), lambda i: (order * groups + jnp.minimum(i + 1, groups - 1), 0)),
                  pl.BlockSpec((cb, 1, p), lambda i: (order * groups + i, 0, 0))],
        out_specs=pl.BlockSpec(blk, lambda i: (i, 0, 0, 0)),
        scratch_shapes=[pltpu.VMEM((p, 2 * nb * p), BF16), pltpu.VMEM((p, 2 * nb * p), BF16),
                        pltpu.VMEM((nb * bsz, p), F32)],
        compiler_params=_params(("arbitrary",)),
        name="hyena_longconv",
    )(jnp.zeros((1, LANES), jnp.int32), z_cm, gate_cm, kc_ext, kc_ext, bias_cm)


def _outproj_kernel(x_ref, g1_ref, a_ref, b_ref, w_ref, o_ref):
    da = a_ref.shape[1]
    ya = _dot(a_ref[...].astype(BF16), w_ref[:da, :])
    groups, nblk, cg, p = b_ref.shape
    parts = []
    for blk in range(nblk):
        cm = b_ref[:, blk, :, :].reshape(groups * cg, p)
        parts.append(_dot(cm.T.astype(BF16), w_ref[da:, :]))
    y = ya + jnp.concatenate(parts, axis=0)
    o_ref[...] = x_ref[...] + g1_ref[0] * y


def _outproj(x2, g1, oa, ob, w, seq, tm=512):
    t, d = x2.shape
    tpb = seq // tm
    return pl.pallas_call(
        _outproj_kernel,
        out_shape=jax.ShapeDtypeStruct((t, d), F32),
        grid=(t // tm,),
        in_specs=[pl.BlockSpec((tm, d), lambda i: (i, 0)),
                  pl.BlockSpec((1, 1, d), lambda i: (i // tpb, 0, 0)),
                  pl.BlockSpec((tm, oa.shape[1]), lambda i: (i, 0)),
                  pl.BlockSpec((ob.shape[0], tm // LANES, SUBLANES, LANES), lambda i: (0, i % tpb, i // tpb, 0)),
                  pl.BlockSpec(w.shape, lambda i: (0, 0))],
        out_specs=pl.BlockSpec((tm, d), lambda i: (i, 0)),
        compiler_params=_params(("parallel",)),
        name="outproj0",
    )(x2, g1, oa, ob, w)


def _confin_kernel(x_ref, g_ref, sh_ref, sc_ref, w_ref, b_ref, o_ref):
    h = _norm_mod(x_ref[...], g_ref[...], sh_ref[0], sc_ref[0]).astype(BF16)
    n = o_ref.shape[1]
    a = _dot(h, w_ref[:, :n]) + b_ref[:, :n]
    gt = _dot(h, w_ref[:, n:]) + b_ref[:, n:]
    o_ref[...] = a * jax.nn.sigmoid(gt)


def _confin(x2, g, sh, sc, w1, b1, seq, tm=512):
    t, d = x2.shape
    n = w1.shape[1] // 2
    tpb = seq // tm
    return pl.pallas_call(
        _confin_kernel,
        out_shape=jax.ShapeDtypeStruct((t, n), F32),
        grid=(t // tm,),
        in_specs=[pl.BlockSpec((tm, d), lambda i: (i, 0)),
                  pl.BlockSpec((1, d), lambda i: (0, 0)),
                  pl.BlockSpec((1, 1, d), lambda i: (i // tpb, 0, 0)),
                  pl.BlockSpec((1, 1, d), lambda i: (i // tpb, 0, 0)),
                  pl.BlockSpec(w1.shape, lambda i: (0, 0)),
                  pl.BlockSpec((1, 2 * n), lambda i: (0, 0))],
        out_specs=pl.BlockSpec((tm, n), lambda i: (i, 0)),
        compiler_params=_params(("parallel",)),
        name="conformer_in",
    )(x2, g, sh, sc, w1, b1.reshape(1, 2 * n))


def _confout_kernel(x_ref, g1_ref, a_ref, lg_ref, lb_ref, w_ref, b_ref, o_ref):
    a = a_ref[...]
    mu = jnp.mean(a, axis=-1, keepdims=True)
    ac = a - mu
    var = jnp.mean(ac * ac, axis=-1, keepdims=True)
    y = ac * lax.rsqrt(var + EPS) * lg_ref[...] + lb_ref[...]
    y = y * jax.nn.sigmoid(y)
    o_ref[...] = x_ref[...] + g1_ref[0] * (_dot(y.astype(BF16), w_ref[...]) + b_ref[...])


def _confout(x2, g1, a2, ln_g, ln_b, w2, b2, seq, tm=512):
    t, d = x2.shape
    n = a2.shape[1]
    tpb = seq // tm
    return pl.pallas_call(
        _confout_kernel,
        out_shape=jax.ShapeDtypeStruct((t, d), F32),
        grid=(t // tm,),
        in_specs=[pl.BlockSpec((tm, d), lambda i: (i, 0)),
                  pl.BlockSpec((1, 1, d), lambda i: (i // tpb, 0, 0)),
                  pl.BlockSpec((tm, n), lambda i: (i, 0)),
                  pl.BlockSpec((1, n), lambda i: (0, 0)),
                  pl.BlockSpec((1, n), lambda i: (0, 0)),
                  pl.BlockSpec(w2.shape, lambda i: (0, 0)),
                  pl.BlockSpec((1, d), lambda i: (0, 0))],
        out_specs=pl.BlockSpec((tm, d), lambda i: (i, 0)),
        compiler_params=_params(("parallel",)),
        name="conformer_out",
    )(x2, g1, a2, ln_g.reshape(1, n), ln_b.reshape(1, n), w2, b2.reshape(1, d))


def _router_kernel(x_ref, g_ref, sh_ref, sc_ref, wr_ref, br_ref, tri_ref, h_ref, route_ref, cnt_ref, run_ref):
    i = pl.program_id(0)

    @pl.when(i == 0)
    def _():
        run_ref[...] = jnp.zeros_like(run_ref)

    h = _norm_mod(x_ref[...], g_ref[...], sh_ref[0], sc_ref[0])
    hh, hl = _split(h)
    h_ref[...] = hh
    logits = _dot(hh, wr_ref[0]) + _dot(hl, wr_ref[0]) + _dot(hh, wr_ref[1]) + br_ref[...]
    tm = logits.shape[0]
    lane = lax.broadcasted_iota(jnp.int32, (tm, LANES), 1)
    ninf = jnp.float32(-jnp.inf)

    def first_argmax(v, m):
        return jnp.min(jnp.where(v == m, lane, LANES), axis=-1, keepdims=True)

    gl = jnp.where(lane < N_GROUPS, logits, ninf)
    gmax = jnp.max(gl, axis=-1, keepdims=True)
    g_w = 1.0 / jnp.sum(jnp.exp(gl - gmax), axis=-1, keepdims=True)
    g_idx = first_argmax(gl, gmax)
    e_lo = ROUTE_LANE0 + EXPERTS_PER_GROUP * g_idx
    el = jnp.where((lane >= e_lo) & (lane < e_lo + EXPERTS_PER_GROUP), logits, ninf)
    m1 = jnp.max(el, axis=-1, keepdims=True)
    esum = jnp.sum(jnp.exp(el - m1), axis=-1, keepdims=True)
    i1 = first_argmax(el, m1)
    el2 = jnp.where(lane == i1, ninf, el)
    m2 = jnp.max(el2, axis=-1, keepdims=True)
    i2 = first_argmax(el2, m2)
    p1 = 1.0 / esum
    p2 = jnp.exp(m2 - m1) / esum
    w1 = g_w * (p1 / (p1 + p2))
    w2 = g_w * (p2 / (p1 + p2))

    oh = jnp.where((lane == i1) | (lane == i2), 1.0, 0.0)
    before = _dot(tri_ref[...], oh.astype(BF16)) + run_ref[...]
    rank1 = jnp.sum(jnp.where(lane == i1, before, 0.0), axis=-1, keepdims=True)
    rank2 = jnp.sum(jnp.where(lane == i2, before, 0.0), axis=-1, keepdims=True)
    run_ref[...] = run_ref[...] + jnp.sum(oh, axis=0, keepdims=True)
    cnt_ref[...] = run_ref[...]

    e1 = (i1 - ROUTE_LANE0).astype(F32)
    e2 = (i2 - ROUTE_LANE0).astype(F32)
    vals = (e1, e2, rank1, rank2, w1, w2)
    out = jnp.zeros((tm, LANES), F32)
    for k, v in enumerate(vals):
        out = jnp.where(lane == k, v, out)
    route_ref[...] = out


def _router(x2, g, sh, sc, wr, br, seq, tm=512):
    t, d = x2.shape
    tpb = seq // tm
    return pl.pallas_call(
        _router_kernel,
        out_shape=(jax.ShapeDtypeStruct((t, d), BF16), jax.ShapeDtypeStruct((t, LANES), F32),
                   jax.ShapeDtypeStruct((1, LANES), F32)),
        grid=(t // tm,),
        in_specs=[pl.BlockSpec((tm, d), lambda i: (i, 0)),
                  pl.BlockSpec((1, d), lambda i: (0, 0)),
                  pl.BlockSpec((1, 1, d), lambda i: (i // tpb, 0, 0)),
                  pl.BlockSpec((1, 1, d), lambda i: (i // tpb, 0, 0)),
                  pl.BlockSpec(wr.shape, lambda i: (0, 0, 0)),
                  pl.BlockSpec((1, LANES), lambda i: (0, 0)),
                  pl.BlockSpec((tm, tm), lambda i: (0, 0))],
        out_specs=(pl.BlockSpec((tm, d), lambda i: (i, 0)), pl.BlockSpec((tm, LANES), lambda i: (i, 0)),
                   pl.BlockSpec((1, LANES), lambda i: (0, 0))),
        scratch_shapes=[pltpu.VMEM((1, LANES), F32)],
        compiler_params=_params(("arbitrary",)),
        name="moe_router",
    )(x2, g, sh, sc, wr, br, jnp.tril(jnp.ones((tm, tm), BF16), -1))


def _expert_kernel(te_ref, nv_ref, x_ref, wg_ref, wu_ref, wd_ref, o_ref, wg_s, wu_s, wd_s):
    i = pl.program_id(0)

    @pl.when((i == 0) | (te_ref[i] != te_ref[jnp.maximum(i - 1, 0)]))
    def _():
        wg_s[...] = wg_ref[0].astype(BF16)
        wu_s[...] = wu_ref[0].astype(BF16)
        wd_s[...] = wd_ref[0].astype(BF16)

    @pl.when(i < nv_ref[0])
    def _():
        x = x_ref[...]
        a = _dot(x, wg_s[...])
        u = _dot(x, wu_s[...])
        he = (a * jax.nn.sigmoid(a)) * u
        o_ref[...] = _dot(he.astype(BF16), wd_s[...]).astype(o_ref.dtype)


def _experts(xs, tile_expert, n_valid, wg, wu, wd, layer):
    r, d = xs.shape
    de = wg.shape[2]
    nt = r // MOE_TILE
    row = lambda i, te, nv: (jnp.minimum(i, nv[0] - 1), 0)
    wsel = lambda i, te, nv: (layer * N_EXPERTS + te[i], 0, 0)
    return pl.pallas_call(
        _expert_kernel,
        out_shape=jax.ShapeDtypeStruct((r, d), BF16),
        grid_spec=pltpu.PrefetchScalarGridSpec(
            num_scalar_prefetch=2,
            grid=(nt,),
            in_specs=[pl.BlockSpec((MOE_TILE, d), row),
                      pl.BlockSpec((1, d, de), wsel),
                      pl.BlockSpec((1, d, de), wsel),
                      pl.BlockSpec((1, de, d), wsel)],
            out_specs=pl.BlockSpec((MOE_TILE, d), row),
            scratch_shapes=[pltpu.VMEM((d, de), BF16), pltpu.VMEM((d, de), BF16), pltpu.VMEM((de, d), BF16)]),
        compiler_params=_params(("arbitrary",)),
        name="moe_experts",
    )(tile_expert, n_valid, xs, wg, wu, wd)


def _combine_kernel(x_ref, g2_ref, route_ref, y1_ref, y2_ref, fg_ref, o_ref, *, final):
    r = route_ref[...]
    w1 = r[:, 4:5]
    w2 = r[:, 5:6]
    x = x_ref[...] + g2_ref[0] * (w1 * y1_ref[...].astype(F32) + w2 * y2_ref[...].astype(F32))
    if final:
        x = (x * lax.rsqrt(jnp.mean(x * x, axis=-1, keepdims=True) + EPS)) * fg_ref[...]
    o_ref[...] = x


def _combine(x2, g2, route, y12, final_g, seq, final, tm=512):
    t, d = x2.shape
    tpb = seq // tm
    nt = t // tm
    kern = functools.partial(_combine_kernel, final=final)
    return pl.pallas_call(
        kern,
        out_shape=jax.ShapeDtypeStruct((t, d), F32),
        grid=(t // tm,),
        in_specs=[pl.BlockSpec((tm, d), lambda i: (i, 0)),
                  pl.BlockSpec((1, 1, d), lambda i: (i // tpb, 0, 0)),
                  pl.BlockSpec((tm, LANES), lambda i: (i, 0)),
                  pl.BlockSpec((tm, d), lambda i: (i, 0)),
                  pl.BlockSpec((tm, d), lambda i: (nt + i, 0)),
                  pl.BlockSpec((1, d), lambda i: (0, 0))],
        out_specs=pl.BlockSpec((tm, d), lambda i: (i, 0)),
        compiler_params=_params(("parallel",)),
        name="moe_combine",
    )(x2, g2, route, y12, y12, final_g)


def _moe(x2, g, sh, sc, g2, wg_r, bg_r, we_r, be_r, w_gate, w_up, w_down, layer, final_g, seq, final):
    t, d = x2.shape
    wr = jnp.zeros((d, LANES), F32).at[:, :N_GROUPS].set(wg_r).at[:, ROUTE_LANE0:ROUTE_LANE0 + N_EXPERTS].set(we_r)
    wr_hi = wr.astype(BF16)
    wr_lo = (wr - wr_hi.astype(F32)).astype(BF16)
    br = jnp.zeros((1, LANES), F32).at[0, :N_GROUPS].set(bg_r).at[0, ROUTE_LANE0:ROUTE_LANE0 + N_EXPERTS].set(be_r)
    h, route, cnt = _router(x2, g, sh, sc, jnp.stack([wr_hi, wr_lo]), br, seq)

    counts = cnt[0, ROUTE_LANE0:ROUTE_LANE0 + N_EXPERTS].astype(jnp.int32)
    tiles = (counts + MOE_TILE - 1) // MOE_TILE
    tile_end = jnp.cumsum(tiles)
    offs = (tile_end - tiles) * MOE_TILE
    nt = (2 * t) // MOE_TILE + N_EXPERTS
    tile_id = jnp.minimum(jnp.arange(nt, dtype=jnp.int32), tile_end[-1] - 1)
    tile_expert = jnp.sum((tile_id[:, None] >= tile_end[None, :]).astype(jnp.int32), axis=1)
    n_valid = tile_end[-1:].astype(jnp.int32)
    e12 = route[:, 0:2].astype(jnp.int32)
    offs12 = jnp.sum(jnp.where(e12[:, :, None] == jnp.arange(N_EXPERTS)[None, None, :], offs[None, None, :], 0), axis=-1)
    pos = (offs12 + route[:, 2:4].astype(jnp.int32)).T.reshape(-1)
    tok = jnp.tile(jnp.arange(t, dtype=jnp.int32), 2)
    _, tok_by_row = lax.sort_key_val(pos, tok)
    row = jnp.arange(nt * MOE_TILE, dtype=jnp.int32)
    per_row = lambda v: jnp.repeat(v[tile_expert], MOE_TILE)
    in_e = row - per_row(offs)
    compact = per_row(jnp.cumsum(counts) - counts) + in_e
    sorted_tok = jnp.where(in_e < per_row(counts),
                           tok_by_row.at[jnp.minimum(compact, 2 * t - 1)].get(mode="promise_in_bounds"), row % t)
    xs = h.at[sorted_tok].get(mode="promise_in_bounds")
    ys = _experts(xs, tile_expert, n_valid, w_gate, w_up, w_down, layer)
    y12 = ys.at[pos].get(mode="promise_in_bounds")
    return _combine(x2, g2, route, y12, final_g, seq, final)


def _rope_tables(seq, width):
    hd = DA_HEAD_DIM
    half = hd // 2
    quarter = half // 2
    pos = jnp.arange(seq)
    row = (pos // GRID_W).astype(F32)
    col = (pos % GRID_W).astype(F32)
    inv = ROPE_THETA ** (-jnp.arange(0, half, 2, dtype=F32) / half)
    i = jnp.arange(hd)
    p = jnp.where((i < half)[None, :], row[:, None], col[:, None])
    ang = p * inv[i % quarter][None, :]
    sign = jnp.where((i % half) < quarter, -1.0, 1.0)[None, :]
    reps = width // hd
    return jnp.tile(jnp.cos(ang), (1, reps)), jnp.tile(jnp.sin(ang) * sign, (1, reps))


def kernel(x, c, ctx, c_ctx, ada_w, ada_b, norm1_g, norm2_g, final_g, w_in0, w_out0, lam_q1, lam_k1, lam_q2, lam_k2, subln_g, hy_short_w, hy_short_b, hy_w1, hy_b1, hy_fr1, hy_w2, hy_b2, hy_fr2, hy_w3, hy_b3, hy_bias, cv_w1, cv_b1, cv_dw_w, cv_dw_b, cv_ln_g, cv_ln_b, cv_w2, cv_b2, moe_wg, moe_bg, moe_we, moe_be, moe_w_gate, moe_w_up, moe_w_down):
    bsz, seq, d = x.shape
    lctx = ctx.shape[1]
    depth = ada_w.shape[0]
    t = bsz * seq
    hyw = d - DA_WIDTH
    x2 = x.reshape(t, d)

    rows = ((bsz + 1 + SUBLANES - 1) // SUBLANES) * SUBLANES
    cs = jnp.zeros((rows, d), F32).at[:bsz].set(c).at[bsz].set(c_ctx)
    mods = _ada(cs, ada_w, ada_b)

    def tok_mod(i, k):
        return mods[i, :bsz, k * d:(k + 1) * d].reshape(bsz, 1, d)

    def ctx_mod(i, k):
        return mods[i, bsz:bsz + 1, k * d:(k + 1) * d].reshape(1, 1, d)

    for i in range(depth):
        j = i // 2
        g_n1 = norm1_g[i].reshape(1, d)
        if i % 2 == 0:
            assert not any(m % 2 == 0 for m in range(i + 1, depth)), "context-stream update is not implemented"
            lam_init = 0.8 - 0.6 * math.exp(-0.3 * i)
            w_in = w_in0[j]
            cw = 2 * LANES
            cos, sin = _rope_tables(seq, cw)
            q, k, v, u = _inproj(x2, g_n1, tok_mod(i, 0), tok_mod(i, 1), cos, sin, w_in.astype(BF16), seq)
            kc, vc = _ctxproj(ctx.reshape(bsz * lctx, d), g_n1, ctx_mod(i, 0), ctx_mod(i, 1),
                              w_in[:, DA_WIDTH:3 * DA_WIDTH].astype(BF16), lctx)
            lam_p = jnp.stack([lam_q1[j], lam_k1[j], lam_q2[j], lam_k2[j]])
            k_all = jnp.concatenate([kc.reshape(bsz, lctx, DA_WIDTH), k.reshape(bsz, seq, DA_WIDTH)], axis=1)
            v_all = jnp.concatenate([vc.reshape(bsz, lctx, DA_WIDTH), v.reshape(bsz, seq, DA_WIDTH)], axis=1)
            o_a = _attention(q, jnp.swapaxes(k_all, 1, 2), v_all, lam_p, subln_g[j].reshape(1, -1), seq, lam_init)

            nb = seq // LC_P
            groups = hyw // SUBLANES
            ucm = _dwconv(u.reshape(bsz, seq, 3 * hyw), hy_short_w[j], hy_short_b[j], channel_major=True)
            kc_ext = _hyena_filters(seq, hy_w1[j], hy_b1[j], hy_fr1[j], hy_w2[j], hy_b2[j], hy_fr2[j],
                                    hy_w3[j], hy_b3[j])
            bias_cm = jnp.broadcast_to(hy_bias[j].reshape(HY_ORDER * hyw, 1, 1), (HY_ORDER * hyw, 1, LC_P))
            z1 = _longconv(ucm, 0, ucm, groups, kc_ext, bias_cm, 0, groups, nb, bsz)
            z2 = _longconv(z1, 0, ucm, 2 * groups, kc_ext, bias_cm, 1, groups, nb, bsz)
            x2 = _outproj(x2, tok_mod(i, 2), o_a, z2, w_out0[j].astype(BF16), seq)
        else:
            a = _confin(x2, g_n1, tok_mod(i, 0), tok_mod(i, 1), cv_w1[j].astype(BF16), cv_b1[j], seq)
            a = _dwconv(a.reshape(bsz, seq, -1), cv_dw_w[j], cv_dw_b[j]).reshape(t, -1)
            x2 = _confout(x2, tok_mod(i, 2), a, cv_ln_g[j], cv_ln_b[j], cv_w2[j].astype(BF16), cv_b2[j], seq)
        x2 = _moe(x2, norm2_g[i].reshape(1, d), tok_mod(i, 3), tok_mod(i, 4), tok_mod(i, 5),
                  moe_wg[i], moe_bg[i], moe_we[i], moe_be[i],
                  moe_w_gate.reshape((-1,) + moe_w_gate.shape[2:]), moe_w_up.reshape((-1,) + moe_w_up.shape[2:]),
                  moe_w_down.reshape((-1,) + moe_w_down.shape[2:]), i,
                  final_g.reshape(1, d), seq, final=(i == depth - 1))
    return x2.reshape(bsz, seq, d)
```

```python
import functools
import math

import jax
import jax.numpy as jnp
from jax import lax
from jax.experimental import pallas as pl
from jax.experimental.pallas import tpu as pltpu

F32 = jnp.float32
BF16 = jnp.bfloat16

GRID_W = 64
DA_HEADS = 4
DA_HEAD_DIM = 64
DA_WIDTH = DA_HEADS * 2 * DA_HEAD_DIM
HY_ORDER = 2
HY_TARGET = 1e-2
HY_MIN_DECAY = math.log(HY_TARGET) / 0.3
HY_MAX_DECAY = math.log(HY_TARGET) / 1.5
N_GROUPS = 4
EXPERTS_PER_GROUP = 8
N_EXPERTS = N_GROUPS * EXPERTS_PER_GROUP
ROPE_THETA = 10000.0
EPS = 1e-6

LANES = 128
SUBLANES = 8
VMEM_LIMIT = 52 * 1024 * 1024
ROUTE_LANE0 = N_GROUPS
MOE_TILE = 256
LC_P = 128
LC_W = 4096


def _params(sem, flags=None):
    return pltpu.CompilerParams(dimension_semantics=sem, vmem_limit_bytes=VMEM_LIMIT, flags=flags)


def _split(a):
    hi = a.astype(BF16)
    lo = (a - hi.astype(F32)).astype(BF16)
    return hi, lo


def _dot(a, b):
    return jnp.dot(a, b, preferred_element_type=F32)


def _dot3(a, b):
    ah, al = _split(a)
    bh, bl = _split(b)
    return _dot(ah, bh) + _dot(al, bh) + _dot(ah, bl)


def _norm_mod(x, g, sh, sc):
    y = x * lax.rsqrt(jnp.mean(x * x, axis=-1, keepdims=True) + EPS)
    return (y * g) * (1.0 + sc) + sh


def _ada_kernel(c_ref, w_ref, b_ref, o_ref):
    c = c_ref[...]
    s = c * jax.nn.sigmoid(c)
    o_ref[0] = _dot3(s, w_ref[0]) + b_ref[0]


def _ada(cs, ada_w, ada_b):
    depth, d, n6 = ada_w.shape
    rows = cs.shape[0]
    tn = 1536
    return pl.pallas_call(
        _ada_kernel,
        out_shape=jax.ShapeDtypeStruct((depth, rows, n6), F32),
        grid=(depth, n6 // tn),
        in_specs=[pl.BlockSpec((rows, d), lambda l, j: (0, 0)),
                  pl.BlockSpec((1, d, tn), lambda l, j: (l, 0, j)),
                  pl.BlockSpec((1, 1, tn), lambda l, j: (l, 0, j))],
        out_specs=pl.BlockSpec((1, rows, tn), lambda l, j: (l, 0, j)),
        compiler_params=_params(("parallel", "parallel")),
        name="adaln",
    )(cs, ada_w, ada_b.reshape(depth, 1, n6))


def _inproj_kernel(x_ref, g_ref, sh_ref, sc_ref, cos_ref, sin_ref, w_ref,
                   q_ref, k_ref, v_ref, u_ref, *, d_attn, d_hy):
    h = _norm_mod(x_ref[...], g_ref[...], sh_ref[0], sc_ref[0]).astype(BF16)
    cos = cos_ref[...]
    sin = sin_ref[...]
    cw = cos.shape[1]
    quarter = DA_HEAD_DIM // 4
    lane = lax.broadcasted_iota(jnp.int32, cos.shape, 1)
    first = (lane % (2 * quarter)) < quarter

    def mm(c0, c1):
        return _dot(h, w_ref[:, c0:c1])

    def rope(a):
        partner = jnp.where(first, pltpu.roll(a, cw - quarter, 1), pltpu.roll(a, quarter, 1))
        return a * cos + partner * sin

    for j in range(d_attn // cw):
        q_ref[:, j * cw:(j + 1) * cw] = (rope(mm(j * cw, (j + 1) * cw)) * (DA_HEAD_DIM ** -0.5)).astype(BF16)
        k_ref[:, j * cw:(j + 1) * cw] = rope(mm(d_attn + j * cw, d_attn + (j + 1) * cw)).astype(BF16)
    v_ref[...] = mm(2 * d_attn, 3 * d_attn).astype(BF16)
    for j in range(d_hy // 512):
        u_ref[:, j * 512:(j + 1) * 512] = mm(3 * d_attn + j * 512, 3 * d_attn + (j + 1) * 512)


def _inproj(x2, g, sh, sc, cos, sin, w_ext, seq, tm=512):
    t, d = x2.shape
    tpb = seq // tm
    d_attn = DA_WIDTH
    d_hy = w_ext.shape[1] - 3 * d_attn
    cw = cos.shape[1]
    kern = functools.partial(_inproj_kernel, d_attn=d_attn, d_hy=d_hy)
    return pl.pallas_call(
        kern,
        out_shape=(jax.ShapeDtypeStruct((t, d_attn), BF16), jax.ShapeDtypeStruct((t, d_attn), BF16),
                   jax.ShapeDtypeStruct((t, d_attn), BF16), jax.ShapeDtypeStruct((t, d_hy), F32)),
        grid=(t // tm,),
        in_specs=[pl.BlockSpec((tm, d), lambda i: (i, 0)),
                  pl.BlockSpec((1, d), lambda i: (0, 0)),
                  pl.BlockSpec((1, 1, d), lambda i: (i // tpb, 0, 0)),
                  pl.BlockSpec((1, 1, d), lambda i: (i // tpb, 0, 0)),
                  pl.BlockSpec((tm, cw), lambda i: (i % tpb, 0)),
                  pl.BlockSpec((tm, cw), lambda i: (i % tpb, 0)),
                  pl.BlockSpec(w_ext.shape, lambda i: (0, 0))],
        out_specs=(pl.BlockSpec((tm, d_attn), lambda i: (i, 0)), pl.BlockSpec((tm, d_attn), lambda i: (i, 0)),
                   pl.BlockSpec((tm, d_attn), lambda i: (i, 0)), pl.BlockSpec((tm, d_hy), lambda i: (i, 0))),
        compiler_params=_params(("parallel",)),
        name="inproj0",
    )(x2, g, sh, sc, cos, sin, w_ext)


def _ctxproj_kernel(x_ref, g_ref, sh_ref, sc_ref, w_ref, k_ref, v_ref):
    h = _norm_mod(x_ref[...], g_ref[...], sh_ref[0], sc_ref[0]).astype(BF16)
    n = k_ref.shape[1]
    k_ref[...] = _dot(h, w_ref[:, :n]).astype(BF16)
    v_ref[...] = _dot(h, w_ref[:, n:]).astype(BF16)


def _ctxproj(c2, g, sh, sc, w_kv, tm):
    t, d = c2.shape
    n = w_kv.shape[1] // 2
    return pl.pallas_call(
        _ctxproj_kernel,
        out_shape=(jax.ShapeDtypeStruct((t, n), BF16), jax.ShapeDtypeStruct((t, n), BF16)),
        grid=(t // tm,),
        in_specs=[pl.BlockSpec((tm, d), lambda i: (i, 0)),
                  pl.BlockSpec((1, d), lambda i: (0, 0)),
                  pl.BlockSpec((1, 1, d), lambda i: (0, 0, 0)),
                  pl.BlockSpec((1, 1, d), lambda i: (0, 0, 0)),
                  pl.BlockSpec(w_kv.shape, lambda i: (0, 0))],
        out_specs=(pl.BlockSpec((tm, n), lambda i: (i, 0)), pl.BlockSpec((tm, n), lambda i: (i, 0))),
        compiler_params=_params(("parallel",)),
        name="ctxproj",
    )(c2, g, sh, sc, w_kv)


def _attn_kernel(lam_ref, g_ref, q_ref, kt_ref, v_ref, o_ref, *, lam_init):
    lp = lam_ref[...]
    lam = (jnp.exp(jnp.sum(lp[0:1] * lp[1:2], axis=-1, keepdims=True))
           - jnp.exp(jnp.sum(lp[2:3] * lp[3:4], axis=-1, keepdims=True)) + lam_init)
    hp = 2 * DA_HEAD_DIM
    scores = []
    for p in range(q_ref.shape[1] // hp):
        cols = slice(p * hp, (p + 1) * hp)
        q = q_ref[:, cols]
        lane = lax.broadcasted_iota(jnp.int32, q.shape, 1)
        zero = jnp.zeros_like(q)
        kt = kt_ref[0, cols, :]
        scores.append((_dot(jnp.where(lane < DA_HEAD_DIM, q, zero), kt),
                       _dot(jnp.where(lane >= DA_HEAD_DIM, q, zero), kt)))
    for p, (s1, s2) in enumerate(scores):
        cols = slice(p * hp, (p + 1) * hp)
        e1 = jnp.exp(s1 - jnp.max(s1, axis=-1, keepdims=True))
        e2 = jnp.exp(s2 - jnp.max(s2, axis=-1, keepdims=True))
        r1 = 1.0 / jnp.sum(e1, axis=-1, keepdims=True)
        r2 = lam / jnp.sum(e2, axis=-1, keepdims=True)
        a = (e1 * r1 - e2 * r2).astype(BF16)
        o = _dot(a, v_ref[0, :, cols])
        y = o * lax.rsqrt(jnp.mean(o * o, axis=-1, keepdims=True) + EPS)
        o_ref[:, cols] = (y * g_ref[...]) * (1.0 - lam_init)


def _attention(q, kt_all, v_all, lam_p, subln_g, seq, lam_init, tq=256, pairs=4):
    t = q.shape[0]
    b, _, lk = kt_all.shape
    hw = 2 * DA_HEAD_DIM * pairs
    nq = seq // tq
    kern = functools.partial(_attn_kernel, lam_init=lam_init)
    return pl.pallas_call(
        kern,
        out_shape=jax.ShapeDtypeStruct((t, DA_WIDTH), F32),
        grid=(b, DA_HEADS // pairs, nq),
        in_specs=[pl.BlockSpec(lam_p.shape, lambda bi, h, i: (0, 0)),
                  pl.BlockSpec((1, 2 * DA_HEAD_DIM), lambda bi, h, i: (0, 0)),
                  pl.BlockSpec((tq, hw), lambda bi, h, i: (bi * nq + i, h)),
                  pl.BlockSpec((1, hw, lk), lambda bi, h, i: (bi, h, 0)),
                  pl.BlockSpec((1, lk, hw), lambda bi, h, i: (bi, 0, h))],
        out_specs=pl.BlockSpec((tq, hw), lambda bi, h, i: (bi * nq + i, h)),
        compiler_params=_params(("parallel", "parallel", "parallel")),
        name="diff_attn",
    )(lam_p, subln_g, q, kt_all, v_all)


def _dwconv_kernel(x_ref, w_ref, b_ref, o_ref, pad_ref, *, kw, halo, channel_major):
    s = x_ref.shape[1]
    pl_ = (kw - 1) // 2
    zeros = jnp.zeros((halo, x_ref.shape[2]), F32)
    pad_ref[0:halo, :] = zeros
    pad_ref[halo + s:halo + s + halo, :] = zeros
    pad_ref[halo:halo + s, :] = x_ref[0]
    w = w_ref[...]
    acc = jnp.zeros((s, x_ref.shape[2]), F32) + b_ref[...]
    for j in range(kw):
        off = halo - pl_ + j
        acc = acc + pad_ref[off:off + s, :] * w[j:j + 1, :]
    if channel_major:
        acc_t = acc.T
        for g in range(LANES // SUBLANES):
            for blk in range(s // LANES):
                o_ref[g, blk] = acc_t[g * SUBLANES:(g + 1) * SUBLANES, blk * LANES:(blk + 1) * LANES]
    else:
        o_ref[0] = acc


def _dwconv(x3, w, bias, channel_major=False):
    b, s, c = x3.shape
    kw = w.shape[0]
    halo = 2 * SUBLANES
    kern = functools.partial(_dwconv_kernel, kw=kw, halo=halo, channel_major=channel_major)
    if channel_major:
        gpb = LANES // SUBLANES
        out_shape = jax.ShapeDtypeStruct((c // SUBLANES, s // LANES, b * SUBLANES, LANES), F32)
        out_spec = pl.BlockSpec((gpb, s // LANES, SUBLANES, LANES), lambda bi, ci: (ci, 0, bi, 0))
    else:
        out_shape = jax.ShapeDtypeStruct((b, s, c), F32)
        out_spec = pl.BlockSpec((1, s, LANES), lambda bi, ci: (bi, 0, ci))
    return pl.pallas_call(
        kern,
        out_shape=out_shape,
        grid=(b, c // LANES),
        in_specs=[pl.BlockSpec((1, s, LANES), lambda bi, ci: (bi, 0, ci)),
                  pl.BlockSpec((kw, LANES), lambda bi, ci: (0, ci)),
                  pl.BlockSpec((1, LANES), lambda bi, ci: (0, ci))],
        out_specs=out_spec,
        scratch_shapes=[pltpu.VMEM((s + 2 * halo, LANES), F32)],
        compiler_params=_params(("parallel", "parallel")),
        name="dwconv",
    )(x3, w, bias.reshape(1, c))


def _filter_kernel(zt_ref, t_ref, w1_ref, b1_ref, f1_ref, w2_ref, b2_ref, f2_ref, w3_ref, b3_ref, dl_ref,
                   o_ref, h_ref, *, n):
    @pl.when(pl.program_id(0) == 0)
    def _():
        h1 = jnp.sin(f1_ref[...] * (_dot3(w1_ref[...], zt_ref[...]) + b1_ref[...]))
        h_ref[...] = jnp.sin(f2_ref[...] * (_dot3(w2_ref[...], h1) + b2_ref[...]))

    h2 = h_ref[...]
    rows = o_ref.shape[0]
    fwd = _dot3(w3_ref[0], h2[:, :n]) + b3_ref[0]
    bwd = _dot3(w3_ref[1], h2[:, n:]) + b3_ref[1]
    decay = jnp.exp(-t_ref[...] * dl_ref[...])
    lane = lax.broadcasted_iota(jnp.int32, (rows, n), 1)
    kf = fwd * decay[:, :n]
    kb = jnp.where(lane == 0, 0.0, bwd * decay[:, n:])
    inv = 1.0 / (jnp.sum(jnp.abs(kf), axis=-1, keepdims=True) + jnp.sum(jnp.abs(kb), axis=-1, keepdims=True))
    kf = kf * inv
    kb = kb * inv
    o_ref[:, 0:LANES] = kb[:, n - LANES:]
    o_ref[:, LANES:LANES + n] = kf
    o_ref[:, LANES + n:] = kb


def _hyena_filters(n, w1, b1, fr1, w2, b2, fr2, w3, b3):
    emb, ffn = w1.shape
    c = w3.shape[1] // (2 * HY_ORDER)
    bands = (emb - 1) // 2
    t = jnp.linspace(0.0, 1.0, n, dtype=F32)[:, None]
    ang = (2.0 * math.pi / n) * jnp.arange(n, dtype=F32)[:, None] * jnp.linspace(1e-4, bands - 1, bands, dtype=F32)[None, :]
    z = jnp.concatenate([t, jnp.cos(ang), -jnp.sin(ang)], axis=-1)
    rev = (n - jnp.arange(n)) % n
    z2 = jnp.concatenate([z, z[rev]], axis=0)
    t2 = jnp.concatenate([t, t[rev]], axis=0).reshape(1, 2 * n)
    emb_p = ((emb + SUBLANES - 1) // SUBLANES) * SUBLANES
    zt = jnp.zeros((emb_p, 2 * n), F32).at[:emb].set(z2.T)
    w1t = jnp.zeros((ffn, emb_p), F32).at[:, :emb].set(w1.T)
    deltas = jnp.abs(jnp.linspace(HY_MIN_DECAY, HY_MAX_DECAY, c, dtype=F32))
    w3t = w3.T.reshape(HY_ORDER, 2, c, ffn).transpose(1, 0, 2, 3).reshape(2, HY_ORDER * c, ffn)
    b3t = b3.reshape(HY_ORDER, 2, c).transpose(1, 0, 2).reshape(2, HY_ORDER * c, 1)
    dl = jnp.tile(deltas, HY_ORDER).reshape(HY_ORDER * c, 1)
    rows = LANES
    kern = functools.partial(_filter_kernel, n=n)
    col = lambda v: v.reshape(ffn, 1)
    return pl.pallas_call(
        kern,
        out_shape=jax.ShapeDtypeStruct((HY_ORDER * c, LANES + 2 * n), F32),
        grid=(HY_ORDER * c // rows,),
        in_specs=[pl.BlockSpec(zt.shape, lambda i: (0, 0)),
                  pl.BlockSpec(t2.shape, lambda i: (0, 0)),
                  pl.BlockSpec(w1t.shape, lambda i: (0, 0)),
                  pl.BlockSpec((ffn, 1), lambda i: (0, 0)),
                  pl.BlockSpec((ffn, 1), lambda i: (0, 0)),
                  pl.BlockSpec((ffn, ffn), lambda i: (0, 0)),
                  pl.BlockSpec((ffn, 1), lambda i: (0, 0)),
                  pl.BlockSpec((ffn, 1), lambda i: (0, 0)),
                  pl.BlockSpec((2, rows, ffn), lambda i: (0, i, 0)),
                  pl.BlockSpec((2, rows, 1), lambda i: (0, i, 0)),
                  pl.BlockSpec((rows, 1), lambda i: (i, 0))],
        out_specs=pl.BlockSpec((rows, LANES + 2 * n), lambda i: (i, 0)),
        scratch_shapes=[pltpu.VMEM((ffn, 2 * n), F32)],
        compiler_params=_params(("arbitrary",)),
        name="hyena_filters",
    )(zt, t2, w1t, col(b1), col(fr1), w2.T, col(b2), col(fr2), w3t, b3t, dl)


def _longconv_kernel(zero_ref, z_ref, gate_ref, kc_ref, kcn_ref, bias_ref, o_ref, r_a, r_b, zs_ref,
                     *, nb, bsz, cb):
    p = LC_P
    n2 = 2 * nb * p
    w = min(LC_W, n2)
    nchunk = n2 // w
    dots_per_chunk = nb // nchunk

    def build_chunk(src, r_dst, ci):
        k_ref, c = src
        win = k_ref[pl.ds(c, 1), ci * w:ci * w + w + LANES]
        rolled = pltpu.roll(jnp.broadcast_to(win, (p, w + LANES)), 0, 1, stride=1, stride_axis=0)
        r_dst[:, ci * w:(ci + 1) * w] = rolled[:, LANES:].astype(BF16)
        bits = pltpu.bitcast(rolled[0:SUBLANES, LANES:2 * LANES], jnp.int32) & zero_ref[...]
        return pltpu.bitcast(bits, F32)[0:1, :].astype(BF16)

    def conv(c, r_src, c_next, r_next):
        for s1 in range(nb):
            zs_ref[s1 * bsz:(s1 + 1) * bsz, :] = z_ref[0, s1, pl.ds(c, bsz, stride=cb), :]
        acc = [None] * nb
        held = None
        for pi in range(nb):
            if pi % dots_per_chunk == 0:
                held = build_chunk(c_next, r_next, pi // dots_per_chunk)
            d = -nb + 2 * pi
            off = (d % (2 * nb)) * p
            lo = max(0, -d - 1)
            hi = min(nb, nb - d)
            lhs = zs_ref[lo * bsz:hi * bsz, :].astype(BF16) + held
            out = _dot(lhs, r_src[:, off:off + 2 * p])
            for k in range(2):
                dk = d + k
                for s1 in range(max(0, -dk), min(nb, nb - dk)):
                    blk = out[(s1 - lo) * bsz:(s1 - lo + 1) * bsz, k * p:(k + 1) * p]
                    acc[s1 + dk] = blk if acc[s1 + dk] is None else acc[s1 + dk] + blk
        bias = bias_ref[c]
        for s1 in range(nb):
            rows = slice(s1 * bsz, (s1 + 1) * bsz)
            where = pl.ds(c, bsz, stride=cb)
            o_ref[0, s1, where, :] = gate_ref[0, s1, where, :] * (acc[s1] + zs_ref[rows, :] * bias)

    @pl.when(pl.program_id(0) == 0)
    def _():
        for ci in range(nchunk):
            build_chunk((kc_ref, 0), r_a, ci)

    def pair(k, carry):
        c = 2 * k
        conv(c, r_a, (kc_ref, c + 1), r_b)
        conv(c + 1, r_b, (kc_ref, c + 2), r_a)
        return carry

    lax.fori_loop(0, cb // 2 - 1, pair, 0)
    conv(cb - 2, r_a, (kc_ref, cb - 1), r_b)
    conv(cb - 1, r_b, (kcn_ref, 0), r_a)


def _longconv(z_cm, z_g0, gate_cm, gate_g0, kc_ext, bias_cm, order, groups, nb, bsz):
    cb = SUBLANES
    rows, p = z_cm.shape[2:]
    blk = (1, nb, rows, p)
    kern = functools.partial(_longconv_kernel, nb=nb, bsz=bsz, cb=cb)
    return pl.pallas_call(
        kern,
        out_shape=jax.ShapeDtypeStruct((groups, nb, rows, p), F32),
        grid=(groups,),
        in_specs=[pl.BlockSpec((1, LANES), lambda i: (0, 0)),
                  pl.BlockSpec(blk, lambda i: (z_g0 + i, 0, 0, 0)),
                  pl.BlockSpec(blk, lambda i: (gate_g0 + i, 0, 0, 0)),
                  pl.BlockSpec((cb, kc_ext.shape[1]), lambda i: (order * groups + i, 0)),
                  pl.BlockSpec((cb, kc_ext.shape[1]), lambda i: (order * groups + jnp.minimum(i + 1, groups - 1), 0)),
                  pl.BlockSpec((cb, 1, p), lambda i: (order * groups + i, 0, 0))],
        out_specs=pl.BlockSpec(blk, lambda i: (i, 0, 0, 0)),
        scratch_shapes=[pltpu.VMEM((p, 2 * nb * p), BF16), pltpu.VMEM((p, 2 * nb * p), BF16),
                        pltpu.VMEM((nb * bsz, p), F32)],
        compiler_params=_params(("arbitrary",)),
        name="hyena_longconv",
    )(jnp.zeros((1, LANES), jnp.int32), z_cm, gate_cm, kc_ext, kc_ext, bias_cm)


def _outproj_kernel(x_ref, g1_ref, a_ref, b_ref, w_ref, o_ref):
    da = a_ref.shape[1]
    ya = _dot(a_ref[...].astype(BF16), w_ref[:da, :])
    groups, nblk, cg, p = b_ref.shape
    parts = []
    for blk in range(nblk):
        cm = b_ref[:, blk, :, :].reshape(groups * cg, p)
        parts.append(_dot(cm.T.astype(BF16), w_ref[da:, :]))
    y = ya + jnp.concatenate(parts, axis=0)
    o_ref[...] = x_ref[...] + g1_ref[0] * y


def _outproj(x2, g1, oa, ob, w, seq, tm=512):
    t, d = x2.shape
    tpb = seq // tm
    return pl.pallas_call(
        _outproj_kernel,
        out_shape=jax.ShapeDtypeStruct((t, d), F32),
        grid=(t // tm,),
        in_specs=[pl.BlockSpec((tm, d), lambda i: (i, 0)),
                  pl.BlockSpec((1, 1, d), lambda i: (i // tpb, 0, 0)),
                  pl.BlockSpec((tm, oa.shape[1]), lambda i: (i, 0)),
                  pl.BlockSpec((ob.shape[0], tm // LANES, SUBLANES, LANES), lambda i: (0, i % tpb, i // tpb, 0)),
                  pl.BlockSpec(w.shape, lambda i: (0, 0))],
        out_specs=pl.BlockSpec((tm, d), lambda i: (i, 0)),
        compiler_params=_params(("parallel",)),
        name="outproj0",
    )(x2, g1, oa, ob, w)


def _confin_kernel(x_ref, g_ref, sh_ref, sc_ref, w_ref, b_ref, o_ref):
    h = _norm_mod(x_ref[...], g_ref[...], sh_ref[0], sc_ref[0]).astype(BF16)
    n = o_ref.shape[1]
    a = _dot(h, w_ref[:, :n]) + b_ref[:, :n]
    gt = _dot(h, w_ref[:, n:]) + b_ref[:, n:]
    o_ref[...] = a * jax.nn.sigmoid(gt)


def _confin(x2, g, sh, sc, w1, b1, seq, tm=512):
    t, d = x2.shape
    n = w1.shape[1] // 2
    tpb = seq // tm
    return pl.pallas_call(
        _confin_kernel,
        out_shape=jax.ShapeDtypeStruct((t, n), F32),
        grid=(t // tm,),
        in_specs=[pl.BlockSpec((tm, d), lambda i: (i, 0)),
                  pl.BlockSpec((1, d), lambda i: (0, 0)),
                  pl.BlockSpec((1, 1, d), lambda i: (i // tpb, 0, 0)),
                  pl.BlockSpec((1, 1, d), lambda i: (i // tpb, 0, 0)),
                  pl.BlockSpec(w1.shape, lambda i: (0, 0)),
                  pl.BlockSpec((1, 2 * n), lambda i: (0, 0))],
        out_specs=pl.BlockSpec((tm, n), lambda i: (i, 0)),
        compiler_params=_params(("parallel",)),
        name="conformer_in",
    )(x2, g, sh, sc, w1, b1.reshape(1, 2 * n))


def _confout_kernel(x_ref, g1_ref, a_ref, lg_ref, lb_ref, w_ref, b_ref, o_ref):
    a = a_ref[...]
    mu = jnp.mean(a, axis=-1, keepdims=True)
    ac = a - mu
    var = jnp.mean(ac * ac, axis=-1, keepdims=True)
    y = ac * lax.rsqrt(var + EPS) * lg_ref[...] + lb_ref[...]
    y = y * jax.nn.sigmoid(y)
    o_ref[...] = x_ref[...] + g1_ref[0] * (_dot(y.astype(BF16), w_ref[...]) + b_ref[...])


def _confout(x2, g1, a2, ln_g, ln_b, w2, b2, seq, tm=512):
    t, d = x2.shape
    n = a2.shape[1]
    tpb = seq // tm
    return pl.pallas_call(
        _confout_kernel,
        out_shape=jax.ShapeDtypeStruct((t, d), F32),
        grid=(t // tm,),
        in_specs=[pl.BlockSpec((tm, d), lambda i: (i, 0)),
                  pl.BlockSpec((1, 1, d), lambda i: (i // tpb, 0, 0)),
                  pl.BlockSpec((tm, n), lambda i: (i, 0)),
                  pl.BlockSpec((1, n), lambda i: (0, 0)),
                  pl.BlockSpec((1, n), lambda i: (0, 0)),
                  pl.BlockSpec(w2.shape, lambda i: (0, 0)),
                  pl.BlockSpec((1, d), lambda i: (0, 0))],
        out_specs=pl.BlockSpec((tm, d), lambda i: (i, 0)),
        compiler_params=_params(("parallel",)),
        name="conformer_out",
    )(x2, g1, a2, ln_g.reshape(1, n), ln_b.reshape(1, n), w2, b2.reshape(1, d))


def _router_kernel(x_ref, g_ref, sh_ref, sc_ref, wr_ref, br_ref, tri_ref, h_ref, route_ref, cnt_ref, run_ref):
    i = pl.program_id(0)

    @pl.when(i == 0)
    def _():
        run_ref[...] = jnp.zeros_like(run_ref)

    h = _norm_mod(x_ref[...], g_ref[...], sh_ref[0], sc_ref[0])
    hh, hl = _split(h)
    h_ref[...] = hh
    logits = _dot(hh, wr_ref[0]) + _dot(hl, wr_ref[0]) + _dot(hh, wr_ref[1]) + br_ref[...]
    tm = logits.shape[0]
    lane = lax.broadcasted_iota(jnp.int32, (tm, LANES), 1)
    ninf = jnp.float32(-jnp.inf)

    def first_argmax(v, m):
        return jnp.min(jnp.where(v == m, lane, LANES), axis=-1, keepdims=True)

    gl = jnp.where(lane < N_GROUPS, logits, ninf)
    gmax = jnp.max(gl, axis=-1, keepdims=True)
    g_w = 1.0 / jnp.sum(jnp.exp(gl - gmax), axis=-1, keepdims=True)
    g_idx = first_argmax(gl, gmax)
    e_lo = ROUTE_LANE0 + EXPERTS_PER_GROUP * g_idx
    el = jnp.where((lane >= e_lo) & (lane < e_lo + EXPERTS_PER_GROUP), logits, ninf)
    m1 = jnp.max(el, axis=-1, keepdims=True)
    esum = jnp.sum(jnp.exp(el - m1), axis=-1, keepdims=True)
    i1 = first_argmax(el, m1)
    el2 = jnp.where(lane == i1, ninf, el)
    m2 = jnp.max(el2, axis=-1, keepdims=True)
    i2 = first_argmax(el2, m2)
    p1 = 1.0 / esum
    p2 = jnp.exp(m2 - m1) / esum
    w1 = g_w * (p1 / (p1 + p2))
    w2 = g_w * (p2 / (p1 + p2))

    oh = jnp.where((lane == i1) | (lane == i2), 1.0, 0.0)
    before = _dot(tri_ref[...], oh.astype(BF16)) + run_ref[...]
    rank1 = jnp.sum(jnp.where(lane == i1, before, 0.0), axis=-1, keepdims=True)
    rank2 = jnp.sum(jnp.where(lane == i2, before, 0.0), axis=-1, keepdims=True)
    run_ref[...] = run_ref[...] + jnp.sum(oh, axis=0, keepdims=True)
    cnt_ref[...] = run_ref[...]

    e1 = (i1 - ROUTE_LANE0).astype(F32)
    e2 = (i2 - ROUTE_LANE0).astype(F32)
    vals = (e1, e2, rank1, rank2, w1, w2)
    out = jnp.zeros((tm, LANES), F32)
    for k, v in enumerate(vals):
        out = jnp.where(lane == k, v, out)
    route_ref[...] = out


def _router(x2, g, sh, sc, wr, br, seq, tok0, ntok, tm=512):
    d = x2.shape[1]
    t = ntok
    tpb = seq // tm
    i0 = tok0 // tm
    return pl.pallas_call(
        _router_kernel,
        out_shape=(jax.ShapeDtypeStruct((t, d), BF16), jax.ShapeDtypeStruct((t, LANES), F32),
                   jax.ShapeDtypeStruct((1, LANES), F32)),
        grid=(t // tm,),
        in_specs=[pl.BlockSpec((tm, d), lambda i: (i0 + i, 0)),
                  pl.BlockSpec((1, d), lambda i: (0, 0)),
                  pl.BlockSpec((1, 1, d), lambda i: ((i0 + i) // tpb, 0, 0)),
                  pl.BlockSpec((1, 1, d), lambda i: ((i0 + i) // tpb, 0, 0)),
                  pl.BlockSpec(wr.shape, lambda i: (0, 0, 0)),
                  pl.BlockSpec((1, LANES), lambda i: (0, 0)),
                  pl.BlockSpec((tm, tm), lambda i: (0, 0))],
        out_specs=(pl.BlockSpec((tm, d), lambda i: (i, 0)), pl.BlockSpec((tm, LANES), lambda i: (i, 0)),
                   pl.BlockSpec((1, LANES), lambda i: (0, 0))),
        scratch_shapes=[pltpu.VMEM((1, LANES), F32)],
        compiler_params=_params(("arbitrary",)),
        name="moe_router",
    )(x2, g, sh, sc, wr, br, jnp.tril(jnp.ones((tm, tm), BF16), -1))


def _expert_kernel(te_ref, nv_ref, x_ref, wg_ref, wu_ref, wd_ref, o_ref, wg_s, wu_s, wd_s):
    i = pl.program_id(0)

    @pl.when((i == 0) | (te_ref[i] != te_ref[jnp.maximum(i - 1, 0)]))
    def _():
        wg_s[...] = wg_ref[0].astype(BF16)
        wu_s[...] = wu_ref[0].astype(BF16)
        wd_s[...] = wd_ref[0].astype(BF16)

    @pl.when(i < nv_ref[0])
    def _():
        x = x_ref[...]
        a = _dot(x, wg_s[...])
        u = _dot(x, wu_s[...])
        he = (a * jax.nn.sigmoid(a)) * u
        o_ref[...] = _dot(he.astype(BF16), wd_s[...]).astype(o_ref.dtype)


def _experts(xs, tile_expert, n_valid, wg, wu, wd, layer):
    r, d = xs.shape
    de = wg.shape[2]
    nt = r // MOE_TILE
    row = lambda i, te, nv: (jnp.minimum(i, nv[0] - 1), 0)
    wsel = lambda i, te, nv: (layer * N_EXPERTS + te[i], 0, 0)
    return pl.pallas_call(
        _expert_kernel,
        out_shape=jax.ShapeDtypeStruct((r, d), BF16),
        grid_spec=pltpu.PrefetchScalarGridSpec(
            num_scalar_prefetch=2,
            grid=(nt,),
            in_specs=[pl.BlockSpec((MOE_TILE, d), row),
                      pl.BlockSpec((1, d, de), wsel),
                      pl.BlockSpec((1, d, de), wsel),
                      pl.BlockSpec((1, de, d), wsel)],
            out_specs=pl.BlockSpec((MOE_TILE, d), row),
            scratch_shapes=[pltpu.VMEM((d, de), BF16), pltpu.VMEM((d, de), BF16), pltpu.VMEM((de, d), BF16)]),
        compiler_params=_params(("arbitrary",)),
        name="moe_experts",
    )(tile_expert, n_valid, xs, wg, wu, wd)


def _combine_kernel(x_ref, g2_ref, ra_ref, rb_ref, ya1_ref, ya2_ref, yb1_ref, yb2_ref, fg_ref, o_ref, *, final, nta):
    def finish(route_ref, y1_ref, y2_ref):
        r = route_ref[...]
        w1 = r[:, 4:5]
        w2 = r[:, 5:6]
        x = x_ref[...] + g2_ref[0] * (w1 * y1_ref[...].astype(F32) + w2 * y2_ref[...].astype(F32))
        if final:
            x = (x * lax.rsqrt(jnp.mean(x * x, axis=-1, keepdims=True) + EPS)) * fg_ref[...]
        o_ref[...] = x

    @pl.when(pl.program_id(0) < nta)
    def _():
        finish(ra_ref, ya1_ref, ya2_ref)

    @pl.when(pl.program_id(0) >= nta)
    def _():
        finish(rb_ref, yb1_ref, yb2_ref)


def _combine(x2, g2, parts, final_g, seq, final, tm=512):
    t, d = x2.shape
    tpb = seq // tm
    (ra, ya), (rb, yb) = parts
    nta = ra.shape[0] // tm
    ntb = rb.shape[0] // tm
    in_a = lambda i: jnp.minimum(i, nta - 1)
    in_b = lambda i: jnp.clip(i - nta, 0, ntb - 1)
    kern = functools.partial(_combine_kernel, final=final, nta=nta)
    return pl.pallas_call(
        kern,
        out_shape=jax.ShapeDtypeStruct((t, d), F32),
        grid=(t // tm,),
        in_specs=[pl.BlockSpec((tm, d), lambda i: (i, 0)),
                  pl.BlockSpec((1, 1, d), lambda i: (i // tpb, 0, 0)),
                  pl.BlockSpec((tm, LANES), lambda i: (in_a(i), 0)),
                  pl.BlockSpec((tm, LANES), lambda i: (in_b(i), 0)),
                  pl.BlockSpec((tm, d), lambda i: (in_a(i), 0)),
                  pl.BlockSpec((tm, d), lambda i: (nta + in_a(i), 0)),
                  pl.BlockSpec((tm, d), lambda i: (in_b(i), 0)),
                  pl.BlockSpec((tm, d), lambda i: (ntb + in_b(i), 0)),
                  pl.BlockSpec((1, d), lambda i: (0, 0))],
        out_specs=pl.BlockSpec((tm, d), lambda i: (i, 0)),
        compiler_params=_params(("parallel",)),
        name="moe_combine",
    )(x2, g2, ra, rb, ya, ya, yb, yb, final_g)


def _moe(x2, g, sh, sc, g2, wg_r, bg_r, we_r, be_r, w_gate, w_up, w_down, layer, final_g, seq, final):
    t, d = x2.shape
    wr = jnp.zeros((d, LANES), F32).at[:, :N_GROUPS].set(wg_r).at[:, ROUTE_LANE0:ROUTE_LANE0 + N_EXPERTS].set(we_r)
    wr_hi = wr.astype(BF16)
    wr_lo = (wr - wr_hi.astype(F32)).astype(BF16)
    br = jnp.zeros((1, LANES), F32).at[0, :N_GROUPS].set(bg_r).at[0, ROUTE_LANE0:ROUTE_LANE0 + N_EXPERTS].set(be_r)
    wr2 = jnp.stack([wr_hi, wr_lo])
    half = t // 2
    parts = [_moe_part(x2, g, sh, sc, wr2, br, w_gate, w_up, w_down, layer, seq, tok0, half) for tok0 in (0, half)]
    return _combine(x2, g2, parts, final_g, seq, final)


def _moe_part(x2, g, sh, sc, wr2, br, w_gate, w_up, w_down, layer, seq, tok0, t):
    h, route, cnt = _router(x2, g, sh, sc, wr2, br, seq, tok0, t)

    counts = cnt[0, ROUTE_LANE0:ROUTE_LANE0 + N_EXPERTS].astype(jnp.int32)
    tiles = (counts + MOE_TILE - 1) // MOE_TILE
    tile_end = jnp.cumsum(tiles)
    offs = (tile_end - tiles) * MOE_TILE
    nt = (2 * t) // MOE_TILE + N_EXPERTS
    tile_id = jnp.minimum(jnp.arange(nt, dtype=jnp.int32), tile_end[-1] - 1)
    tile_expert = jnp.sum((tile_id[:, None] >= tile_end[None, :]).astype(jnp.int32), axis=1)
    n_valid = tile_end[-1:].astype(jnp.int32)
    e12 = route[:, 0:2].astype(jnp.int32)
    offs12 = jnp.sum(jnp.where(e12[:, :, None] == jnp.arange(N_EXPERTS)[None, None, :], offs[None, None, :], 0), axis=-1)
    pos = (offs12 + route[:, 2:4].astype(jnp.int32)).T.reshape(-1)
    tok = jnp.tile(jnp.arange(t, dtype=jnp.int32), 2)
    _, tok_by_row = lax.sort_key_val(pos, tok)
    row = jnp.arange(nt * MOE_TILE, dtype=jnp.int32)
    per_row = lambda v: jnp.repeat(v[tile_expert], MOE_TILE)
    in_e = row - per_row(offs)
    compact = per_row(jnp.cumsum(counts) - counts) + in_e
    sorted_tok = jnp.where(in_e < per_row(counts),
                           tok_by_row.at[jnp.minimum(compact, 2 * t - 1)].get(mode="promise_in_bounds"), row % t)
    xs = h.at[sorted_tok].get(mode="promise_in_bounds")
    ys = _experts(xs, tile_expert, n_valid, w_gate, w_up, w_down, layer)
    y12 = ys.at[pos].get(mode="promise_in_bounds")
    return route, y12


def _rope_tables(seq, width):
    hd = DA_HEAD_DIM
    half = hd // 2
    quarter = half // 2
    pos = jnp.arange(seq)
    row = (pos // GRID_W).astype(F32)
    col = (pos % GRID_W).astype(F32)
    inv = ROPE_THETA ** (-jnp.arange(0, half, 2, dtype=F32) / half)
    i = jnp.arange(hd)
    p = jnp.where((i < half)[None, :], row[:, None], col[:, None])
    ang = p * inv[i % quarter][None, :]
    sign = jnp.where((i % half) < quarter, -1.0, 1.0)[None, :]
    reps = width // hd
    return jnp.tile(jnp.cos(ang), (1, reps)), jnp.tile(jnp.sin(ang) * sign, (1, reps))


def kernel(x, c, ctx, c_ctx, ada_w, ada_b, norm1_g, norm2_g, final_g, w_in0, w_out0, lam_q1, lam_k1, lam_q2, lam_k2, subln_g, hy_short_w, hy_short_b, hy_w1, hy_b1, hy_fr1, hy_w2, hy_b2, hy_fr2, hy_w3, hy_b3, hy_bias, cv_w1, cv_b1, cv_dw_w, cv_dw_b, cv_ln_g, cv_ln_b, cv_w2, cv_b2, moe_wg, moe_bg, moe_we, moe_be, moe_w_gate, moe_w_up, moe_w_down):
    bsz, seq, d = x.shape
    lctx = ctx.shape[1]
    depth = ada_w.shape[0]
    t = bsz * seq
    hyw = d - DA_WIDTH
    x2 = x.reshape(t, d)

    rows = ((bsz + 1 + SUBLANES - 1) // SUBLANES) * SUBLANES
    cs = jnp.zeros((rows, d), F32).at[:bsz].set(c).at[bsz].set(c_ctx)
    mods = _ada(cs, ada_w, ada_b)

    def tok_mod(i, k):
        return mods[i, :bsz, k * d:(k + 1) * d].reshape(bsz, 1, d)

    def ctx_mod(i, k):
        return mods[i, bsz:bsz + 1, k * d:(k + 1) * d].reshape(1, 1, d)

    for i in range(depth):
        j = i // 2
        g_n1 = norm1_g[i].reshape(1, d)
        if i % 2 == 0:
            assert not any(m % 2 == 0 for m in range(i + 1, depth)), "context-stream update is not implemented"
            lam_init = 0.8 - 0.6 * math.exp(-0.3 * i)
            w_in = w_in0[j]
            cw = 2 * LANES
            cos, sin = _rope_tables(seq, cw)
            q, k, v, u = _inproj(x2, g_n1, tok_mod(i, 0), tok_mod(i, 1), cos, sin, w_in.astype(BF16), seq)
            kc, vc = _ctxproj(ctx.reshape(bsz * lctx, d), g_n1, ctx_mod(i, 0), ctx_mod(i, 1),
                              w_in[:, DA_WIDTH:3 * DA_WIDTH].astype(BF16), lctx)
            lam_p = jnp.stack([lam_q1[j], lam_k1[j], lam_q2[j], lam_k2[j]])
            k_all = jnp.concatenate([kc.reshape(bsz, lctx, DA_WIDTH), k.reshape(bsz, seq, DA_WIDTH)], axis=1)
            v_all = jnp.concatenate([vc.reshape(bsz, lctx, DA_WIDTH), v.reshape(bsz, seq, DA_WIDTH)], axis=1)
            o_a = _attention(q, jnp.swapaxes(k_all, 1, 2), v_all, lam_p, subln_g[j].reshape(1, -1), seq, lam_init)

            nb = seq // LC_P
            groups = hyw // SUBLANES
            ucm = _dwconv(u.reshape(bsz, seq, 3 * hyw), hy_short_w[j], hy_short_b[j], channel_major=True)
            kc_ext = _hyena_filters(seq, hy_w1[j], hy_b1[j], hy_fr1[j], hy_w2[j], hy_b2[j], hy_fr2[j],
                                    hy_w3[j], hy_b3[j])
            bias_cm = jnp.broadcast_to(hy_bias[j].reshape(HY_ORDER * hyw, 1, 1), (HY_ORDER * hyw, 1, LC_P))
            z1 = _longconv(ucm, 0, ucm, groups, kc_ext, bias_cm, 0, groups, nb, bsz)
            z2 = _longconv(z1, 0, ucm, 2 * groups, kc_ext, bias_cm, 1, groups, nb, bsz)
            x2 = _outproj(x2, tok_mod(i, 2), o_a, z2, w_out0[j].astype(BF16), seq)
        else:
            a = _confin(x2, g_n1, tok_mod(i, 0), tok_mod(i, 1), cv_w1[j].astype(BF16), cv_b1[j], seq)
            a = _dwconv(a.reshape(bsz, seq, -1), cv_dw_w[j], cv_dw_b[j]).reshape(t, -1)
            x2 = _confout(x2, tok_mod(i, 2), a, cv_ln_g[j], cv_ln_b[j], cv_w2[j].astype(BF16), cv_b2[j], seq)
        x2 = _moe(x2, norm2_g[i].reshape(1, d), tok_mod(i, 3), tok_mod(i, 4), tok_mod(i, 5),
                  moe_wg[i], moe_bg[i], moe_we[i], moe_be[i],
                  moe_w_gate.reshape((-1,) + moe_w_gate.shape[2:]), moe_w_up.reshape((-1,) + moe_w_up.shape[2:]),
                  moe_w_down.reshape((-1,) + moe_w_down.shape[2:]), i,
                  final_g.reshape(1, d), seq, final=(i == depth - 1))
    return x2.reshape(bsz, seq, d)
```

```python
import functools
import math

import jax
import jax.numpy as jnp
from jax import lax
from jax.experimental import pallas as pl
from jax.experimental.pallas import tpu as pltpu

F32 = jnp.float32
BF16 = jnp.bfloat16

GRID_W = 64
DA_HEADS = 4
DA_HEAD_DIM = 64
DA_WIDTH = DA_HEADS * 2 * DA_HEAD_DIM
HY_ORDER = 2
HY_TARGET = 1e-2
HY_MIN_DECAY = math.log(HY_TARGET) / 0.3
HY_MAX_DECAY = math.log(HY_TARGET) / 1.5
N_GROUPS = 4
EXPERTS_PER_GROUP = 8
N_EXPERTS = N_GROUPS * EXPERTS_PER_GROUP
ROPE_THETA = 10000.0
EPS = 1e-6

LANES = 128
SUBLANES = 8
VMEM_LIMIT = 52 * 1024 * 1024
ROUTE_LANE0 = N_GROUPS
MOE_TILE = 256
LC_P = 128
LC_W = 4096


def _params(sem, flags=None):
    return pltpu.CompilerParams(dimension_semantics=sem, vmem_limit_bytes=VMEM_LIMIT, flags=flags)


def _split(a):
    hi = a.astype(BF16)
    lo = (a - hi.astype(F32)).astype(BF16)
    return hi, lo


def _dot(a, b):
    return jnp.dot(a, b, preferred_element_type=F32)


def _dot3(a, b):
    ah, al = _split(a)
    bh, bl = _split(b)
    return _dot(ah, bh) + _dot(al, bh) + _dot(ah, bl)


def _norm_mod(x, g, sh, sc):
    y = x * lax.rsqrt(jnp.mean(x * x, axis=-1, keepdims=True) + EPS)
    return (y * g) * (1.0 + sc) + sh


def _ada_kernel(c_ref, w_ref, b_ref, o_ref):
    c = c_ref[...]
    s = c * jax.nn.sigmoid(c)
    o_ref[0] = _dot3(s, w_ref[0]) + b_ref[0]


def _ada(cs, ada_w, ada_b):
    depth, d, n6 = ada_w.shape
    rows = cs.shape[0]
    tn = 1536
    return pl.pallas_call(
        _ada_kernel,
        out_shape=jax.ShapeDtypeStruct((depth, rows, n6), F32),
        grid=(depth, n6 // tn),
        in_specs=[pl.BlockSpec((rows, d), lambda l, j: (0, 0)),
                  pl.BlockSpec((1, d, tn), lambda l, j: (l, 0, j)),
                  pl.BlockSpec((1, 1, tn), lambda l, j: (l, 0, j))],
        out_specs=pl.BlockSpec((1, rows, tn), lambda l, j: (l, 0, j)),
        compiler_params=_params(("parallel", "parallel")),
        name="adaln",
    )(cs, ada_w, ada_b.reshape(depth, 1, n6))


def _inproj_kernel(x_ref, g_ref, sh_ref, sc_ref, cos_ref, sin_ref, w_ref,
                   q_ref, k_ref, v_ref, u_ref, *, d_attn, d_hy):
    h = _norm_mod(x_ref[...], g_ref[...], sh_ref[0], sc_ref[0]).astype(BF16)
    cos = cos_ref[...]
    sin = sin_ref[...]
    cw = cos.shape[1]
    quarter = DA_HEAD_DIM // 4
    lane = lax.broadcasted_iota(jnp.int32, cos.shape, 1)
    first = (lane % (2 * quarter)) < quarter

    def mm(c0, c1):
        return _dot(h, w_ref[:, c0:c1])

    def rope(a):
        partner = jnp.where(first, pltpu.roll(a, cw - quarter, 1), pltpu.roll(a, quarter, 1))
        return a * cos + partner * sin

    for j in range(d_attn // cw):
        q_ref[:, j * cw:(j + 1) * cw] = (rope(mm(j * cw, (j + 1) * cw)) * (DA_HEAD_DIM ** -0.5)).astype(BF16)
        k_ref[:, j * cw:(j + 1) * cw] = rope(mm(d_attn + j * cw, d_attn + (j + 1) * cw)).astype(BF16)
    v_ref[...] = mm(2 * d_attn, 3 * d_attn).astype(BF16)
    for j in range(d_hy // 512):
        u_ref[:, j * 512:(j + 1) * 512] = mm(3 * d_attn + j * 512, 3 * d_attn + (j + 1) * 512)


def _inproj(x2, g, sh, sc, cos, sin, w_ext, seq, tm=512):
    t, d = x2.shape
    tpb = seq // tm
    d_attn = DA_WIDTH
    d_hy = w_ext.shape[1] - 3 * d_attn
    cw = cos.shape[1]
    kern = functools.partial(_inproj_kernel, d_attn=d_attn, d_hy=d_hy)
    return pl.pallas_call(
        kern,
        out_shape=(jax.ShapeDtypeStruct((t, d_attn), BF16), jax.ShapeDtypeStruct((t, d_attn), BF16),
                   jax.ShapeDtypeStruct((t, d_attn), BF16), jax.ShapeDtypeStruct((t, d_hy), F32)),
        grid=(t // tm,),
        in_specs=[pl.BlockSpec((tm, d), lambda i: (i, 0)),
                  pl.BlockSpec((1, d), lambda i: (0, 0)),
                  pl.BlockSpec((1, 1, d), lambda i: (i // tpb, 0, 0)),
                  pl.BlockSpec((1, 1, d), lambda i: (i // tpb, 0, 0)),
                  pl.BlockSpec((tm, cw), lambda i: (i % tpb, 0)),
                  pl.BlockSpec((tm, cw), lambda i: (i % tpb, 0)),
                  pl.BlockSpec(w_ext.shape, lambda i: (0, 0))],
        out_specs=(pl.BlockSpec((tm, d_attn), lambda i: (i, 0)), pl.BlockSpec((tm, d_attn), lambda i: (i, 0)),
                   pl.BlockSpec((tm, d_attn), lambda i: (i, 0)), pl.BlockSpec((tm, d_hy), lambda i: (i, 0))),
        compiler_params=_params(("parallel",)),
        name="inproj0",
    )(x2, g, sh, sc, cos, sin, w_ext)


def _ctxproj_kernel(x_ref, g_ref, sh_ref, sc_ref, w_ref, k_ref, v_ref):
    h = _norm_mod(x_ref[...], g_ref[...], sh_ref[0], sc_ref[0]).astype(BF16)
    n = k_ref.shape[1]
    k_ref[...] = _dot(h, w_ref[:, :n]).astype(BF16)
    v_ref[...] = _dot(h, w_ref[:, n:]).astype(BF16)


def _ctxproj(c2, g, sh, sc, w_kv, tm):
    t, d = c2.shape
    n = w_kv.shape[1] // 2
    return pl.pallas_call(
        _ctxproj_kernel,
        out_shape=(jax.ShapeDtypeStruct((t, n), BF16), jax.ShapeDtypeStruct((t, n), BF16)),
        grid=(t // tm,),
        in_specs=[pl.BlockSpec((tm, d), lambda i: (i, 0)),
                  pl.BlockSpec((1, d), lambda i: (0, 0)),
                  pl.BlockSpec((1, 1, d), lambda i: (0, 0, 0)),
                  pl.BlockSpec((1, 1, d), lambda i: (0, 0, 0)),
                  pl.BlockSpec(w_kv.shape, lambda i: (0, 0))],
        out_specs=(pl.BlockSpec((tm, n), lambda i: (i, 0)), pl.BlockSpec((tm, n), lambda i: (i, 0))),
        compiler_params=_params(("parallel",)),
        name="ctxproj",
    )(c2, g, sh, sc, w_kv)


def _attn_kernel(lam_ref, g_ref, q_ref, kt_ref, v_ref, o_ref, *, lam_init):
    lp = lam_ref[...]
    lam = (jnp.exp(jnp.sum(lp[0:1] * lp[1:2], axis=-1, keepdims=True))
           - jnp.exp(jnp.sum(lp[2:3] * lp[3:4], axis=-1, keepdims=True)) + lam_init)
    hp = 2 * DA_HEAD_DIM
    scores = []
    for p in range(q_ref.shape[1] // hp):
        cols = slice(p * hp, (p + 1) * hp)
        q = q_ref[:, cols]
        lane = lax.broadcasted_iota(jnp.int32, q.shape, 1)
        zero = jnp.zeros_like(q)
        kt = kt_ref[0, cols, :]
        scores.append((_dot(jnp.where(lane < DA_HEAD_DIM, q, zero), kt),
                       _dot(jnp.where(lane >= DA_HEAD_DIM, q, zero), kt)))
    for p, (s1, s2) in enumerate(scores):
        cols = slice(p * hp, (p + 1) * hp)
        e1 = jnp.exp(s1 - jnp.max(s1, axis=-1, keepdims=True))
        e2 = jnp.exp(s2 - jnp.max(s2, axis=-1, keepdims=True))
        r1 = 1.0 / jnp.sum(e1, axis=-1, keepdims=True)
        r2 = lam / jnp.sum(e2, axis=-1, keepdims=True)
        a = (e1 * r1 - e2 * r2).astype(BF16)
        o = _dot(a, v_ref[0, :, cols])
        y = o * lax.rsqrt(jnp.mean(o * o, axis=-1, keepdims=True) + EPS)
        o_ref[:, cols] = (y * g_ref[...]) * (1.0 - lam_init)


def _attention(q, kt_all, v_all, lam_p, subln_g, seq, lam_init, tq=256, pairs=4):
    t = q.shape[0]
    b, _, lk = kt_all.shape
    hw = 2 * DA_HEAD_DIM * pairs
    nq = seq // tq
    kern = functools.partial(_attn_kernel, lam_init=lam_init)
    return pl.pallas_call(
        kern,
        out_shape=jax.ShapeDtypeStruct((t, DA_WIDTH), F32),
        grid=(b, DA_HEADS // pairs, nq),
        in_specs=[pl.BlockSpec(lam_p.shape, lambda bi, h, i: (0, 0)),
                  pl.BlockSpec((1, 2 * DA_HEAD_DIM), lambda bi, h, i: (0, 0)),
                  pl.BlockSpec((tq, hw), lambda bi, h, i: (bi * nq + i, h)),
                  pl.BlockSpec((1, hw, lk), lambda bi, h, i: (bi, h, 0)),
                  pl.BlockSpec((1, lk, hw), lambda bi, h, i: (bi, 0, h))],
        out_specs=pl.BlockSpec((tq, hw), lambda bi, h, i: (bi * nq + i, h)),
        compiler_params=_params(("parallel", "parallel", "parallel")),
        name="diff_attn",
    )(lam_p, subln_g, q, kt_all, v_all)


def _dwconv_kernel(x_ref, w_ref, b_ref, o_ref, pad_ref, *, kw, halo, channel_major):
    s = x_ref.shape[1]
    pl_ = (kw - 1) // 2
    zeros = jnp.zeros((halo, x_ref.shape[2]), F32)
    pad_ref[0:halo, :] = zeros
    pad_ref[halo + s:halo + s + halo, :] = zeros
    pad_ref[halo:halo + s, :] = x_ref[0]
    w = w_ref[...]
    acc = jnp.zeros((s, x_ref.shape[2]), F32) + b_ref[...]
    for j in range(kw):
        off = halo - pl_ + j
        acc = acc + pad_ref[off:off + s, :] * w[j:j + 1, :]
    if channel_major:
        acc_t = acc.T
        for g in range(LANES // SUBLANES):
            for blk in range(s // LANES):
                o_ref[g, blk] = acc_t[g * SUBLANES:(g + 1) * SUBLANES, blk * LANES:(blk + 1) * LANES]
    else:
        o_ref[0] = acc


def _dwconv(x3, w, bias, channel_major=False):
    b, s, c = x3.shape
    kw = w.shape[0]
    halo = 2 * SUBLANES
    kern = functools.partial(_dwconv_kernel, kw=kw, halo=halo, channel_major=channel_major)
    if channel_major:
        gpb = LANES // SUBLANES
        out_shape = jax.ShapeDtypeStruct((c // SUBLANES, s // LANES, b * SUBLANES, LANES), F32)
        out_spec = pl.BlockSpec((gpb, s // LANES, SUBLANES, LANES), lambda bi, ci: (ci, 0, bi, 0))
    else:
        out_shape = jax.ShapeDtypeStruct((b, s, c), F32)
        out_spec = pl.BlockSpec((1, s, LANES), lambda bi, ci: (bi, 0, ci))
    return pl.pallas_call(
        kern,
        out_shape=out_shape,
        grid=(b, c // LANES),
        in_specs=[pl.BlockSpec((1, s, LANES), lambda bi, ci: (bi, 0, ci)),
                  pl.BlockSpec((kw, LANES), lambda bi, ci: (0, ci)),
                  pl.BlockSpec((1, LANES), lambda bi, ci: (0, ci))],
        out_specs=out_spec,
        scratch_shapes=[pltpu.VMEM((s + 2 * halo, LANES), F32)],
        compiler_params=_params(("parallel", "parallel")),
        name="dwconv",
    )(x3, w, bias.reshape(1, c))


def _filter_kernel(zt_ref, t_ref, w1_ref, b1_ref, f1_ref, w2_ref, b2_ref, f2_ref, w3_ref, b3_ref, dl_ref,
                   o_ref, h_ref, *, n):
    @pl.when(pl.program_id(0) == 0)
    def _():
        h1 = jnp.sin(f1_ref[...] * (_dot3(w1_ref[...], zt_ref[...]) + b1_ref[...]))
        h_ref[...] = jnp.sin(f2_ref[...] * (_dot3(w2_ref[...], h1) + b2_ref[...]))

    h2 = h_ref[...]
    rows = o_ref.shape[0]
    fwd = _dot3(w3_ref[0], h2[:, :n]) + b3_ref[0]
    bwd = _dot3(w3_ref[1], h2[:, n:]) + b3_ref[1]
    decay = jnp.exp(-t_ref[...] * dl_ref[...])
    lane = lax.broadcasted_iota(jnp.int32, (rows, n), 1)
    kf = fwd * decay[:, :n]
    kb = jnp.where(lane == 0, 0.0, bwd * decay[:, n:])
    inv = 1.0 / (jnp.sum(jnp.abs(kf), axis=-1, keepdims=True) + jnp.sum(jnp.abs(kb), axis=-1, keepdims=True))
    kf = kf * inv
    kb = kb * inv
    o_ref[:, 0:LANES] = kb[:, n - LANES:]
    o_ref[:, LANES:LANES + n] = kf
    o_ref[:, LANES + n:] = kb


def _hyena_filters(n, w1, b1, fr1, w2, b2, fr2, w3, b3):
    emb, ffn = w1.shape
    c = w3.shape[1] // (2 * HY_ORDER)
    bands = (emb - 1) // 2
    t = jnp.linspace(0.0, 1.0, n, dtype=F32)[:, None]
    ang = (2.0 * math.pi / n) * jnp.arange(n, dtype=F32)[:, None] * jnp.linspace(1e-4, bands - 1, bands, dtype=F32)[None, :]
    z = jnp.concatenate([t, jnp.cos(ang), -jnp.sin(ang)], axis=-1)
    rev = (n - jnp.arange(n)) % n
    z2 = jnp.concatenate([z, z[rev]], axis=0)
    t2 = jnp.concatenate([t, t[rev]], axis=0).reshape(1, 2 * n)
    emb_p = ((emb + SUBLANES - 1) // SUBLANES) * SUBLANES
    zt = jnp.zeros((emb_p, 2 * n), F32).at[:emb].set(z2.T)
    w1t = jnp.zeros((ffn, emb_p), F32).at[:, :emb].set(w1.T)
    deltas = jnp.abs(jnp.linspace(HY_MIN_DECAY, HY_MAX_DECAY, c, dtype=F32))
    w3t = w3.T.reshape(HY_ORDER, 2, c, ffn).transpose(1, 0, 2, 3).reshape(2, HY_ORDER * c, ffn)
    b3t = b3.reshape(HY_ORDER, 2, c).transpose(1, 0, 2).reshape(2, HY_ORDER * c, 1)
    dl = jnp.tile(deltas, HY_ORDER).reshape(HY_ORDER * c, 1)
    rows = LANES
    kern = functools.partial(_filter_kernel, n=n)
    col = lambda v: v.reshape(ffn, 1)
    return pl.pallas_call(
        kern,
        out_shape=jax.ShapeDtypeStruct((HY_ORDER * c, LANES + 2 * n), F32),
        grid=(HY_ORDER * c // rows,),
        in_specs=[pl.BlockSpec(zt.shape, lambda i: (0, 0)),
                  pl.BlockSpec(t2.shape, lambda i: (0, 0)),
                  pl.BlockSpec(w1t.shape, lambda i: (0, 0)),
                  pl.BlockSpec((ffn, 1), lambda i: (0, 0)),
                  pl.BlockSpec((ffn, 1), lambda i: (0, 0)),
                  pl.BlockSpec((ffn, ffn), lambda i: (0, 0)),
                  pl.BlockSpec((ffn, 1), lambda i: (0, 0)),
                  pl.BlockSpec((ffn, 1), lambda i: (0, 0)),
                  pl.BlockSpec((2, rows, ffn), lambda i: (0, i, 0)),
                  pl.BlockSpec((2, rows, 1), lambda i: (0, i, 0)),
                  pl.BlockSpec((rows, 1), lambda i: (i, 0))],
        out_specs=pl.BlockSpec((rows, LANES + 2 * n), lambda i: (i, 0)),
        scratch_shapes=[pltpu.VMEM((ffn, 2 * n), F32)],
        compiler_params=_params(("arbitrary",)),
        name="hyena_filters",
    )(zt, t2, w1t, col(b1), col(fr1), w2.T, col(b2), col(fr2), w3t, b3t, dl)


def _longconv_kernel(zero_ref, z_ref, gate_ref, kc_ref, kcn_ref, bias_ref, o_ref, r_a, r_b, zs_ref,
                     *, nb, bsz, cb):
    p = LC_P
    n2 = 2 * nb * p
    w = min(LC_W, n2)
    nchunk = n2 // w
    dots_per_chunk = nb // nchunk

    def build_chunk(src, r_dst, ci):
        k_ref, c = src
        win = k_ref[pl.ds(c, 1), ci * w:ci * w + w + LANES]
        rolled = pltpu.roll(jnp.broadcast_to(win, (p, w + LANES)), 0, 1, stride=1, stride_axis=0)
        r_dst[:, ci * w:(ci + 1) * w] = rolled[:, LANES:].astype(BF16)
        bits = pltpu.bitcast(rolled[0:SUBLANES, LANES:2 * LANES], jnp.int32) & zero_ref[...]
        return pltpu.bitcast(bits, F32)[0:1, :].astype(BF16)

    def conv(c, r_src, c_next, r_next):
        for s1 in range(nb):
            zs_ref[s1 * bsz:(s1 + 1) * bsz, :] = z_ref[0, s1, pl.ds(c, bsz, stride=cb), :]
        acc = [None] * nb
        held = None
        for pi in range(nb):
            if pi % dots_per_chunk == 0:
                held = build_chunk(c_next, r_next, pi // dots_per_chunk)
            d = -nb + 2 * pi
            off = (d % (2 * nb)) * p
            lo = max(0, -d - 1)
            hi = min(nb, nb - d)
            lhs = zs_ref[lo * bsz:hi * bsz, :].astype(BF16) + held
            out = _dot(lhs, r_src[:, off:off + 2 * p])
            for k in range(2):
                dk = d + k
                for s1 in range(max(0, -dk), min(nb, nb - dk)):
                    blk = out[(s1 - lo) * bsz:(s1 - lo + 1) * bsz, k * p:(k + 1) * p]
                    acc[s1 + dk] = blk if acc[s1 + dk] is None else acc[s1 + dk] + blk
        bias = bias_ref[c]
        for s1 in range(nb):
            rows = slice(s1 * bsz, (s1 + 1) * bsz)
            where = pl.ds(c, bsz, stride=cb)
            o_ref[0, s1, where, :] = gate_ref[0, s1, where, :] * (acc[s1] + zs_ref[rows, :] * bias)

    @pl.when(pl.program_id(0) == 0)
    def _():
        for ci in range(nchunk):
            build_chunk((kc_ref, 0), r_a, ci)

    def pair(k, carry):
        c = 2 * k
        conv(c, r_a, (kc_ref, c + 1), r_b)
        conv(c + 1, r_b, (kc_ref, c + 2), r_a)
        return carry

    lax.fori_loop(0, cb // 2 - 1, pair, 0)
    conv(cb - 2, r_a, (kc_ref, cb - 1), r_b)
    conv(cb - 1, r_b, (kcn_ref, 0), r_a)


def _longconv(z_cm, z_g0, gate_cm, gate_g0, kc_ext, bias_cm, order, groups, nb, bsz):
    cb = SUBLANES
    rows, p = z_cm.shape[2:]
    blk = (1, nb, rows, p)
    kern = functools.partial(_longconv_kernel, nb=nb, bsz=bsz, cb=cb)
    return pl.pallas_call(
        kern,
        out_shape=jax.ShapeDtypeStruct((groups, nb, rows, p), F32),
        grid=(groups,),
        in_specs=[pl.BlockSpec((1, LANES), lambda i: (0, 0)),
                  pl.BlockSpec(blk, lambda i: (z_g0 + i, 0, 0, 0)),
                  pl.BlockSpec(blk, lambda i: (gate_g0 + i, 0, 0, 0)),
                  pl.BlockSpec((cb, kc_ext.shape[1]), lambda i: (order * groups + i, 0)),
                  pl.BlockSpec((cb, kc_ext.shape[1]), lambda i: (order * groups + jnp.minimum(i + 1, groups - 1), 0)),
                  pl.BlockSpec((cb, 1, p), lambda i: (order * groups + i, 0, 0))],
        out_specs=pl.BlockSpec(blk, lambda i: (i, 0, 0, 0)),
        scratch_shapes=[pltpu.VMEM((p, 2 * nb * p), BF16), pltpu.VMEM((p, 2 * nb * p), BF16),
                        pltpu.VMEM((nb * bsz, p), F32)],
        compiler_params=_params(("arbitrary",)),
        name="hyena_longconv",
    )(jnp.zeros((1, LANES), jnp.int32), z_cm, gate_cm, kc_ext, kc_ext, bias_cm)


def _outproj_kernel(x_ref, g1_ref, a_ref, b_ref, w_ref, o_ref):
    da = a_ref.shape[1]
    ya = _dot(a_ref[...].astype(BF16), w_ref[:da, :])
    groups, nblk, cg, p = b_ref.shape
    parts = []
    for blk in range(nblk):
        cm = b_ref[:, blk, :, :].reshape(groups * cg, p)
        parts.append(_dot(cm.T.astype(BF16), w_ref[da:, :]))
    y = ya + jnp.concatenate(parts, axis=0)
    o_ref[...] = x_ref[...] + g1_ref[0] * y


def _outproj(x2, g1, oa, ob, w, seq, tm=512):
    t, d = x2.shape
    tpb = seq // tm
    return pl.pallas_call(
        _outproj_kernel,
        out_shape=jax.ShapeDtypeStruct((t, d), F32),
        grid=(t // tm,),
        in_specs=[pl.BlockSpec((tm, d), lambda i: (i, 0)),
                  pl.BlockSpec((1, 1, d), lambda i: (i // tpb, 0, 0)),
                  pl.BlockSpec((tm, oa.shape[1]), lambda i: (i, 0)),
                  pl.BlockSpec((ob.shape[0], tm // LANES, SUBLANES, LANES), lambda i: (0, i % tpb, i // tpb, 0)),
                  pl.BlockSpec(w.shape, lambda i: (0, 0))],
        out_specs=pl.BlockSpec((tm, d), lambda i: (i, 0)),
        compiler_params=_params(("parallel",)),
        name="outproj0",
    )(x2, g1, oa, ob, w)


def _confin_kernel(x_ref, g_ref, sh_ref, sc_ref, w_ref, b_ref, o_ref):
    h = _norm_mod(x_ref[...], g_ref[...], sh_ref[0], sc_ref[0]).astype(BF16)
    n = o_ref.shape[1]
    a = _dot(h, w_ref[:, :n]) + b_ref[:, :n]
    gt = _dot(h, w_ref[:, n:]) + b_ref[:, n:]
    o_ref[...] = a * jax.nn.sigmoid(gt)


def _confin(x2, g, sh, sc, w1, b1, seq, tm=512):
    t, d = x2.shape
    n = w1.shape[1] // 2
    tpb = seq // tm
    return pl.pallas_call(
        _confin_kernel,
        out_shape=jax.ShapeDtypeStruct((t, n), F32),
        grid=(t // tm,),
        in_specs=[pl.BlockSpec((tm, d), lambda i: (i, 0)),
                  pl.BlockSpec((1, d), lambda i: (0, 0)),
                  pl.BlockSpec((1, 1, d), lambda i: (i // tpb, 0, 0)),
                  pl.BlockSpec((1, 1, d), lambda i: (i // tpb, 0, 0)),
                  pl.BlockSpec(w1.shape, lambda i: (0, 0)),
                  pl.BlockSpec((1, 2 * n), lambda i: (0, 0))],
        out_specs=pl.BlockSpec((tm, n), lambda i: (i, 0)),
        compiler_params=_params(("parallel",)),
        name="conformer_in",
    )(x2, g, sh, sc, w1, b1.reshape(1, 2 * n))


def _confout_kernel(x_ref, g1_ref, a_ref, lg_ref, lb_ref, w_ref, b_ref, o_ref):
    a = a_ref[...]
    mu = jnp.mean(a, axis=-1, keepdims=True)
    ac = a - mu
    var = jnp.mean(ac * ac, axis=-1, keepdims=True)
    y = ac * lax.rsqrt(var + EPS) * lg_ref[...] + lb_ref[...]
    y = y * jax.nn.sigmoid(y)
    o_ref[...] = x_ref[...] + g1_ref[0] * (_dot(y.astype(BF16), w_ref[...]) + b_ref[...])


def _confout(x2, g1, a2, ln_g, ln_b, w2, b2, seq, tm=512):
    t, d = x2.shape
    n = a2.shape[1]
    tpb = seq // tm
    return pl.pallas_call(
        _confout_kernel,
        out_shape=jax.ShapeDtypeStruct((t, d), F32),
        grid=(t // tm,),
        in_specs=[pl.BlockSpec((tm, d), lambda i: (i, 0)),
                  pl.BlockSpec((1, 1, d), lambda i: (i // tpb, 0, 0)),
                  pl.BlockSpec((tm, n), lambda i: (i, 0)),
                  pl.BlockSpec((1, n), lambda i: (0, 0)),
                  pl.BlockSpec((1, n), lambda i: (0, 0)),
                  pl.BlockSpec(w2.shape, lambda i: (0, 0)),
                  pl.BlockSpec((1, d), lambda i: (0, 0))],
        out_specs=pl.BlockSpec((tm, d), lambda i: (i, 0)),
        compiler_params=_params(("parallel",)),
        name="conformer_out",
    )(x2, g1, a2, ln_g.reshape(1, n), ln_b.reshape(1, n), w2, b2.reshape(1, d))


def _router_kernel(x_ref, g_ref, sh_ref, sc_ref, wr_ref, br_ref, tri_ref, h_ref, route_ref, cnt_ref, run_ref):
    i = pl.program_id(0)

    @pl.when(i == 0)
    def _():
        run_ref[...] = jnp.zeros_like(run_ref)

    h = _norm_mod(x_ref[...], g_ref[...], sh_ref[0], sc_ref[0])
    hh, hl = _split(h)
    h_ref[...] = hh
    logits = _dot(hh, wr_ref[0]) + _dot(hl, wr_ref[0]) + _dot(hh, wr_ref[1]) + br_ref[...]
    tm = logits.shape[0]
    lane = lax.broadcasted_iota(jnp.int32, (tm, LANES), 1)
    ninf = jnp.float32(-jnp.inf)

    def first_argmax(v, m):
        return jnp.min(jnp.where(v == m, lane, LANES), axis=-1, keepdims=True)

    gl = jnp.where(lane < N_GROUPS, logits, ninf)
    gmax = jnp.max(gl, axis=-1, keepdims=True)
    g_w = 1.0 / jnp.sum(jnp.exp(gl - gmax), axis=-1, keepdims=True)
    g_idx = first_argmax(gl, gmax)
    e_lo = ROUTE_LANE0 + EXPERTS_PER_GROUP * g_idx
    el = jnp.where((lane >= e_lo) & (lane < e_lo + EXPERTS_PER_GROUP), logits, ninf)
    m1 = jnp.max(el, axis=-1, keepdims=True)
    esum = jnp.sum(jnp.exp(el - m1), axis=-1, keepdims=True)
    i1 = first_argmax(el, m1)
    el2 = jnp.where(lane == i1, ninf, el)
    m2 = jnp.max(el2, axis=-1, keepdims=True)
    i2 = first_argmax(el2, m2)
    p1 = 1.0 / esum
    p2 = jnp.exp(m2 - m1) / esum
    w1 = g_w * (p1 / (p1 + p2))
    w2 = g_w * (p2 / (p1 + p2))

    oh = jnp.where((lane == i1) | (lane == i2), 1.0, 0.0)
    before = _dot(tri_ref[...], oh.astype(BF16)) + run_ref[...]
    rank1 = jnp.sum(jnp.where(lane == i1, before, 0.0), axis=-1, keepdims=True)
    rank2 = jnp.sum(jnp.where(lane == i2, before, 0.0), axis=-1, keepdims=True)
    run_ref[...] = run_ref[...] + jnp.sum(oh, axis=0, keepdims=True)
    cnt_ref[...] = run_ref[...]

    e1 = (i1 - ROUTE_LANE0).astype(F32)
    e2 = (i2 - ROUTE_LANE0).astype(F32)
    vals = (e1, e2, rank1, rank2, w1, w2)
    out = jnp.zeros((tm, LANES), F32)
    for k, v in enumerate(vals):
        out = jnp.where(lane == k, v, out)
    route_ref[...] = out


def _router(x2, g, sh, sc, wr, br, seq, tm=512):
    t, d = x2.shape
    tpb = seq // tm
    return pl.pallas_call(
        _router_kernel,
        out_shape=(jax.ShapeDtypeStruct((t, d), BF16), jax.ShapeDtypeStruct((t, LANES), F32),
                   jax.ShapeDtypeStruct((1, LANES), F32)),
        grid=(t // tm,),
        in_specs=[pl.BlockSpec((tm, d), lambda i: (i, 0)),
                  pl.BlockSpec((1, d), lambda i: (0, 0)),
                  pl.BlockSpec((1, 1, d), lambda i: (i // tpb, 0, 0)),
                  pl.BlockSpec((1, 1, d), lambda i: (i // tpb, 0, 0)),
                  pl.BlockSpec(wr.shape, lambda i: (0, 0, 0)),
                  pl.BlockSpec((1, LANES), lambda i: (0, 0)),
                  pl.BlockSpec((tm, tm), lambda i: (0, 0))],
        out_specs=(pl.BlockSpec((tm, d), lambda i: (i, 0)), pl.BlockSpec((tm, LANES), lambda i: (i, 0)),
                   pl.BlockSpec((1, LANES), lambda i: (0, 0))),
        scratch_shapes=[pltpu.VMEM((1, LANES), F32)],
        compiler_params=_params(("arbitrary",)),
        name="moe_router",
    )(x2, g, sh, sc, wr, br, jnp.tril(jnp.ones((tm, tm), BF16), -1))


def _expert_kernel(te_ref, nv_ref, x_ref, wg_ref, wu_ref, wd_ref, o_ref, wg_s, wu_s, wd_s):
    i = pl.program_id(0)

    @pl.when((i == 0) | (te_ref[i] != te_ref[jnp.maximum(i - 1, 0)]))
    def _():
        wg_s[...] = wg_ref[0].astype(BF16)
        wu_s[...] = wu_ref[0].astype(BF16)
        wd_s[...] = wd_ref[0].astype(BF16)

    @pl.when(i < nv_ref[0])
    def _():
        x = x_ref[...]
        a = _dot(x, wg_s[...])
        u = _dot(x, wu_s[...])
        he = (a * jax.nn.sigmoid(a)) * u
        o_ref[...] = _dot(he.astype(BF16), wd_s[...]).astype(o_ref.dtype)


def _experts(xs, tile_expert, n_valid, wg, wu, wd, layer):
    r, d = xs.shape
    de = wg.shape[2]
    nt = r // MOE_TILE
    row = lambda i, te, nv: (jnp.minimum(i, nv[0] - 1), 0)
    wsel = lambda i, te, nv: (layer * N_EXPERTS + te[i], 0, 0)
    return pl.pallas_call(
        _expert_kernel,
        out_shape=jax.ShapeDtypeStruct((r, d), BF16),
        grid_spec=pltpu.PrefetchScalarGridSpec(
            num_scalar_prefetch=2,
            grid=(nt,),
            in_specs=[pl.BlockSpec((MOE_TILE, d), row),
                      pl.BlockSpec((1, d, de), wsel),
                      pl.BlockSpec((1, d, de), wsel),
                      pl.BlockSpec((1, de, d), wsel)],
            out_specs=pl.BlockSpec((MOE_TILE, d), row),
            scratch_shapes=[pltpu.VMEM((d, de), BF16), pltpu.VMEM((d, de), BF16), pltpu.VMEM((de, d), BF16)]),
        compiler_params=_params(("arbitrary",)),
        name="moe_experts",
    )(tile_expert, n_valid, xs, wg, wu, wd)


def _combine_kernel(x_ref, g2_ref, route_ref, y1_ref, y2_ref, fg_ref, o_ref, *, final):
    r = route_ref[...]
    w1 = r[:, 4:5]
    w2 = r[:, 5:6]
    x = x_ref[...] + g2_ref[0] * (w1 * y1_ref[...].astype(F32) + w2 * y2_ref[...].astype(F32))
    if final:
        x = (x * lax.rsqrt(jnp.mean(x * x, axis=-1, keepdims=True) + EPS)) * fg_ref[...]
    o_ref[...] = x


def _combine(x2, g2, route, y12, final_g, seq, final, tm=512):
    t, d = x2.shape
    tpb = seq // tm
    nt = t // tm
    kern = functools.partial(_combine_kernel, final=final)
    return pl.pallas_call(
        kern,
        out_shape=jax.ShapeDtypeStruct((t, d), F32),
        grid=(t // tm,),
        in_specs=[pl.BlockSpec((tm, d), lambda i: (i, 0)),
                  pl.BlockSpec((1, 1, d), lambda i: (i // tpb, 0, 0)),
                  pl.BlockSpec((tm, LANES), lambda i: (i, 0)),
                  pl.BlockSpec((tm, d), lambda i: (i, 0)),
                  pl.BlockSpec((tm, d), lambda i: (nt + i, 0)),
                  pl.BlockSpec((1, d), lambda i: (0, 0))],
        out_specs=pl.BlockSpec((tm, d), lambda i: (i, 0)),
        compiler_params=_params(("parallel",)),
        name="moe_combine",
    )(x2, g2, route, y12, y12, final_g)


def _moe(x2, g, sh, sc, g2, wg_r, bg_r, we_r, be_r, w_gate, w_up, w_down, layer, final_g, seq, final):
    t, d = x2.shape
    wr = jnp.zeros((d, LANES), F32).at[:, :N_GROUPS].set(wg_r).at[:, ROUTE_LANE0:ROUTE_LANE0 + N_EXPERTS].set(we_r)
    wr_hi = wr.astype(BF16)
    wr_lo = (wr - wr_hi.astype(F32)).astype(BF16)
    br = jnp.zeros((1, LANES), F32).at[0, :N_GROUPS].set(bg_r).at[0, ROUTE_LANE0:ROUTE_LANE0 + N_EXPERTS].set(be_r)
    h, route, cnt = _router(x2, g, sh, sc, jnp.stack([wr_hi, wr_lo]), br, seq)

    counts = cnt[0, ROUTE_LANE0:ROUTE_LANE0 + N_EXPERTS].astype(jnp.int32)
    tiles = (counts + MOE_TILE - 1) // MOE_TILE
    tile_end = jnp.cumsum(tiles)
    offs = (tile_end - tiles) * MOE_TILE
    nt = (2 * t) // MOE_TILE + N_EXPERTS
    tile_id = jnp.minimum(jnp.arange(nt, dtype=jnp.int32), tile_end[-1] - 1)
    tile_expert = jnp.sum((tile_id[:, None] >= tile_end[None, :]).astype(jnp.int32), axis=1)
    n_valid = tile_end[-1:].astype(jnp.int32)
    e12 = route[:, 0:2].astype(jnp.int32)
    offs12 = jnp.sum(jnp.where(e12[:, :, None] == jnp.arange(N_EXPERTS)[None, None, :], offs[None, None, :], 0), axis=-1)
    pos = (offs12 + route[:, 2:4].astype(jnp.int32)).T.reshape(-1)
    tok = jnp.tile(jnp.arange(t, dtype=jnp.int32), 2)
    _, tok_by_row = lax.sort_key_val(pos, tok)
    row = jnp.arange(nt * MOE_TILE, dtype=jnp.int32)
    per_row = lambda v: jnp.repeat(v[tile_expert], MOE_TILE)
    in_e = row - per_row(offs)
    compact = per_row(jnp.cumsum(counts) - counts) + in_e
    sorted_tok = jnp.where(in_e < per_row(counts),
                           tok_by_row.at[jnp.minimum(compact, 2 * t - 1)].get(mode="promise_in_bounds"), row % t)
    xs = h.at[sorted_tok].get(mode="promise_in_bounds")
    ys = _experts(xs, tile_expert, n_valid, w_gate, w_up, w_down, layer)
    y12 = ys.at[pos].get(mode="promise_in_bounds")
    return _combine(x2, g2, route, y12, final_g, seq, final)


def _rope_tables(seq, width):
    hd = DA_HEAD_DIM
    half = hd // 2
    quarter = half // 2
    pos = jnp.arange(seq)
    row = (pos // GRID_W).astype(F32)
    col = (pos % GRID_W).astype(F32)
    inv = ROPE_THETA ** (-jnp.arange(0, half, 2, dtype=F32) / half)
    i = jnp.arange(hd)
    p = jnp.where((i < half)[None, :], row[:, None], col[:, None])
    ang = p * inv[i % quarter][None, :]
    sign = jnp.where((i % half) < quarter, -1.0, 1.0)[None, :]
    reps = width // hd
    return jnp.tile(jnp.cos(ang), (1, reps)), jnp.tile(jnp.sin(ang) * sign, (1, reps))


def kernel(x, c, ctx, c_ctx, ada_w, ada_b, norm1_g, norm2_g, final_g, w_in0, w_out0, lam_q1, lam_k1, lam_q2, lam_k2, subln_g, hy_short_w, hy_short_b, hy_w1, hy_b1, hy_fr1, hy_w2, hy_b2, hy_fr2, hy_w3, hy_b3, hy_bias, cv_w1, cv_b1, cv_dw_w, cv_dw_b, cv_ln_g, cv_ln_b, cv_w2, cv_b2, moe_wg, moe_bg, moe_we, moe_be, moe_w_gate, moe_w_up, moe_w_down):
    bsz, seq, d = x.shape
    lctx = ctx.shape[1]
    depth = ada_w.shape[0]
    t = bsz * seq
    hyw = d - DA_WIDTH
    x2 = x.reshape(t, d)

    rows = ((bsz + 1 + SUBLANES - 1) // SUBLANES) * SUBLANES
    cs = jnp.zeros((rows, d), F32).at[:bsz].set(c).at[bsz].set(c_ctx)
    mods = _ada(cs, ada_w, ada_b)

    def tok_mod(i, k):
        return mods[i, :bsz, k * d:(k + 1) * d].reshape(bsz, 1, d)

    def ctx_mod(i, k):
        return mods[i, bsz:bsz + 1, k * d:(k + 1) * d].reshape(1, 1, d)

    for i in range(depth):
        j = i // 2
        g_n1 = norm1_g[i].reshape(1, d)
        if i % 2 == 0:
            assert not any(m % 2 == 0 for m in range(i + 1, depth)), "context-stream update is not implemented"
            lam_init = 0.8 - 0.6 * math.exp(-0.3 * i)
            w_in = w_in0[j]
            cw = 2 * LANES
            cos, sin = _rope_tables(seq, cw)
            q, k, v, u = _inproj(x2, g_n1, tok_mod(i, 0), tok_mod(i, 1), cos, sin, w_in.astype(BF16), seq)
            kc, vc = _ctxproj(ctx.reshape(bsz * lctx, d), g_n1, ctx_mod(i, 0), ctx_mod(i, 1),
                              w_in[:, DA_WIDTH:3 * DA_WIDTH].astype(BF16), lctx)
            lam_p = jnp.stack([lam_q1[j], lam_k1[j], lam_q2[j], lam_k2[j]])
            k_all = jnp.concatenate([kc.reshape(bsz, lctx, DA_WIDTH), k.reshape(bsz, seq, DA_WIDTH)], axis=1)
            v_all = jnp.concatenate([vc.reshape(bsz, lctx, DA_WIDTH), v.reshape(bsz, seq, DA_WIDTH)], axis=1)
            o_a = _attention(q, jnp.swapaxes(k_all, 1, 2), v_all, lam_p, subln_g[j].reshape(1, -1), seq, lam_init)

            nb = seq // LC_P
            groups = hyw // SUBLANES
            ucm = _dwconv(u.reshape(bsz, seq, 3 * hyw), hy_short_w[j], hy_short_b[j], channel_major=True)
            kc_ext = _hyena_filters(seq, hy_w1[j], hy_b1[j], hy_fr1[j], hy_w2[j], hy_b2[j], hy_fr2[j],
                                    hy_w3[j], hy_b3[j])
            bias_cm = jnp.broadcast_to(hy_bias[j].reshape(HY_ORDER * hyw, 1, 1), (HY_ORDER * hyw, 1, LC_P))
            z1 = _longconv(ucm, 0, ucm, groups, kc_ext, bias_cm, 0, groups, nb, bsz)
            z2 = _longconv(z1, 0, ucm, 2 * groups, kc_ext, bias_cm, 1, groups, nb, bsz)
            x2 = _outproj(x2, tok_mod(i, 2), o_a, z2, w_out0[j].astype(BF16), seq)
        else:
            a = _confin(x2, g_n1, tok_mod(i, 0), tok_mod(i, 1), cv_w1[j].astype(BF16), cv_b1[j], seq)
            a = _dwconv(a.reshape(bsz, seq, -1), cv_dw_w[j], cv_dw_b[j]).reshape(t, -1)
            x2 = _confout(x2, tok_mod(i, 2), a, cv_ln_g[j], cv_ln_b[j], cv_w2[j].astype(BF16), cv_b2[j], seq)
        x2 = _moe(x2, norm2_g[i].reshape(1, d), tok_mod(i, 3), tok_mod(i, 4), tok_mod(i, 5),
                  moe_wg[i], moe_bg[i], moe_we[i], moe_be[i],
                  moe_w_gate.reshape((-1,) + moe_w_gate.shape[2:]), moe_w_up.reshape((-1,) + moe_w_up.shape[2:]),
                  moe_w_down.reshape((-1,) + moe_w_down.shape[2:]), i,
                  final_g.reshape(1, d), seq, final=(i == depth - 1))
    return x2.reshape(bsz, seq, d)
```

```python
import functools
import math

import jax
import jax.numpy as jnp
from jax import lax
from jax.experimental import pallas as pl
from jax.experimental.pallas import tpu as pltpu

F32 = jnp.float32
BF16 = jnp.bfloat16

GRID_W = 64
DA_HEADS = 4
DA_HEAD_DIM = 64
DA_WIDTH = DA_HEADS * 2 * DA_HEAD_DIM
HY_ORDER = 2
HY_TARGET = 1e-2
HY_MIN_DECAY = math.log(HY_TARGET) / 0.3
HY_MAX_DECAY = math.log(HY_TARGET) / 1.5
N_GROUPS = 4
EXPERTS_PER_GROUP = 8
N_EXPERTS = N_GROUPS * EXPERTS_PER_GROUP
ROPE_THETA = 10000.0
EPS = 1e-6

LANES = 128
SUBLANES = 8
VMEM_LIMIT = 52 * 1024 * 1024
ROUTE_LANE0 = N_GROUPS
MOE_TILE = 256
LC_P = 128
LC_W = 4096


def _params(sem, flags=None):
    return pltpu.CompilerParams(dimension_semantics=sem, vmem_limit_bytes=VMEM_LIMIT, flags=flags)


def _split(a):
    hi = a.astype(BF16)
    lo = (a - hi.astype(F32)).astype(BF16)
    return hi, lo


def _dot(a, b):
    return jnp.dot(a, b, preferred_element_type=F32)


def _dot3(a, b):
    ah, al = _split(a)
    bh, bl = _split(b)
    return _dot(ah, bh) + _dot(al, bh) + _dot(ah, bl)


def _norm_mod(x, g, sh, sc):
    y = x * lax.rsqrt(jnp.mean(x * x, axis=-1, keepdims=True) + EPS)
    return (y * g) * (1.0 + sc) + sh


def _ada_kernel(c_ref, w_ref, b_ref, o_ref):
    c = c_ref[...]
    s = c * jax.nn.sigmoid(c)
    o_ref[0] = _dot3(s, w_ref[0]) + b_ref[0]


def _ada(cs, ada_w, ada_b):
    depth, d, n6 = ada_w.shape
    rows = cs.shape[0]
    tn = 1536
    return pl.pallas_call(
        _ada_kernel,
        out_shape=jax.ShapeDtypeStruct((depth, rows, n6), F32),
        grid=(depth, n6 // tn),
        in_specs=[pl.BlockSpec((rows, d), lambda l, j: (0, 0)),
                  pl.BlockSpec((1, d, tn), lambda l, j: (l, 0, j)),
                  pl.BlockSpec((1, 1, tn), lambda l, j: (l, 0, j))],
        out_specs=pl.BlockSpec((1, rows, tn), lambda l, j: (l, 0, j)),
        compiler_params=_params(("parallel", "parallel")),
        name="adaln",
    )(cs, ada_w, ada_b.reshape(depth, 1, n6))


def _inproj_kernel(x_ref, g_ref, sh_ref, sc_ref, cos_ref, sin_ref, w_ref,
                   q_ref, k_ref, v_ref, u_ref, *, d_attn, d_hy):
    h = _norm_mod(x_ref[...], g_ref[...], sh_ref[0], sc_ref[0]).astype(BF16)
    cos = cos_ref[...]
    sin = sin_ref[...]
    cw = cos.shape[1]
    quarter = DA_HEAD_DIM // 4
    lane = lax.broadcasted_iota(jnp.int32, cos.shape, 1)
    first = (lane % (2 * quarter)) < quarter

    def mm(c0, c1):
        return _dot(h, w_ref[:, c0:c1])

    def rope(a):
        partner = jnp.where(first, pltpu.roll(a, cw - quarter, 1), pltpu.roll(a, quarter, 1))
        return a * cos + partner * sin

    for j in range(d_attn // cw):
        q_ref[:, j * cw:(j + 1) * cw] = (rope(mm(j * cw, (j + 1) * cw)) * (DA_HEAD_DIM ** -0.5)).astype(BF16)
        k_ref[:, j * cw:(j + 1) * cw] = rope(mm(d_attn + j * cw, d_attn + (j + 1) * cw)).astype(BF16)
    v_ref[...] = mm(2 * d_attn, 3 * d_attn).astype(BF16)
    for j in range(d_hy // 512):
        u_ref[:, j * 512:(j + 1) * 512] = mm(3 * d_attn + j * 512, 3 * d_attn + (j + 1) * 512)


def _inproj(x2, g, sh, sc, cos, sin, w_ext, seq, tm=512):
    t, d = x2.shape
    tpb = seq // tm
    d_attn = DA_WIDTH
    d_hy = w_ext.shape[1] - 3 * d_attn
    cw = cos.shape[1]
    kern = functools.partial(_inproj_kernel, d_attn=d_attn, d_hy=d_hy)
    return pl.pallas_call(
        kern,
        out_shape=(jax.ShapeDtypeStruct((t, d_attn), BF16), jax.ShapeDtypeStruct((t, d_attn), BF16),
                   jax.ShapeDtypeStruct((t, d_attn), BF16), jax.ShapeDtypeStruct((t, d_hy), F32)),
        grid=(t // tm,),
        in_specs=[pl.BlockSpec((tm, d), lambda i: (i, 0)),
                  pl.BlockSpec((1, d), lambda i: (0, 0)),
                  pl.BlockSpec((1, 1, d), lambda i: (i // tpb, 0, 0)),
                  pl.BlockSpec((1, 1, d), lambda i: (i // tpb, 0, 0)),
                  pl.BlockSpec((tm, cw), lambda i: (i % tpb, 0)),
                  pl.BlockSpec((tm, cw), lambda i: (i % tpb, 0)),
                  pl.BlockSpec(w_ext.shape, lambda i: (0, 0))],
        out_specs=(pl.BlockSpec((tm, d_attn), lambda i: (i, 0)), pl.BlockSpec((tm, d_attn), lambda i: (i, 0)),
                   pl.BlockSpec((tm, d_attn), lambda i: (i, 0)), pl.BlockSpec((tm, d_hy), lambda i: (i, 0))),
        compiler_params=_params(("parallel",)),
        name="inproj0",
    )(x2, g, sh, sc, cos, sin, w_ext)


def _ctxproj_kernel(x_ref, g_ref, sh_ref, sc_ref, w_ref, k_ref, v_ref):
    h = _norm_mod(x_ref[...], g_ref[...], sh_ref[0], sc_ref[0]).astype(BF16)
    n = k_ref.shape[1]
    k_ref[...] = _dot(h, w_ref[:, :n]).astype(BF16)
    v_ref[...] = _dot(h, w_ref[:, n:]).astype(BF16)


def _ctxproj(c2, g, sh, sc, w_kv, tm):
    t, d = c2.shape
    n = w_kv.shape[1] // 2
    return pl.pallas_call(
        _ctxproj_kernel,
        out_shape=(jax.ShapeDtypeStruct((t, n), BF16), jax.ShapeDtypeStruct((t, n), BF16)),
        grid=(t // tm,),
        in_specs=[pl.BlockSpec((tm, d), lambda i: (i, 0)),
                  pl.BlockSpec((1, d), lambda i: (0, 0)),
                  pl.BlockSpec((1, 1, d), lambda i: (0, 0, 0)),
                  pl.BlockSpec((1, 1, d), lambda i: (0, 0, 0)),
                  pl.BlockSpec(w_kv.shape, lambda i: (0, 0))],
        out_specs=(pl.BlockSpec((tm, n), lambda i: (i, 0)), pl.BlockSpec((tm, n), lambda i: (i, 0))),
        compiler_params=_params(("parallel",)),
        name="ctxproj",
    )(c2, g, sh, sc, w_kv)


def _attn_kernel(lam_ref, g_ref, q_ref, kt_ref, v_ref, o_ref, *, lam_init):
    lp = lam_ref[...]
    lam = (jnp.exp(jnp.sum(lp[0:1] * lp[1:2], axis=-1, keepdims=True))
           - jnp.exp(jnp.sum(lp[2:3] * lp[3:4], axis=-1, keepdims=True)) + lam_init)
    hp = 2 * DA_HEAD_DIM
    scores = []
    for p in range(q_ref.shape[1] // hp):
        cols = slice(p * hp, (p + 1) * hp)
        q = q_ref[:, cols]
        lane = lax.broadcasted_iota(jnp.int32, q.shape, 1)
        zero = jnp.zeros_like(q)
        kt = kt_ref[0, cols, :]
        scores.append((_dot(jnp.where(lane < DA_HEAD_DIM, q, zero), kt),
                       _dot(jnp.where(lane >= DA_HEAD_DIM, q, zero), kt)))
    for p, (s1, s2) in enumerate(scores):
        cols = slice(p * hp, (p + 1) * hp)
        e1 = jnp.exp(s1 - jnp.max(s1, axis=-1, keepdims=True))
        e2 = jnp.exp(s2 - jnp.max(s2, axis=-1, keepdims=True))
        r1 = 1.0 / jnp.sum(e1, axis=-1, keepdims=True)
        r2 = lam / jnp.sum(e2, axis=-1, keepdims=True)
        a = (e1 * r1 - e2 * r2).astype(BF16)
        o = _dot(a, v_ref[0, :, cols])
        y = o * lax.rsqrt(jnp.mean(o * o, axis=-1, keepdims=True) + EPS)
        o_ref[:, cols] = (y * g_ref[...]) * (1.0 - lam_init)


def _attention(q, kt_all, v_all, lam_p, subln_g, seq, lam_init, tq=256, pairs=4):
    t = q.shape[0]
    b, _, lk = kt_all.shape
    hw = 2 * DA_HEAD_DIM * pairs
    nq = seq // tq
    kern = functools.partial(_attn_kernel, lam_init=lam_init)
    return pl.pallas_call(
        kern,
        out_shape=jax.ShapeDtypeStruct((t, DA_WIDTH), F32),
        grid=(b, DA_HEADS // pairs, nq),
        in_specs=[pl.BlockSpec(lam_p.shape, lambda bi, h, i: (0, 0)),
                  pl.BlockSpec((1, 2 * DA_HEAD_DIM), lambda bi, h, i: (0, 0)),
                  pl.BlockSpec((tq, hw), lambda bi, h, i: (bi * nq + i, h)),
                  pl.BlockSpec((1, hw, lk), lambda bi, h, i: (bi, h, 0)),
                  pl.BlockSpec((1, lk, hw), lambda bi, h, i: (bi, 0, h))],
        out_specs=pl.BlockSpec((tq, hw), lambda bi, h, i: (bi * nq + i, h)),
        compiler_params=_params(("parallel", "parallel", "parallel")),
        name="diff_attn",
    )(lam_p, subln_g, q, kt_all, v_all)


def _dwconv_kernel(x_ref, w_ref, b_ref, o_ref, pad_ref, *, kw, halo, channel_major):
    s = x_ref.shape[1]
    pl_ = (kw - 1) // 2
    zeros = jnp.zeros((halo, x_ref.shape[2]), F32)
    pad_ref[0:halo, :] = zeros
    pad_ref[halo + s:halo + s + halo, :] = zeros
    pad_ref[halo:halo + s, :] = x_ref[0]
    w = w_ref[...]
    acc = jnp.zeros((s, x_ref.shape[2]), F32) + b_ref[...]
    for j in range(kw):
        off = halo - pl_ + j
        acc = acc + pad_ref[off:off + s, :] * w[j:j + 1, :]
    if channel_major:
        acc_t = acc.T
        for g in range(LANES // SUBLANES):
            for blk in range(s // LANES):
                o_ref[g, blk] = acc_t[g * SUBLANES:(g + 1) * SUBLANES, blk * LANES:(blk + 1) * LANES]
    else:
        o_ref[0] = acc


def _dwconv(x3, w, bias, channel_major=False):
    b, s, c = x3.shape
    kw = w.shape[0]
    halo = 2 * SUBLANES
    kern = functools.partial(_dwconv_kernel, kw=kw, halo=halo, channel_major=channel_major)
    if channel_major:
        gpb = LANES // SUBLANES
        out_shape = jax.ShapeDtypeStruct((c // SUBLANES, s // LANES, b * SUBLANES, LANES), F32)
        out_spec = pl.BlockSpec((gpb, s // LANES, SUBLANES, LANES), lambda bi, ci: (ci, 0, bi, 0))
    else:
        out_shape = jax.ShapeDtypeStruct((b, s, c), F32)
        out_spec = pl.BlockSpec((1, s, LANES), lambda bi, ci: (bi, 0, ci))
    return pl.pallas_call(
        kern,
        out_shape=out_shape,
        grid=(b, c // LANES),
        in_specs=[pl.BlockSpec((1, s, LANES), lambda bi, ci: (bi, 0, ci)),
                  pl.BlockSpec((kw, LANES), lambda bi, ci: (0, ci)),
                  pl.BlockSpec((1, LANES), lambda bi, ci: (0, ci))],
        out_specs=out_spec,
        scratch_shapes=[pltpu.VMEM((s + 2 * halo, LANES), F32)],
        compiler_params=_params(("parallel", "parallel")),
        name="dwconv",
    )(x3, w, bias.reshape(1, c))


def _filter_kernel(zt_ref, t_ref, w1_ref, b1_ref, f1_ref, w2_ref, b2_ref, f2_ref, w3_ref, b3_ref, dl_ref,
                   o_ref, h_ref, *, n):
    @pl.when(pl.program_id(0) == 0)
    def _():
        h1 = jnp.sin(f1_ref[...] * (_dot3(w1_ref[...], zt_ref[...]) + b1_ref[...]))
        h_ref[...] = jnp.sin(f2_ref[...] * (_dot3(w2_ref[...], h1) + b2_ref[...]))

    h2 = h_ref[...]
    rows = o_ref.shape[0]
    fwd = _dot3(w3_ref[0], h2[:, :n]) + b3_ref[0]
    bwd = _dot3(w3_ref[1], h2[:, n:]) + b3_ref[1]
    decay = jnp.exp(-t_ref[...] * dl_ref[...])
    lane = lax.broadcasted_iota(jnp.int32, (rows, n), 1)
    kf = fwd * decay[:, :n]
    kb = jnp.where(lane == 0, 0.0, bwd * decay[:, n:])
    inv = 1.0 / (jnp.sum(jnp.abs(kf), axis=-1, keepdims=True) + jnp.sum(jnp.abs(kb), axis=-1, keepdims=True))
    kf = kf * inv
    kb = kb * inv
    o_ref[:, 0:LANES] = kb[:, n - LANES:]
    o_ref[:, LANES:LANES + n] = kf
    o_ref[:, LANES + n:] = kb


def _hyena_filters(n, w1, b1, fr1, w2, b2, fr2, w3, b3):
    emb, ffn = w1.shape
    c = w3.shape[1] // (2 * HY_ORDER)
    bands = (emb - 1) // 2
    t = jnp.linspace(0.0, 1.0, n, dtype=F32)[:, None]
    ang = (2.0 * math.pi / n) * jnp.arange(n, dtype=F32)[:, None] * jnp.linspace(1e-4, bands - 1, bands, dtype=F32)[None, :]
    z = jnp.concatenate([t, jnp.cos(ang), -jnp.sin(ang)], axis=-1)
    rev = (n - jnp.arange(n)) % n
    z2 = jnp.concatenate([z, z[rev]], axis=0)
    t2 = jnp.concatenate([t, t[rev]], axis=0).reshape(1, 2 * n)
    emb_p = ((emb + SUBLANES - 1) // SUBLANES) * SUBLANES
    zt = jnp.zeros((emb_p, 2 * n), F32).at[:emb].set(z2.T)
    w1t = jnp.zeros((ffn, emb_p), F32).at[:, :emb].set(w1.T)
    deltas = jnp.abs(jnp.linspace(HY_MIN_DECAY, HY_MAX_DECAY, c, dtype=F32))
    w3t = w3.T.reshape(HY_ORDER, 2, c, ffn).transpose(1, 0, 2, 3).reshape(2, HY_ORDER * c, ffn)
    b3t = b3.reshape(HY_ORDER, 2, c).transpose(1, 0, 2).reshape(2, HY_ORDER * c, 1)
    dl = jnp.tile(deltas, HY_ORDER).reshape(HY_ORDER * c, 1)
    rows = LANES
    kern = functools.partial(_filter_kernel, n=n)
    col = lambda v: v.reshape(ffn, 1)
    return pl.pallas_call(
        kern,
        out_shape=jax.ShapeDtypeStruct((HY_ORDER * c, LANES + 2 * n), F32),
        grid=(HY_ORDER * c // rows,),
        in_specs=[pl.BlockSpec(zt.shape, lambda i: (0, 0)),
                  pl.BlockSpec(t2.shape, lambda i: (0, 0)),
                  pl.BlockSpec(w1t.shape, lambda i: (0, 0)),
                  pl.BlockSpec((ffn, 1), lambda i: (0, 0)),
                  pl.BlockSpec((ffn, 1), lambda i: (0, 0)),
                  pl.BlockSpec((ffn, ffn), lambda i: (0, 0)),
                  pl.BlockSpec((ffn, 1), lambda i: (0, 0)),
                  pl.BlockSpec((ffn, 1), lambda i: (0, 0)),
                  pl.BlockSpec((2, rows, ffn), lambda i: (0, i, 0)),
                  pl.BlockSpec((2, rows, 1), lambda i: (0, i, 0)),
                  pl.BlockSpec((rows, 1), lambda i: (i, 0))],
        out_specs=pl.BlockSpec((rows, LANES + 2 * n), lambda i: (i, 0)),
        scratch_shapes=[pltpu.VMEM((ffn, 2 * n), F32)],
        compiler_params=_params(("arbitrary",)),
        name="hyena_filters",
    )(zt, t2, w1t, col(b1), col(fr1), w2.T, col(b2), col(fr2), w3t, b3t, dl)


def _longconv_kernel(zero_ref, z_ref, gate_ref, kc_ref, kcn_ref, bias_ref, o_ref, r_a, r_b, zs_ref,
                     *, nb, bsz, cb):
    p = LC_P
    n2 = 2 * nb * p
    w = min(LC_W, n2)
    nchunk = n2 // w
    dots_per_chunk = nb // nchunk

    def build_chunk(src, r_dst, ci):
        k_ref, c = src
        win = k_ref[pl.ds(c, 1), ci * w:ci * w + w + LANES]
        rolled = pltpu.roll(jnp.broadcast_to(win, (p, w + LANES)), 0, 1, stride=1, stride_axis=0)
        r_dst[:, ci * w:(ci + 1) * w] = rolled[:, LANES:].astype(BF16)
        bits = pltpu.bitcast(rolled[0:SUBLANES, LANES:2 * LANES], jnp.int32) & zero_ref[...]
        return pltpu.bitcast(bits, F32)[0:1, :].astype(BF16)

    def conv(c, r_src, c_next, r_next):
        for s1 in range(nb):
            zs_ref[s1 * bsz:(s1 + 1) * bsz, :] = z_ref[0, s1, pl.ds(c, bsz, stride=cb), :]
        acc = [None] * nb
        held = None
        for pi in range(nb):
            if pi % dots_per_chunk == 0:
                held = build_chunk(c_next, r_next, pi // dots_per_chunk)
            d = -nb + 2 * pi
            off = (d % (2 * nb)) * p
            lo = max(0, -d - 1)
            hi = min(nb, nb - d)
            lhs = zs_ref[lo * bsz:hi * bsz, :].astype(BF16) + held
            out = _dot(lhs, r_src[:, off:off + 2 * p])
            for k in range(2):
                dk = d + k
                for s1 in range(max(0, -dk), min(nb, nb - dk)):
                    blk = out[(s1 - lo) * bsz:(s1 - lo + 1) * bsz, k * p:(k + 1) * p]
                    acc[s1 + dk] = blk if acc[s1 + dk] is None else acc[s1 + dk] + blk
        bias = bias_ref[c]
        for s1 in range(nb):
            rows = slice(s1 * bsz, (s1 + 1) * bsz)
            where = pl.ds(c, bsz, stride=cb)
            o_ref[0, s1, where, :] = gate_ref[0, s1, where, :] * (acc[s1] + zs_ref[rows, :] * bias)

    @pl.when(pl.program_id(0) == 0)
    def _():
        for ci in range(nchunk):
            build_chunk((kc_ref, 0), r_a, ci)

    def pair(k, carry):
        c = 2 * k
        conv(c, r_a, (kc_ref, c + 1), r_b)
        conv(c + 1, r_b, (kc_ref, c + 2), r_a)
        return carry

    lax.fori_loop(0, cb // 2 - 1, pair, 0)
    conv(cb - 2, r_a, (kc_ref, cb - 1), r_b)
    conv(cb - 1, r_b, (kcn_ref, 0), r_a)


def _longconv(z_cm, z_g0, gate_cm, gate_g0, kc_ext, bias_cm, order, groups, nb, bsz):
    cb = SUBLANES
    rows, p = z_cm.shape[2:]
    blk = (1, nb, rows, p)
    kern = functools.partial(_longconv_kernel, nb=nb, bsz=bsz, cb=cb)
    return pl.pallas_call(
        kern,
        out_shape=jax.ShapeDtypeStruct((groups, nb, rows, p), F32),
        grid=(groups,),
        in_specs=[pl.BlockSpec((1, LANES), lambda i: (0, 0)),
                  pl.BlockSpec(blk, lambda i: (z_g0 + i, 0, 0, 0)),
                  pl.BlockSpec(blk, lambda i: (gate_g0 + i, 0, 0, 0)),
                  pl.BlockSpec((cb, kc_ext.shape[1]), lambda i: (order * groups + i, 0)),
                  pl.BlockSpec((cb, kc_ext.shape[1]), lambda i: (order * groups + jnp.minimum(i + 1, groups - 1), 0)),
                  pl.BlockSpec((cb, 1, p), lambda i: (order * groups + i, 0, 0))],
        out_specs=pl.BlockSpec(blk, lambda i: (i, 0, 0, 0)),
        scratch_shapes=[pltpu.VMEM((p, 2 * nb * p), BF16), pltpu.VMEM((p, 2 * nb * p), BF16),
                        pltpu.VMEM((nb * bsz, p), F32)],
        compiler_params=_params(("arbitrary",)),
        name="hyena_longconv",
    )(jnp.zeros((1, LANES), jnp.int32), z_cm, gate_cm, kc_ext, kc_ext, bias_cm)


def _outproj_kernel(x_ref, g1_ref, a_ref, b_ref, w_ref, o_ref):
    da = a_ref.shape[1]
    ya = _dot(a_ref[...].astype(BF16), w_ref[:da, :])
    groups, nblk, cg, p = b_ref.shape
    parts = []
    for blk in range(nblk):
        cm = b_ref[:, blk, :, :].reshape(groups * cg, p)
        parts.append(_dot(cm.T.astype(BF16), w_ref[da:, :]))
    y = ya + jnp.concatenate(parts, axis=0)
    o_ref[...] = x_ref[...] + g1_ref[0] * y


def _outproj(x2, g1, oa, ob, w, seq, tm=512):
    t, d = x2.shape
    tpb = seq // tm
    return pl.pallas_call(
        _outproj_kernel,
        out_shape=jax.ShapeDtypeStruct((t, d), F32),
        grid=(t // tm,),
        in_specs=[pl.BlockSpec((tm, d), lambda i: (i, 0)),
                  pl.BlockSpec((1, 1, d), lambda i: (i // tpb, 0, 0)),
                  pl.BlockSpec((tm, oa.shape[1]), lambda i: (i, 0)),
                  pl.BlockSpec((ob.shape[0], tm // LANES, SUBLANES, LANES), lambda i: (0, i % tpb, i // tpb, 0)),
                  pl.BlockSpec(w.shape, lambda i: (0, 0))],
        out_specs=pl.BlockSpec((tm, d), lambda i: (i, 0)),
        compiler_params=_params(("parallel",)),
        name="outproj0",
    )(x2, g1, oa, ob, w)


def _confin_kernel(x_ref, g_ref, sh_ref, sc_ref, w_ref, b_ref, o_ref):
    h = _norm_mod(x_ref[...], g_ref[...], sh_ref[0], sc_ref[0]).astype(BF16)
    n = o_ref.shape[1]
    a = _dot(h, w_ref[:, :n]) + b_ref[:, :n]
    gt = _dot(h, w_ref[:, n:]) + b_ref[:, n:]
    o_ref[...] = a * jax.nn.sigmoid(gt)


def _confin(x2, g, sh, sc, w1, b1, seq, tm=512):
    t, d = x2.shape
    n = w1.shape[1] // 2
    tpb = seq // tm
    return pl.pallas_call(
        _confin_kernel,
        out_shape=jax.ShapeDtypeStruct((t, n), F32),
        grid=(t // tm,),
        in_specs=[pl.BlockSpec((tm, d), lambda i: (i, 0)),
                  pl.BlockSpec((1, d), lambda i: (0, 0)),
                  pl.BlockSpec((1, 1, d), lambda i: (i // tpb, 0, 0)),
                  pl.BlockSpec((1, 1, d), lambda i: (i // tpb, 0, 0)),
                  pl.BlockSpec(w1.shape, lambda i: (0, 0)),
                  pl.BlockSpec((1, 2 * n), lambda i: (0, 0))],
        out_specs=pl.BlockSpec((tm, n), lambda i: (i, 0)),
        compiler_params=_params(("parallel",)),
        name="conformer_in",
    )(x2, g, sh, sc, w1, b1.reshape(1, 2 * n))


def _confout_kernel(x_ref, g1_ref, a_ref, lg_ref, lb_ref, w_ref, b_ref, o_ref):
    a = a_ref[...]
    mu = jnp.mean(a, axis=-1, keepdims=True)
    ac = a - mu
    var = jnp.mean(ac * ac, axis=-1, keepdims=True)
    y = ac * lax.rsqrt(var + EPS) * lg_ref[...] + lb_ref[...]
    y = y * jax.nn.sigmoid(y)
    o_ref[...] = x_ref[...] + g1_ref[0] * (_dot(y.astype(BF16), w_ref[...]) + b_ref[...])


def _confout(x2, g1, a2, ln_g, ln_b, w2, b2, seq, tm=512):
    t, d = x2.shape
    n = a2.shape[1]
    tpb = seq // tm
    return pl.pallas_call(
        _confout_kernel,
        out_shape=jax.ShapeDtypeStruct((t, d), F32),
        grid=(t // tm,),
        in_specs=[pl.BlockSpec((tm, d), lambda i: (i, 0)),
                  pl.BlockSpec((1, 1, d), lambda i: (i // tpb, 0, 0)),
                  pl.BlockSpec((tm, n), lambda i: (i, 0)),
                  pl.BlockSpec((1, n), lambda i: (0, 0)),
                  pl.BlockSpec((1, n), lambda i: (0, 0)),
                  pl.BlockSpec(w2.shape, lambda i: (0, 0)),
                  pl.BlockSpec((1, d), lambda i: (0, 0))],
        out_specs=pl.BlockSpec((tm, d), lambda i: (i, 0)),
        compiler_params=_params(("parallel",)),
        name="conformer_out",
    )(x2, g1, a2, ln_g.reshape(1, n), ln_b.reshape(1, n), w2, b2.reshape(1, d))


def _router_kernel(x_ref, g_ref, sh_ref, sc_ref, wr_ref, br_ref, tri_ref, h_ref, route_ref, cnt_ref, run_ref):
    i = pl.program_id(0)

    @pl.when(i == 0)
    def _():
        run_ref[...] = jnp.zeros_like(run_ref)

    h = _norm_mod(x_ref[...], g_ref[...], sh_ref[0], sc_ref[0])
    hh, hl = _split(h)
    h_ref[...] = hh
    both = _dot(hh, jnp.concatenate([wr_ref[0], wr_ref[1]], axis=1))
    logits = both[:, :LANES] + _dot(hl, wr_ref[0]) + both[:, LANES:] + br_ref[...]
    tm = logits.shape[0]
    lane = lax.broadcasted_iota(jnp.int32, (tm, LANES), 1)
    ninf = jnp.float32(-jnp.inf)

    def first_argmax(v, m):
        return jnp.min(jnp.where(v == m, lane, LANES), axis=-1, keepdims=True)

    gl = jnp.where(lane < N_GROUPS, logits, ninf)
    gmax = jnp.max(gl, axis=-1, keepdims=True)
    g_w = 1.0 / jnp.sum(jnp.exp(gl - gmax), axis=-1, keepdims=True)
    g_idx = first_argmax(gl, gmax)
    e_lo = ROUTE_LANE0 + EXPERTS_PER_GROUP * g_idx
    el = jnp.where((lane >= e_lo) & (lane < e_lo + EXPERTS_PER_GROUP), logits, ninf)
    m1 = jnp.max(el, axis=-1, keepdims=True)
    esum = jnp.sum(jnp.exp(el - m1), axis=-1, keepdims=True)
    i1 = first_argmax(el, m1)
    el2 = jnp.where(lane == i1, ninf, el)
    m2 = jnp.max(el2, axis=-1, keepdims=True)
    i2 = first_argmax(el2, m2)
    p1 = 1.0 / esum
    p2 = jnp.exp(m2 - m1) / esum
    w1 = g_w * (p1 / (p1 + p2))
    w2 = g_w * (p2 / (p1 + p2))

    oh = jnp.where((lane == i1) | (lane == i2), 1.0, 0.0)
    before = _dot(tri_ref[...], oh.astype(BF16)) + run_ref[...]
    rank1 = jnp.sum(jnp.where(lane == i1, before, 0.0), axis=-1, keepdims=True)
    rank2 = jnp.sum(jnp.where(lane == i2, before, 0.0), axis=-1, keepdims=True)
    run_ref[...] = run_ref[...] + jnp.sum(oh, axis=0, keepdims=True)
    cnt_ref[...] = run_ref[...]

    e1 = (i1 - ROUTE_LANE0).astype(F32)
    e2 = (i2 - ROUTE_LANE0).astype(F32)
    vals = (e1, e2, rank1, rank2, w1, w2)
    out = jnp.zeros((tm, LANES), F32)
    for k, v in enumerate(vals):
        out = jnp.where(lane == k, v, out)
    route_ref[...] = out


def _router(x2, g, sh, sc, wr, br, seq, tm=512):
    t, d = x2.shape
    tpb = seq // tm
    return pl.pallas_call(
        _router_kernel,
        out_shape=(jax.ShapeDtypeStruct((t, d), BF16), jax.ShapeDtypeStruct((t, LANES), F32),
                   jax.ShapeDtypeStruct((1, LANES), F32)),
        grid=(t // tm,),
        in_specs=[pl.BlockSpec((tm, d), lambda i: (i, 0)),
                  pl.BlockSpec((1, d), lambda i: (0, 0)),
                  pl.BlockSpec((1, 1, d), lambda i: (i // tpb, 0, 0)),
                  pl.BlockSpec((1, 1, d), lambda i: (i // tpb, 0, 0)),
                  pl.BlockSpec(wr.shape, lambda i: (0, 0, 0)),
                  pl.BlockSpec((1, LANES), lambda i: (0, 0)),
                  pl.BlockSpec((tm, tm), lambda i: (0, 0))],
        out_specs=(pl.BlockSpec((tm, d), lambda i: (i, 0)), pl.BlockSpec((tm, LANES), lambda i: (i, 0)),
                   pl.BlockSpec((1, LANES), lambda i: (0, 0))),
        scratch_shapes=[pltpu.VMEM((1, LANES), F32)],
        compiler_params=_params(("arbitrary",)),
        name="moe_router",
    )(x2, g, sh, sc, wr, br, jnp.tril(jnp.ones((tm, tm), BF16), -1))


def _expert_kernel(te_ref, nv_ref, x_ref, wg_ref, wu_ref, wd_ref, o_ref, wg_s, wu_s, wd_s):
    i = pl.program_id(0)

    @pl.when((i == 0) | (te_ref[i] != te_ref[jnp.maximum(i - 1, 0)]))
    def _():
        wg_s[...] = wg_ref[0].astype(BF16)
        wu_s[...] = wu_ref[0].astype(BF16)
        wd_s[...] = wd_ref[0].astype(BF16)

    @pl.when(i < nv_ref[0])
    def _():
        x = x_ref[...]
        a = _dot(x, wg_s[...])
        u = _dot(x, wu_s[...])
        he = (a * jax.nn.sigmoid(a)) * u
        o_ref[...] = _dot(he.astype(BF16), wd_s[...]).astype(o_ref.dtype)


def _experts(xs, tile_expert, n_valid, wg, wu, wd, layer):
    r, d = xs.shape
    de = wg.shape[2]
    nt = r // MOE_TILE
    row = lambda i, te, nv: (jnp.minimum(i, nv[0] - 1), 0)
    wsel = lambda i, te, nv: (layer * N_EXPERTS + te[i], 0, 0)
    return pl.pallas_call(
        _expert_kernel,
        out_shape=jax.ShapeDtypeStruct((r, d), BF16),
        grid_spec=pltpu.PrefetchScalarGridSpec(
            num_scalar_prefetch=2,
            grid=(nt,),
            in_specs=[pl.BlockSpec((MOE_TILE, d), row),
                      pl.BlockSpec((1, d, de), wsel),
                      pl.BlockSpec((1, d, de), wsel),
                      pl.BlockSpec((1, de, d), wsel)],
            out_specs=pl.BlockSpec((MOE_TILE, d), row),
            scratch_shapes=[pltpu.VMEM((d, de), BF16), pltpu.VMEM((d, de), BF16), pltpu.VMEM((de, d), BF16)]),
        compiler_params=_params(("arbitrary",)),
        name="moe_experts",
    )(tile_expert, n_valid, xs, wg, wu, wd)


def _combine_kernel(x_ref, g2_ref, route_ref, y1_ref, y2_ref, fg_ref, o_ref, *, final):
    r = route_ref[...]
    w1 = r[:, 4:5]
    w2 = r[:, 5:6]
    x = x_ref[...] + g2_ref[0] * (w1 * y1_ref[...].astype(F32) + w2 * y2_ref[...].astype(F32))
    if final:
        x = (x * lax.rsqrt(jnp.mean(x * x, axis=-1, keepdims=True) + EPS)) * fg_ref[...]
    o_ref[...] = x


def _combine(x2, g2, route, y12, final_g, seq, final, tm=512):
    t, d = x2.shape
    tpb = seq // tm
    nt = t // tm
    kern = functools.partial(_combine_kernel, final=final)
    return pl.pallas_call(
        kern,
        out_shape=jax.ShapeDtypeStruct((t, d), F32),
        grid=(t // tm,),
        in_specs=[pl.BlockSpec((tm, d), lambda i: (i, 0)),
                  pl.BlockSpec((1, 1, d), lambda i: (i // tpb, 0, 0)),
                  pl.BlockSpec((tm, LANES), lambda i: (i, 0)),
                  pl.BlockSpec((tm, d), lambda i: (i, 0)),
                  pl.BlockSpec((tm, d), lambda i: (nt + i, 0)),
                  pl.BlockSpec((1, d), lambda i: (0, 0))],
        out_specs=pl.BlockSpec((tm, d), lambda i: (i, 0)),
        compiler_params=_params(("parallel",)),
        name="moe_combine",
    )(x2, g2, route, y12, y12, final_g)


def _moe(x2, g, sh, sc, g2, wg_r, bg_r, we_r, be_r, w_gate, w_up, w_down, layer, final_g, seq, final):
    t, d = x2.shape
    wr = jnp.zeros((d, LANES), F32).at[:, :N_GROUPS].set(wg_r).at[:, ROUTE_LANE0:ROUTE_LANE0 + N_EXPERTS].set(we_r)
    wr_hi = wr.astype(BF16)
    wr_lo = (wr - wr_hi.astype(F32)).astype(BF16)
    br = jnp.zeros((1, LANES), F32).at[0, :N_GROUPS].set(bg_r).at[0, ROUTE_LANE0:ROUTE_LANE0 + N_EXPERTS].set(be_r)
    h, route, cnt = _router(x2, g, sh, sc, jnp.stack([wr_hi, wr_lo]), br, seq)

    counts = cnt[0, ROUTE_LANE0:ROUTE_LANE0 + N_EXPERTS].astype(jnp.int32)
    tiles = (counts + MOE_TILE - 1) // MOE_TILE
    tile_end = jnp.cumsum(tiles)
    offs = (tile_end - tiles) * MOE_TILE
    nt = (2 * t) // MOE_TILE + N_EXPERTS
    tile_id = jnp.minimum(jnp.arange(nt, dtype=jnp.int32), tile_end[-1] - 1)
    tile_expert = jnp.sum((tile_id[:, None] >= tile_end[None, :]).astype(jnp.int32), axis=1)
    n_valid = tile_end[-1:].astype(jnp.int32)
    e12 = route[:, 0:2].astype(jnp.int32)
    offs12 = jnp.sum(jnp.where(e12[:, :, None] == jnp.arange(N_EXPERTS)[None, None, :], offs[None, None, :], 0), axis=-1)
    pos = (offs12 + route[:, 2:4].astype(jnp.int32)).T.reshape(-1)
    tok = jnp.tile(jnp.arange(t, dtype=jnp.int32), 2)
    _, tok_by_row = lax.sort_key_val(pos, tok)
    row = jnp.arange(nt * MOE_TILE, dtype=jnp.int32)
    per_row = lambda v: jnp.repeat(v[tile_expert], MOE_TILE)
    in_e = row - per_row(offs)
    compact = per_row(jnp.cumsum(counts) - counts) + in_e
    sorted_tok = jnp.where(in_e < per_row(counts),
                           tok_by_row.at[jnp.minimum(compact, 2 * t - 1)].get(mode="promise_in_bounds"), row % t)
    xs = h.at[sorted_tok].get(mode="promise_in_bounds")
    ys = _experts(xs, tile_expert, n_valid, w_gate, w_up, w_down, layer)
    y12 = ys.at[pos].get(mode="promise_in_bounds")
    return _combine(x2, g2, route, y12, final_g, seq, final)


def _rope_tables(seq, width):
    hd = DA_HEAD_DIM
    half = hd // 2
    quarter = half // 2
    pos = jnp.arange(seq)
    row = (pos // GRID_W).astype(F32)
    col = (pos % GRID_W).astype(F32)
    inv = ROPE_THETA ** (-jnp.arange(0, half, 2, dtype=F32) / half)
    i = jnp.arange(hd)
    p = jnp.where((i < half)[None, :], row[:, None], col[:, None])
    ang = p * inv[i % quarter][None, :]
    sign = jnp.where((i % half) < quarter, -1.0, 1.0)[None, :]
    reps = width // hd
    return jnp.tile(jnp.cos(ang), (1, reps)), jnp.tile(jnp.sin(ang) * sign, (1, reps))


def kernel(x, c, ctx, c_ctx, ada_w, ada_b, norm1_g, norm2_g, final_g, w_in0, w_out0, lam_q1, lam_k1, lam_q2, lam_k2, subln_g, hy_short_w, hy_short_b, hy_w1, hy_b1, hy_fr1, hy_w2, hy_b2, hy_fr2, hy_w3, hy_b3, hy_bias, cv_w1, cv_b1, cv_dw_w, cv_dw_b, cv_ln_g, cv_ln_b, cv_w2, cv_b2, moe_wg, moe_bg, moe_we, moe_be, moe_w_gate, moe_w_up, moe_w_down):
    bsz, seq, d = x.shape
    lctx = ctx.shape[1]
    depth = ada_w.shape[0]
    t = bsz * seq
    hyw = d - DA_WIDTH
    x2 = x.reshape(t, d)

    rows = ((bsz + 1 + SUBLANES - 1) // SUBLANES) * SUBLANES
    cs = jnp.zeros((rows, d), F32).at[:bsz].set(c).at[bsz].set(c_ctx)
    mods = _ada(cs, ada_w, ada_b)

    def tok_mod(i, k):
        return mods[i, :bsz, k * d:(k + 1) * d].reshape(bsz, 1, d)

    def ctx_mod(i, k):
        return mods[i, bsz:bsz + 1, k * d:(k + 1) * d].reshape(1, 1, d)

    for i in range(depth):
        j = i // 2
        g_n1 = norm1_g[i].reshape(1, d)
        if i % 2 == 0:
            assert not any(m % 2 == 0 for m in range(i + 1, depth)), "context-stream update is not implemented"
            lam_init = 0.8 - 0.6 * math.exp(-0.3 * i)
            w_in = w_in0[j]
            cw = 2 * LANES
            cos, sin = _rope_tables(seq, cw)
            q, k, v, u = _inproj(x2, g_n1, tok_mod(i, 0), tok_mod(i, 1), cos, sin, w_in.astype(BF16), seq)
            kc, vc = _ctxproj(ctx.reshape(bsz * lctx, d), g_n1, ctx_mod(i, 0), ctx_mod(i, 1),
                              w_in[:, DA_WIDTH:3 * DA_WIDTH].astype(BF16), lctx)
            lam_p = jnp.stack([lam_q1[j], lam_k1[j], lam_q2[j], lam_k2[j]])
            k_all = jnp.concatenate([kc.reshape(bsz, lctx, DA_WIDTH), k.reshape(bsz, seq, DA_WIDTH)], axis=1)
            v_all = jnp.concatenate([vc.reshape(bsz, lctx, DA_WIDTH), v.reshape(bsz, seq, DA_WIDTH)], axis=1)
            o_a = _attention(q, jnp.swapaxes(k_all, 1, 2), v_all, lam_p, subln_g[j].reshape(1, -1), seq, lam_init)

            nb = seq // LC_P
            groups = hyw // SUBLANES
            ucm = _dwconv(u.reshape(bsz, seq, 3 * hyw), hy_short_w[j], hy_short_b[j], channel_major=True)
            kc_ext = _hyena_filters(seq, hy_w1[j], hy_b1[j], hy_fr1[j], hy_w2[j], hy_b2[j], hy_fr2[j],
                                    hy_w3[j], hy_b3[j])
            bias_cm = jnp.broadcast_to(hy_bias[j].reshape(HY_ORDER * hyw, 1, 1), (HY_ORDER * hyw, 1, LC_P))
            z1 = _longconv(ucm, 0, ucm, groups, kc_ext, bias_cm, 0, groups, nb, bsz)
            z2 = _longconv(z1, 0, ucm, 2 * groups, kc_ext, bias_cm, 1, groups, nb, bsz)
            x2 = _outproj(x2, tok_mod(i, 2), o_a, z2, w_out0[j].astype(BF16), seq)
        else:
            a = _confin(x2, g_n1, tok_mod(i, 0), tok_mod(i, 1), cv_w1[j].astype(BF16), cv_b1[j], seq)
            a = _dwconv(a.reshape(bsz, seq, -1), cv_dw_w[j], cv_dw_b[j]).reshape(t, -1)
            x2 = _confout(x2, tok_mod(i, 2), a, cv_ln_g[j], cv_ln_b[j], cv_w2[j].astype(BF16), cv_b2[j], seq)
        x2 = _moe(x2, norm2_g[i].reshape(1, d), tok_mod(i, 3), tok_mod(i, 4), tok_mod(i, 5),
                  moe_wg[i], moe_bg[i], moe_we[i], moe_be[i],
                  moe_w_gate.reshape((-1,) + moe_w_gate.shape[2:]), moe_w_up.reshape((-1,) + moe_w_up.shape[2:]),
                  moe_w_down.reshape((-1,) + moe_w_down.shape[2:]), i,
                  final_g.reshape(1, d), seq, final=(i == depth - 1))
    return x2.reshape(bsz, seq, d)
```

```python
import functools
import math

import jax
import jax.numpy as jnp
from jax import lax
from jax.experimental import pallas as pl
from jax.experimental.pallas import tpu as pltpu

F32 = jnp.float32
BF16 = jnp.bfloat16

GRID_W = 64
DA_HEADS = 4
DA_HEAD_DIM = 64
DA_WIDTH = DA_HEADS * 2 * DA_HEAD_DIM
HY_ORDER = 2
HY_TARGET = 1e-2
HY_MIN_DECAY = math.log(HY_TARGET) / 0.3
HY_MAX_DECAY = math.log(HY_TARGET) / 1.5
N_GROUPS = 4
EXPERTS_PER_GROUP = 8
N_EXPERTS = N_GROUPS * EXPERTS_PER_GROUP
ROPE_THETA = 10000.0
EPS = 1e-6

LANES = 128
SUBLANES = 8
VMEM_LIMIT = 52 * 1024 * 1024
ROUTE_LANE0 = N_GROUPS
MOE_TILE = 256
LC_P = 128
LC_W = 4096


def _params(sem, flags=None):
    return pltpu.CompilerParams(dimension_semantics=sem, vmem_limit_bytes=VMEM_LIMIT, flags=flags)


def _split(a):
    hi = a.astype(BF16)
    lo = (a - hi.astype(F32)).astype(BF16)
    return hi, lo


def _dot(a, b):
    return jnp.dot(a, b, preferred_element_type=F32)


def _dot3(a, b):
    ah, al = _split(a)
    bh, bl = _split(b)
    return _dot(ah, bh) + _dot(al, bh) + _dot(ah, bl)


def _norm_mod(x, g, sh, sc):
    y = x * lax.rsqrt(jnp.mean(x * x, axis=-1, keepdims=True) + EPS)
    return (y * g) * (1.0 + sc) + sh


def _ada_kernel(c_ref, w_ref, b_ref, o_ref):
    c = c_ref[...]
    s = c * jax.nn.sigmoid(c)
    o_ref[0] = _dot3(s, w_ref[0]) + b_ref[0]


def _ada(cs, ada_w, ada_b):
    depth, d, n6 = ada_w.shape
    rows = cs.shape[0]
    tn = 1536
    return pl.pallas_call(
        _ada_kernel,
        out_shape=jax.ShapeDtypeStruct((depth, rows, n6), F32),
        grid=(depth, n6 // tn),
        in_specs=[pl.BlockSpec((rows, d), lambda l, j: (0, 0)),
                  pl.BlockSpec((1, d, tn), lambda l, j: (l, 0, j)),
                  pl.BlockSpec((1, 1, tn), lambda l, j: (l, 0, j))],
        out_specs=pl.BlockSpec((1, rows, tn), lambda l, j: (l, 0, j)),
        compiler_params=_params(("parallel", "parallel")),
        name="adaln",
    )(cs, ada_w, ada_b.reshape(depth, 1, n6))


def _inproj_kernel(x_ref, g_ref, sh_ref, sc_ref, cos_ref, sin_ref, w_ref,
                   q_ref, k_ref, v_ref, u_ref, *, d_attn, d_hy):
    h = _norm_mod(x_ref[...], g_ref[...], sh_ref[0], sc_ref[0]).astype(BF16)
    cos = cos_ref[...]
    sin = sin_ref[...]
    cw = cos.shape[1]
    quarter = DA_HEAD_DIM // 4
    lane = lax.broadcasted_iota(jnp.int32, cos.shape, 1)
    first = (lane % (2 * quarter)) < quarter

    def mm(c0, c1):
        return _dot(h, w_ref[:, c0:c1])

    def rope(a):
        partner = jnp.where(first, pltpu.roll(a, cw - quarter, 1), pltpu.roll(a, quarter, 1))
        return a * cos + partner * sin

    for j in range(d_attn // cw):
        q_ref[:, j * cw:(j + 1) * cw] = (rope(mm(j * cw, (j + 1) * cw)) * (DA_HEAD_DIM ** -0.5)).astype(BF16)
        k_ref[:, j * cw:(j + 1) * cw] = rope(mm(d_attn + j * cw, d_attn + (j + 1) * cw)).astype(BF16)
    v_ref[...] = mm(2 * d_attn, 3 * d_attn).astype(BF16)
    for j in range(d_hy // 512):
        u_ref[:, j * 512:(j + 1) * 512] = mm(3 * d_attn + j * 512, 3 * d_attn + (j + 1) * 512)


def _inproj(x2, g, sh, sc, cos, sin, w_ext, seq, tm=512):
    t, d = x2.shape
    tpb = seq // tm
    d_attn = DA_WIDTH
    d_hy = w_ext.shape[1] - 3 * d_attn
    cw = cos.shape[1]
    kern = functools.partial(_inproj_kernel, d_attn=d_attn, d_hy=d_hy)
    return pl.pallas_call(
        kern,
        out_shape=(jax.ShapeDtypeStruct((t, d_attn), BF16), jax.ShapeDtypeStruct((t, d_attn), BF16),
                   jax.ShapeDtypeStruct((t, d_attn), BF16), jax.ShapeDtypeStruct((t, d_hy), F32)),
        grid=(t // tm,),
        in_specs=[pl.BlockSpec((tm, d), lambda i: (i, 0)),
                  pl.BlockSpec((1, d), lambda i: (0, 0)),
                  pl.BlockSpec((1, 1, d), lambda i: (i // tpb, 0, 0)),
                  pl.BlockSpec((1, 1, d), lambda i: (i // tpb, 0, 0)),
                  pl.BlockSpec((tm, cw), lambda i: (i % tpb, 0)),
                  pl.BlockSpec((tm, cw), lambda i: (i % tpb, 0)),
                  pl.BlockSpec(w_ext.shape, lambda i: (0, 0))],
        out_specs=(pl.BlockSpec((tm, d_attn), lambda i: (i, 0)), pl.BlockSpec((tm, d_attn), lambda i: (i, 0)),
                   pl.BlockSpec((tm, d_attn), lambda i: (i, 0)), pl.BlockSpec((tm, d_hy), lambda i: (i, 0))),
        compiler_params=_params(("parallel",)),
        name="inproj0",
    )(x2, g, sh, sc, cos, sin, w_ext)


def _ctxproj_kernel(x_ref, g_ref, sh_ref, sc_ref, w_ref, k_ref, v_ref):
    h = _norm_mod(x_ref[...], g_ref[...], sh_ref[0], sc_ref[0]).astype(BF16)
    n = k_ref.shape[1]
    k_ref[...] = _dot(h, w_ref[:, :n]).astype(BF16)
    v_ref[...] = _dot(h, w_ref[:, n:]).astype(BF16)


def _ctxproj(c2, g, sh, sc, w_kv, tm):
    t, d = c2.shape
    n = w_kv.shape[1] // 2
    return pl.pallas_call(
        _ctxproj_kernel,
        out_shape=(jax.ShapeDtypeStruct((t, n), BF16), jax.ShapeDtypeStruct((t, n), BF16)),
        grid=(t // tm,),
        in_specs=[pl.BlockSpec((tm, d), lambda i: (i, 0)),
                  pl.BlockSpec((1, d), lambda i: (0, 0)),
                  pl.BlockSpec((1, 1, d), lambda i: (0, 0, 0)),
                  pl.BlockSpec((1, 1, d), lambda i: (0, 0, 0)),
                  pl.BlockSpec(w_kv.shape, lambda i: (0, 0))],
        out_specs=(pl.BlockSpec((tm, n), lambda i: (i, 0)), pl.BlockSpec((tm, n), lambda i: (i, 0))),
        compiler_params=_params(("parallel",)),
        name="ctxproj",
    )(c2, g, sh, sc, w_kv)


def _attn_kernel(lam_ref, g_ref, q_ref, kt_ref, v_ref, o_ref, *, lam_init):
    lp = lam_ref[...]
    lam = (jnp.exp(jnp.sum(lp[0:1] * lp[1:2], axis=-1, keepdims=True))
           - jnp.exp(jnp.sum(lp[2:3] * lp[3:4], axis=-1, keepdims=True)) + lam_init)
    hp = 2 * DA_HEAD_DIM
    scores = []
    for p in range(q_ref.shape[1] // hp):
        cols = slice(p * hp, (p + 1) * hp)
        q = q_ref[:, cols]
        lane = lax.broadcasted_iota(jnp.int32, q.shape, 1)
        zero = jnp.zeros_like(q)
        kt = kt_ref[0, cols, :]
        scores.append((_dot(jnp.where(lane < DA_HEAD_DIM, q, zero), kt),
                       _dot(jnp.where(lane >= DA_HEAD_DIM, q, zero), kt)))
    for p, (s1, s2) in enumerate(scores):
        cols = slice(p * hp, (p + 1) * hp)
        e1 = jnp.exp(s1 - jnp.max(s1, axis=-1, keepdims=True))
        e2 = jnp.exp(s2 - jnp.max(s2, axis=-1, keepdims=True))
        r1 = 1.0 / jnp.sum(e1, axis=-1, keepdims=True)
        r2 = lam / jnp.sum(e2, axis=-1, keepdims=True)
        a = (e1 * r1 - e2 * r2).astype(BF16)
        o = _dot(a, v_ref[0, :, cols])
        y = o * lax.rsqrt(jnp.mean(o * o, axis=-1, keepdims=True) + EPS)
        o_ref[:, cols] = (y * g_ref[...]) * (1.0 - lam_init)


def _attention(q, kt_all, v_all, lam_p, subln_g, seq, lam_init, tq=512, pairs=2):
    t = q.shape[0]
    b, _, lk = kt_all.shape
    hw = 2 * DA_HEAD_DIM * pairs
    nq = seq // tq
    kern = functools.partial(_attn_kernel, lam_init=lam_init)
    return pl.pallas_call(
        kern,
        out_shape=jax.ShapeDtypeStruct((t, DA_WIDTH), F32),
        grid=(b, DA_HEADS // pairs, nq),
        in_specs=[pl.BlockSpec(lam_p.shape, lambda bi, h, i: (0, 0)),
                  pl.BlockSpec((1, 2 * DA_HEAD_DIM), lambda bi, h, i: (0, 0)),
                  pl.BlockSpec((tq, hw), lambda bi, h, i: (bi * nq + i, h)),
                  pl.BlockSpec((1, hw, lk), lambda bi, h, i: (bi, h, 0)),
                  pl.BlockSpec((1, lk, hw), lambda bi, h, i: (bi, 0, h))],
        out_specs=pl.BlockSpec((tq, hw), lambda bi, h, i: (bi * nq + i, h)),
        compiler_params=_params(("parallel", "parallel", "parallel")),
        name="diff_attn",
    )(lam_p, subln_g, q, kt_all, v_all)


def _dwconv_kernel(x_ref, w_ref, b_ref, o_ref, pad_ref, *, kw, halo, channel_major):
    s = x_ref.shape[1]
    pl_ = (kw - 1) // 2
    zeros = jnp.zeros((halo, x_ref.shape[2]), F32)
    pad_ref[0:halo, :] = zeros
    pad_ref[halo + s:halo + s + halo, :] = zeros
    pad_ref[halo:halo + s, :] = x_ref[0]
    w = w_ref[...]
    acc = jnp.zeros((s, x_ref.shape[2]), F32) + b_ref[...]
    for j in range(kw):
        off = halo - pl_ + j
        acc = acc + pad_ref[off:off + s, :] * w[j:j + 1, :]
    if channel_major:
        acc_t = acc.T
        for g in range(LANES // SUBLANES):
            for blk in range(s // LANES):
                o_ref[g, blk] = acc_t[g * SUBLANES:(g + 1) * SUBLANES, blk * LANES:(blk + 1) * LANES]
    else:
        o_ref[0] = acc


def _dwconv(x3, w, bias, channel_major=False):
    b, s, c = x3.shape
    kw = w.shape[0]
    halo = 2 * SUBLANES
    kern = functools.partial(_dwconv_kernel, kw=kw, halo=halo, channel_major=channel_major)
    if channel_major:
        gpb = LANES // SUBLANES
        out_shape = jax.ShapeDtypeStruct((c // SUBLANES, s // LANES, b * SUBLANES, LANES), F32)
        out_spec = pl.BlockSpec((gpb, s // LANES, SUBLANES, LANES), lambda bi, ci: (ci, 0, bi, 0))
    else:
        out_shape = jax.ShapeDtypeStruct((b, s, c), F32)
        out_spec = pl.BlockSpec((1, s, LANES), lambda bi, ci: (bi, 0, ci))
    return pl.pallas_call(
        kern,
        out_shape=out_shape,
        grid=(b, c // LANES),
        in_specs=[pl.BlockSpec((1, s, LANES), lambda bi, ci: (bi, 0, ci)),
                  pl.BlockSpec((kw, LANES), lambda bi, ci: (0, ci)),
                  pl.BlockSpec((1, LANES), lambda bi, ci: (0, ci))],
        out_specs=out_spec,
        scratch_shapes=[pltpu.VMEM((s + 2 * halo, LANES), F32)],
        compiler_params=_params(("parallel", "parallel")),
        name="dwconv",
    )(x3, w, bias.reshape(1, c))


def _filter_kernel(zt_ref, t_ref, w1_ref, b1_ref, f1_ref, w2_ref, b2_ref, f2_ref, w3_ref, b3_ref, dl_ref,
                   o_ref, h_ref, *, n):
    @pl.when(pl.program_id(0) == 0)
    def _():
        h1 = jnp.sin(f1_ref[...] * (_dot3(w1_ref[...], zt_ref[...]) + b1_ref[...]))
        h_ref[...] = jnp.sin(f2_ref[...] * (_dot3(w2_ref[...], h1) + b2_ref[...]))

    h2 = h_ref[...]
    rows = o_ref.shape[0]
    fwd = _dot3(w3_ref[0], h2[:, :n]) + b3_ref[0]
    bwd = _dot3(w3_ref[1], h2[:, n:]) + b3_ref[1]
    decay = jnp.exp(-t_ref[...] * dl_ref[...])
    lane = lax.broadcasted_iota(jnp.int32, (rows, n), 1)
    kf = fwd * decay[:, :n]
    kb = jnp.where(lane == 0, 0.0, bwd * decay[:, n:])
    inv = 1.0 / (jnp.sum(jnp.abs(kf), axis=-1, keepdims=True) + jnp.sum(jnp.abs(kb), axis=-1, keepdims=True))
    kf = kf * inv
    kb = kb * inv
    o_ref[:, 0:LANES] = kb[:, n - LANES:]
    o_ref[:, LANES:LANES + n] = kf
    o_ref[:, LANES + n:] = kb


def _hyena_filters(n, w1, b1, fr1, w2, b2, fr2, w3, b3):
    emb, ffn = w1.shape
    c = w3.shape[1] // (2 * HY_ORDER)
    bands = (emb - 1) // 2
    t = jnp.linspace(0.0, 1.0, n, dtype=F32)[:, None]
    ang = (2.0 * math.pi / n) * jnp.arange(n, dtype=F32)[:, None] * jnp.linspace(1e-4, bands - 1, bands, dtype=F32)[None, :]
    z = jnp.concatenate([t, jnp.cos(ang), -jnp.sin(ang)], axis=-1)
    rev = (n - jnp.arange(n)) % n
    z2 = jnp.concatenate([z, z[rev]], axis=0)
    t2 = jnp.concatenate([t, t[rev]], axis=0).reshape(1, 2 * n)
    emb_p = ((emb + SUBLANES - 1) // SUBLANES) * SUBLANES
    zt = jnp.zeros((emb_p, 2 * n), F32).at[:emb].set(z2.T)
    w1t = jnp.zeros((ffn, emb_p), F32).at[:, :emb].set(w1.T)
    deltas = jnp.abs(jnp.linspace(HY_MIN_DECAY, HY_MAX_DECAY, c, dtype=F32))
    w3t = w3.T.reshape(HY_ORDER, 2, c, ffn).transpose(1, 0, 2, 3).reshape(2, HY_ORDER * c, ffn)
    b3t = b3.reshape(HY_ORDER, 2, c).transpose(1, 0, 2).reshape(2, HY_ORDER * c, 1)
    dl = jnp.tile(deltas, HY_ORDER).reshape(HY_ORDER * c, 1)
    rows = LANES
    kern = functools.partial(_filter_kernel, n=n)
    col = lambda v: v.reshape(ffn, 1)
    return pl.pallas_call(
        kern,
        out_shape=jax.ShapeDtypeStruct((HY_ORDER * c, LANES + 2 * n), F32),
        grid=(HY_ORDER * c // rows,),
        in_specs=[pl.BlockSpec(zt.shape, lambda i: (0, 0)),
                  pl.BlockSpec(t2.shape, lambda i: (0, 0)),
                  pl.BlockSpec(w1t.shape, lambda i: (0, 0)),
                  pl.BlockSpec((ffn, 1), lambda i: (0, 0)),
                  pl.BlockSpec((ffn, 1), lambda i: (0, 0)),
                  pl.BlockSpec((ffn, ffn), lambda i: (0, 0)),
                  pl.BlockSpec((ffn, 1), lambda i: (0, 0)),
                  pl.BlockSpec((ffn, 1), lambda i: (0, 0)),
                  pl.BlockSpec((2, rows, ffn), lambda i: (0, i, 0)),
                  pl.BlockSpec((2, rows, 1), lambda i: (0, i, 0)),
                  pl.BlockSpec((rows, 1), lambda i: (i, 0))],
        out_specs=pl.BlockSpec((rows, LANES + 2 * n), lambda i: (i, 0)),
        scratch_shapes=[pltpu.VMEM((ffn, 2 * n), F32)],
        compiler_params=_params(("arbitrary",)),
        name="hyena_filters",
    )(zt, t2, w1t, col(b1), col(fr1), w2.T, col(b2), col(fr2), w3t, b3t, dl)


def _longconv_kernel(zero_ref, z_ref, gate_ref, kc_ref, kcn_ref, bias_ref, o_ref, r_a, r_b, zs_ref,
                     *, nb, bsz, cb):
    p = LC_P
    n2 = 2 * nb * p
    w = min(LC_W, n2)
    nchunk = n2 // w
    dots_per_chunk = nb // nchunk

    def build_chunk(src, r_dst, ci):
        k_ref, c = src
        win = k_ref[pl.ds(c, 1), ci * w:ci * w + w + LANES]
        rolled = pltpu.roll(jnp.broadcast_to(win, (p, w + LANES)), 0, 1, stride=1, stride_axis=0)
        r_dst[:, ci * w:(ci + 1) * w] = rolled[:, LANES:].astype(BF16)
        bits = pltpu.bitcast(rolled[0:SUBLANES, LANES:2 * LANES], jnp.int32) & zero_ref[...]
        return pltpu.bitcast(bits, F32)[0:1, :].astype(BF16)

    def conv(c, r_src, c_next, r_next):
        for s1 in range(nb):
            zs_ref[s1 * bsz:(s1 + 1) * bsz, :] = z_ref[0, s1, pl.ds(c, bsz, stride=cb), :]
        acc = [None] * nb
        held = None
        for pi in range(nb):
            if pi % dots_per_chunk == 0:
                held = build_chunk(c_next, r_next, pi // dots_per_chunk)
            d = -nb + 2 * pi
            off = (d % (2 * nb)) * p
            lo = max(0, -d - 1)
            hi = min(nb, nb - d)
            lhs = zs_ref[lo * bsz:hi * bsz, :].astype(BF16) + held
            out = _dot(lhs, r_src[:, off:off + 2 * p])
            for k in range(2):
                dk = d + k
                for s1 in range(max(0, -dk), min(nb, nb - dk)):
                    blk = out[(s1 - lo) * bsz:(s1 - lo + 1) * bsz, k * p:(k + 1) * p]
                    acc[s1 + dk] = blk if acc[s1 + dk] is None else acc[s1 + dk] + blk
        bias = bias_ref[c]
        for s1 in range(nb):
            rows = slice(s1 * bsz, (s1 + 1) * bsz)
            where = pl.ds(c, bsz, stride=cb)
            o_ref[0, s1, where, :] = gate_ref[0, s1, where, :] * (acc[s1] + zs_ref[rows, :] * bias)

    @pl.when(pl.program_id(0) == 0)
    def _():
        for ci in range(nchunk):
            build_chunk((kc_ref, 0), r_a, ci)

    def pair(k, carry):
        c = 2 * k
        conv(c, r_a, (kc_ref, c + 1), r_b)
        conv(c + 1, r_b, (kc_ref, c + 2), r_a)
        return carry

    lax.fori_loop(0, cb // 2 - 1, pair, 0)
    conv(cb - 2, r_a, (kc_ref, cb - 1), r_b)
    conv(cb - 1, r_b, (kcn_ref, 0), r_a)


def _longconv(z_cm, z_g0, gate_cm, gate_g0, kc_ext, bias_cm, order, groups, nb, bsz):
    cb = SUBLANES
    rows, p = z_cm.shape[2:]
    blk = (1, nb, rows, p)
    kern = functools.partial(_longconv_kernel, nb=nb, bsz=bsz, cb=cb)
    return pl.pallas_call(
        kern,
        out_shape=jax.ShapeDtypeStruct((groups, nb, rows, p), F32),
        grid=(groups,),
        in_specs=[pl.BlockSpec((1, LANES), lambda i: (0, 0)),
                  pl.BlockSpec(blk, lambda i: (z_g0 + i, 0, 0, 0)),
                  pl.BlockSpec(blk, lambda i: (gate_g0 + i, 0, 0, 0)),
                  pl.BlockSpec((cb, kc_ext.shape[1]), lambda i: (order * groups + i, 0)),
                  pl.BlockSpec((cb, kc_ext.shape[1]), lambda i: (order * groups + jnp.minimum(i + 1, groups - 1), 0)),
                  pl.BlockSpec((cb, 1, p), lambda i: (order * groups + i, 0, 0))],
        out_specs=pl.BlockSpec(blk, lambda i: (i, 0, 0, 0)),
        scratch_shapes=[pltpu.VMEM((p, 2 * nb * p), BF16), pltpu.VMEM((p, 2 * nb * p), BF16),
                        pltpu.VMEM((nb * bsz, p), F32)],
        compiler_params=_params(("arbitrary",)),
        name="hyena_longconv",
    )(jnp.zeros((1, LANES), jnp.int32), z_cm, gate_cm, kc_ext, kc_ext, bias_cm)


def _outproj_kernel(x_ref, g1_ref, a_ref, b_ref, w_ref, o_ref):
    da = a_ref.shape[1]
    ya = _dot(a_ref[...].astype(BF16), w_ref[:da, :])
    groups, nblk, cg, p = b_ref.shape
    parts = []
    for blk in range(nblk):
        cm = b_ref[:, blk, :, :].reshape(groups * cg, p)
        parts.append(_dot(cm.T.astype(BF16), w_ref[da:, :]))
    y = ya + jnp.concatenate(parts, axis=0)
    o_ref[...] = x_ref[...] + g1_ref[0] * y


def _outproj(x2, g1, oa, ob, w, seq, tm=512):
    t, d = x2.shape
    tpb = seq // tm
    return pl.pallas_call(
        _outproj_kernel,
        out_shape=jax.ShapeDtypeStruct((t, d), F32),
        grid=(t // tm,),
        in_specs=[pl.BlockSpec((tm, d), lambda i: (i, 0)),
                  pl.BlockSpec((1, 1, d), lambda i: (i // tpb, 0, 0)),
                  pl.BlockSpec((tm, oa.shape[1]), lambda i: (i, 0)),
                  pl.BlockSpec((ob.shape[0], tm // LANES, SUBLANES, LANES), lambda i: (0, i % tpb, i // tpb, 0)),
                  pl.BlockSpec(w.shape, lambda i: (0, 0))],
        out_specs=pl.BlockSpec((tm, d), lambda i: (i, 0)),
        compiler_params=_params(("parallel",)),
        name="outproj0",
    )(x2, g1, oa, ob, w)


def _confin_kernel(x_ref, g_ref, sh_ref, sc_ref, w_ref, b_ref, o_ref):
    h = _norm_mod(x_ref[...], g_ref[...], sh_ref[0], sc_ref[0]).astype(BF16)
    n = o_ref.shape[1]
    a = _dot(h, w_ref[:, :n]) + b_ref[:, :n]
    gt = _dot(h, w_ref[:, n:]) + b_ref[:, n:]
    o_ref[...] = a * jax.nn.sigmoid(gt)


def _confin(x2, g, sh, sc, w1, b1, seq, tm=512):
    t, d = x2.shape
    n = w1.shape[1] // 2
    tpb = seq // tm
    return pl.pallas_call(
        _confin_kernel,
        out_shape=jax.ShapeDtypeStruct((t, n), F32),
        grid=(t // tm,),
        in_specs=[pl.BlockSpec((tm, d), lambda i: (i, 0)),
                  pl.BlockSpec((1, d), lambda i: (0, 0)),
                  pl.BlockSpec((1, 1, d), lambda i: (i // tpb, 0, 0)),
                  pl.BlockSpec((1, 1, d), lambda i: (i // tpb, 0, 0)),
                  pl.BlockSpec(w1.shape, lambda i: (0, 0)),
                  pl.BlockSpec((1, 2 * n), lambda i: (0, 0))],
        out_specs=pl.BlockSpec((tm, n), lambda i: (i, 0)),
        compiler_params=_params(("parallel",)),
        name="conformer_in",
    )(x2, g, sh, sc, w1, b1.reshape(1, 2 * n))


def _confout_kernel(x_ref, g1_ref, a_ref, lg_ref, lb_ref, w_ref, b_ref, o_ref):
    a = a_ref[...]
    mu = jnp.mean(a, axis=-1, keepdims=True)
    ac = a - mu
    var = jnp.mean(ac * ac, axis=-1, keepdims=True)
    y = ac * lax.rsqrt(var + EPS) * lg_ref[...] + lb_ref[...]
    y = y * jax.nn.sigmoid(y)
    o_ref[...] = x_ref[...] + g1_ref[0] * (_dot(y.astype(BF16), w_ref[...]) + b_ref[...])


def _confout(x2, g1, a2, ln_g, ln_b, w2, b2, seq, tm=512):
    t, d = x2.shape
    n = a2.shape[1]
    tpb = seq // tm
    return pl.pallas_call(
        _confout_kernel,
        out_shape=jax.ShapeDtypeStruct((t, d), F32),
        grid=(t // tm,),
        in_specs=[pl.BlockSpec((tm, d), lambda i: (i, 0)),
                  pl.BlockSpec((1, 1, d), lambda i: (i // tpb, 0, 0)),
                  pl.BlockSpec((tm, n), lambda i: (i, 0)),
                  pl.BlockSpec((1, n), lambda i: (0, 0)),
                  pl.BlockSpec((1, n), lambda i: (0, 0)),
                  pl.BlockSpec(w2.shape, lambda i: (0, 0)),
                  pl.BlockSpec((1, d), lambda i: (0, 0))],
        out_specs=pl.BlockSpec((tm, d), lambda i: (i, 0)),
        compiler_params=_params(("parallel",)),
        name="conformer_out",
    )(x2, g1, a2, ln_g.reshape(1, n), ln_b.reshape(1, n), w2, b2.reshape(1, d))


def _router_kernel(x_ref, g_ref, sh_ref, sc_ref, wr_ref, br_ref, tri_ref, h_ref, route_ref, cnt_ref, run_ref):
    i = pl.program_id(0)

    @pl.when(i == 0)
    def _():
        run_ref[...] = jnp.zeros_like(run_ref)

    h = _norm_mod(x_ref[...], g_ref[...], sh_ref[0], sc_ref[0])
    hh, hl = _split(h)
    h_ref[...] = hh
    both = _dot(hh, jnp.concatenate([wr_ref[0], wr_ref[1]], axis=1))
    logits = both[:, :LANES] + _dot(hl, wr_ref[0]) + both[:, LANES:] + br_ref[...]
    tm = logits.shape[0]
    lane = lax.broadcasted_iota(jnp.int32, (tm, LANES), 1)
    ninf = jnp.float32(-jnp.inf)

    def first_argmax(v, m):
        return jnp.min(jnp.where(v == m, lane, LANES), axis=-1, keepdims=True)

    gl = jnp.where(lane < N_GROUPS, logits, ninf)
    gmax = jnp.max(gl, axis=-1, keepdims=True)
    g_w = 1.0 / jnp.sum(jnp.exp(gl - gmax), axis=-1, keepdims=True)
    g_idx = first_argmax(gl, gmax)
    e_lo = ROUTE_LANE0 + EXPERTS_PER_GROUP * g_idx
    el = jnp.where((lane >= e_lo) & (lane < e_lo + EXPERTS_PER_GROUP), logits, ninf)
    m1 = jnp.max(el, axis=-1, keepdims=True)
    esum = jnp.sum(jnp.exp(el - m1), axis=-1, keepdims=True)
    i1 = first_argmax(el, m1)
    el2 = jnp.where(lane == i1, ninf, el)
    m2 = jnp.max(el2, axis=-1, keepdims=True)
    i2 = first_argmax(el2, m2)
    p1 = 1.0 / esum
    p2 = jnp.exp(m2 - m1) / esum
    w1 = g_w * (p1 / (p1 + p2))
    w2 = g_w * (p2 / (p1 + p2))

    oh = jnp.where((lane == i1) | (lane == i2), 1.0, 0.0)
    before = _dot(tri_ref[...], oh.astype(BF16)) + run_ref[...]
    rank1 = jnp.sum(jnp.where(lane == i1, before, 0.0), axis=-1, keepdims=True)
    rank2 = jnp.sum(jnp.where(lane == i2, before, 0.0), axis=-1, keepdims=True)
    run_ref[...] = run_ref[...] + jnp.sum(oh, axis=0, keepdims=True)
    cnt_ref[...] = run_ref[...]

    e1 = (i1 - ROUTE_LANE0).astype(F32)
    e2 = (i2 - ROUTE_LANE0).astype(F32)
    vals = (e1, e2, rank1, rank2, w1, w2)
    out = jnp.zeros((tm, LANES), F32)
    for k, v in enumerate(vals):
        out = jnp.where(lane == k, v, out)
    route_ref[...] = out


def _router(x2, g, sh, sc, wr, br, seq, tm=512):
    t, d = x2.shape
    tpb = seq // tm
    return pl.pallas_call(
        _router_kernel,
        out_shape=(jax.ShapeDtypeStruct((t, d), BF16), jax.ShapeDtypeStruct((t, LANES), F32),
                   jax.ShapeDtypeStruct((1, LANES), F32)),
        grid=(t // tm,),
        in_specs=[pl.BlockSpec((tm, d), lambda i: (i, 0)),
                  pl.BlockSpec((1, d), lambda i: (0, 0)),
                  pl.BlockSpec((1, 1, d), lambda i: (i // tpb, 0, 0)),
                  pl.BlockSpec((1, 1, d), lambda i: (i // tpb, 0, 0)),
                  pl.BlockSpec(wr.shape, lambda i: (0, 0, 0)),
                  pl.BlockSpec((1, LANES), lambda i: (0, 0)),
                  pl.BlockSpec((tm, tm), lambda i: (0, 0))],
        out_specs=(pl.BlockSpec((tm, d), lambda i: (i, 0)), pl.BlockSpec((tm, LANES), lambda i: (i, 0)),
                   pl.BlockSpec((1, LANES), lambda i: (0, 0))),
        scratch_shapes=[pltpu.VMEM((1, LANES), F32)],
        compiler_params=_params(("arbitrary",)),
        name="moe_router",
    )(x2, g, sh, sc, wr, br, jnp.tril(jnp.ones((tm, tm), BF16), -1))


def _expert_kernel(te_ref, nv_ref, x_ref, wg_ref, wu_ref, wd_ref, o_ref, wg_s, wu_s, wd_s):
    i = pl.program_id(0)

    @pl.when((i == 0) | (te_ref[i] != te_ref[jnp.maximum(i - 1, 0)]))
    def _():
        wg_s[...] = wg_ref[0].astype(BF16)
        wu_s[...] = wu_ref[0].astype(BF16)
        wd_s[...] = wd_ref[0].astype(BF16)

    @pl.when(i < nv_ref[0])
    def _():
        x = x_ref[...]
        a = _dot(x, wg_s[...])
        u = _dot(x, wu_s[...])
        he = (a * jax.nn.sigmoid(a)) * u
        o_ref[...] = _dot(he.astype(BF16), wd_s[...]).astype(o_ref.dtype)


def _experts(xs, tile_expert, n_valid, wg, wu, wd, layer):
    r, d = xs.shape
    de = wg.shape[2]
    nt = r // MOE_TILE
    row = lambda i, te, nv: (jnp.minimum(i, nv[0] - 1), 0)
    wsel = lambda i, te, nv: (layer * N_EXPERTS + te[i], 0, 0)
    return pl.pallas_call(
        _expert_kernel,
        out_shape=jax.ShapeDtypeStruct((r, d), BF16),
        grid_spec=pltpu.PrefetchScalarGridSpec(
            num_scalar_prefetch=2,
            grid=(nt,),
            in_specs=[pl.BlockSpec((MOE_TILE, d), row),
                      pl.BlockSpec((1, d, de), wsel),
                      pl.BlockSpec((1, d, de), wsel),
                      pl.BlockSpec((1, de, d), wsel)],
            out_specs=pl.BlockSpec((MOE_TILE, d), row),
            scratch_shapes=[pltpu.VMEM((d, de), BF16), pltpu.VMEM((d, de), BF16), pltpu.VMEM((de, d), BF16)]),
        compiler_params=_params(("arbitrary",)),
        name="moe_experts",
    )(tile_expert, n_valid, xs, wg, wu, wd)


def _combine_kernel(x_ref, g2_ref, route_ref, y1_ref, y2_ref, fg_ref, o_ref, *, final):
    r = route_ref[...]
    w1 = r[:, 4:5]
    w2 = r[:, 5:6]
    x = x_ref[...] + g2_ref[0] * (w1 * y1_ref[...].astype(F32) + w2 * y2_ref[...].astype(F32))
    if final:
        x = (x * lax.rsqrt(jnp.mean(x * x, axis=-1, keepdims=True) + EPS)) * fg_ref[...]
    o_ref[...] = x


def _combine(x2, g2, route, y12, final_g, seq, final, tm=512):
    t, d = x2.shape
    tpb = seq // tm
    nt = t // tm
    kern = functools.partial(_combine_kernel, final=final)
    return pl.pallas_call(
        kern,
        out_shape=jax.ShapeDtypeStruct((t, d), F32),
        grid=(t // tm,),
        in_specs=[pl.BlockSpec((tm, d), lambda i: (i, 0)),
                  pl.BlockSpec((1, 1, d), lambda i: (i // tpb, 0, 0)),
                  pl.BlockSpec((tm, LANES), lambda i: (i, 0)),
                  pl.BlockSpec((tm, d), lambda i: (i, 0)),
                  pl.BlockSpec((tm, d), lambda i: (nt + i, 0)),
                  pl.BlockSpec((1, d), lambda i: (0, 0))],
        out_specs=pl.BlockSpec((tm, d), lambda i: (i, 0)),
        compiler_params=_params(("parallel",)),
        name="moe_combine",
    )(x2, g2, route, y12, y12, final_g)


def _moe(x2, g, sh, sc, g2, wg_r, bg_r, we_r, be_r, w_gate, w_up, w_down, layer, final_g, seq, final):
    t, d = x2.shape
    wr = jnp.zeros((d, LANES), F32).at[:, :N_GROUPS].set(wg_r).at[:, ROUTE_LANE0:ROUTE_LANE0 + N_EXPERTS].set(we_r)
    wr_hi = wr.astype(BF16)
    wr_lo = (wr - wr_hi.astype(F32)).astype(BF16)
    br = jnp.zeros((1, LANES), F32).at[0, :N_GROUPS].set(bg_r).at[0, ROUTE_LANE0:ROUTE_LANE0 + N_EXPERTS].set(be_r)
    h, route, cnt = _router(x2, g, sh, sc, jnp.stack([wr_hi, wr_lo]), br, seq)

    counts = cnt[0, ROUTE_LANE0:ROUTE_LANE0 + N_EXPERTS].astype(jnp.int32)
    tiles = (counts + MOE_TILE - 1) // MOE_TILE
    tile_end = jnp.cumsum(tiles)
    offs = (tile_end - tiles) * MOE_TILE
    nt = (2 * t) // MOE_TILE + N_EXPERTS
    tile_id = jnp.minimum(jnp.arange(nt, dtype=jnp.int32), tile_end[-1] - 1)
    tile_expert = jnp.sum((tile_id[:, None] >= tile_end[None, :]).astype(jnp.int32), axis=1)
    n_valid = tile_end[-1:].astype(jnp.int32)
    e12 = route[:, 0:2].astype(jnp.int32)
    offs12 = jnp.sum(jnp.where(e12[:, :, None] == jnp.arange(N_EXPERTS)[None, None, :], offs[None, None, :], 0), axis=-1)
    pos = (offs12 + route[:, 2:4].astype(jnp.int32)).T.reshape(-1)
    tok = jnp.tile(jnp.arange(t, dtype=jnp.int32), 2)
    _, tok_by_row = lax.sort_key_val(pos, tok)
    row = jnp.arange(nt * MOE_TILE, dtype=jnp.int32)
    per_row = lambda v: jnp.repeat(v[tile_expert], MOE_TILE)
    in_e = row - per_row(offs)
    compact = per_row(jnp.cumsum(counts) - counts) + in_e
    sorted_tok = jnp.where(in_e < per_row(counts),
                           tok_by_row.at[jnp.minimum(compact, 2 * t - 1)].get(mode="promise_in_bounds"), row % t)
    xs = h.at[sorted_tok].get(mode="promise_in_bounds")
    ys = _experts(xs, tile_expert, n_valid, w_gate, w_up, w_down, layer)
    y12 = ys.at[pos].get(mode="promise_in_bounds")
    return _combine(x2, g2, route, y12, final_g, seq, final)


def _rope_tables(seq, width):
    hd = DA_HEAD_DIM
    half = hd // 2
    quarter = half // 2
    pos = jnp.arange(seq)
    row = (pos // GRID_W).astype(F32)
    col = (pos % GRID_W).astype(F32)
    inv = ROPE_THETA ** (-jnp.arange(0, half, 2, dtype=F32) / half)
    i = jnp.arange(hd)
    p = jnp.where((i < half)[None, :], row[:, None], col[:, None])
    ang = p * inv[i % quarter][None, :]
    sign = jnp.where((i % half) < quarter, -1.0, 1.0)[None, :]
    reps = width // hd
    return jnp.tile(jnp.cos(ang), (1, reps)), jnp.tile(jnp.sin(ang) * sign, (1, reps))


def kernel(x, c, ctx, c_ctx, ada_w, ada_b, norm1_g, norm2_g, final_g, w_in0, w_out0, lam_q1, lam_k1, lam_q2, lam_k2, subln_g, hy_short_w, hy_short_b, hy_w1, hy_b1, hy_fr1, hy_w2, hy_b2, hy_fr2, hy_w3, hy_b3, hy_bias, cv_w1, cv_b1, cv_dw_w, cv_dw_b, cv_ln_g, cv_ln_b, cv_w2, cv_b2, moe_wg, moe_bg, moe_we, moe_be, moe_w_gate, moe_w_up, moe_w_down):
    bsz, seq, d = x.shape
    lctx = ctx.shape[1]
    depth = ada_w.shape[0]
    t = bsz * seq
    hyw = d - DA_WIDTH
    x2 = x.reshape(t, d)

    rows = ((bsz + 1 + SUBLANES - 1) // SUBLANES) * SUBLANES
    cs = jnp.zeros((rows, d), F32).at[:bsz].set(c).at[bsz].set(c_ctx)
    mods = _ada(cs, ada_w, ada_b)

    def tok_mod(i, k):
        return mods[i, :bsz, k * d:(k + 1) * d].reshape(bsz, 1, d)

    def ctx_mod(i, k):
        return mods[i, bsz:bsz + 1, k * d:(k + 1) * d].reshape(1, 1, d)

    for i in range(depth):
        j = i // 2
        g_n1 = norm1_g[i].reshape(1, d)
        if i % 2 == 0:
            assert not any(m % 2 == 0 for m in range(i + 1, depth)), "context-stream update is not implemented"
            lam_init = 0.8 - 0.6 * math.exp(-0.3 * i)
            w_in = w_in0[j]
            cw = 2 * LANES
            cos, sin = _rope_tables(seq, cw)
            q, k, v, u = _inproj(x2, g_n1, tok_mod(i, 0), tok_mod(i, 1), cos, sin, w_in.astype(BF16), seq)
            kc, vc = _ctxproj(ctx.reshape(bsz * lctx, d), g_n1, ctx_mod(i, 0), ctx_mod(i, 1),
                              w_in[:, DA_WIDTH:3 * DA_WIDTH].astype(BF16), lctx)
            lam_p = jnp.stack([lam_q1[j], lam_k1[j], lam_q2[j], lam_k2[j]])
            k_all = jnp.concatenate([kc.reshape(bsz, lctx, DA_WIDTH), k.reshape(bsz, seq, DA_WIDTH)], axis=1)
            v_all = jnp.concatenate([vc.reshape(bsz, lctx, DA_WIDTH), v.reshape(bsz, seq, DA_WIDTH)], axis=1)
            o_a = _attention(q, jnp.swapaxes(k_all, 1, 2), v_all, lam_p, subln_g[j].reshape(1, -1), seq, lam_init)

            nb = seq // LC_P
            groups = hyw // SUBLANES
            ucm = _dwconv(u.reshape(bsz, seq, 3 * hyw), hy_short_w[j], hy_short_b[j], channel_major=True)
            kc_ext = _hyena_filters(seq, hy_w1[j], hy_b1[j], hy_fr1[j], hy_w2[j], hy_b2[j], hy_fr2[j],
                                    hy_w3[j], hy_b3[j])
            bias_cm = jnp.broadcast_to(hy_bias[j].reshape(HY_ORDER * hyw, 1, 1), (HY_ORDER * hyw, 1, LC_P))
            z1 = _longconv(ucm, 0, ucm, groups, kc_ext, bias_cm, 0, groups, nb, bsz)
            z2 = _longconv(z1, 0, ucm, 2 * groups, kc_ext, bias_cm, 1, groups, nb, bsz)
            x2 = _outproj(x2, tok_mod(i, 2), o_a, z2, w_out0[j].astype(BF16), seq)
        else:
            a = _confin(x2, g_n1, tok_mod(i, 0), tok_mod(i, 1), cv_w1[j].astype(BF16), cv_b1[j], seq)
            a = _dwconv(a.reshape(bsz, seq, -1), cv_dw_w[j], cv_dw_b[j]).reshape(t, -1)
            x2 = _confout(x2, tok_mod(i, 2), a, cv_ln_g[j], cv_ln_b[j], cv_w2[j].astype(BF16), cv_b2[j], seq)
        x2 = _moe(x2, norm2_g[i].reshape(1, d), tok_mod(i, 3), tok_mod(i, 4), tok_mod(i, 5),
                  moe_wg[i], moe_bg[i], moe_we[i], moe_be[i],
                  moe_w_gate.reshape((-1,) + moe_w_gate.shape[2:]), moe_w_up.reshape((-1,) + moe_w_up.shape[2:]),
                  moe_w_down.reshape((-1,) + moe_w_down.shape[2:]), i,
                  final_g.reshape(1, d), seq, final=(i == depth - 1))
    return x2.reshape(bsz, seq, d)
```
